```python
import jax, jax.numpy as jnp
from jax import lax
import numpy as np


D_MODEL = 1024
BATCH = 8
SEQ = 2048
DEPTH = 2

MIX_WIDTH = 256
CONV_K = 3
GMLP_GROUPS = 4
GMLP_CHUNK = 128
HEAD_DIM = 64
SB_HEADS = MIX_WIDTH // HEAD_DIM
FOX_HEADS = MIX_WIDTH // HEAD_DIM
Q_BLOCK = 128
N_BRANCH = 4
FFN_HIDDEN = -(-8 * D_MODEL // (3 * 256)) * 256
EPS = 1e-6
IN_SPLIT_SIZES = (MIX_WIDTH, MIX_WIDTH, MIX_WIDTH, 2 * MIX_WIDTH, MIX_WIDTH, MIX_WIDTH, MIX_WIDTH,
                  MIX_WIDTH, MIX_WIDTH, MIX_WIDTH, FOX_HEADS, N_BRANCH * D_MODEL)
N_IN = sum(IN_SPLIT_SIZES)

kernel_name = 'hybrid_gated_conv_gmlp_stickbreak_fox'


def _rms_norm(x, g):
    xf = x.astype(jnp.float32)
    y = xf * lax.rsqrt(jnp.mean(xf * xf, axis=-1, keepdims=True) + EPS)
    return (y * g.astype(jnp.float32)).astype(x.dtype)


def _layer_norm(x, g, b):
    xf = x.astype(jnp.float32)
    mu = jnp.mean(xf, axis=-1, keepdims=True)
    var = jnp.mean(jnp.square(xf - mu), axis=-1, keepdims=True)
    y = (xf - mu) * lax.rsqrt(var + EPS)
    return (y * g.astype(jnp.float32) + b.astype(jnp.float32)).astype(x.dtype)


def _split_heads(t, n_heads):
    b, s, _ = t.shape
    return t.reshape(b, s, n_heads, HEAD_DIM).transpose(0, 2, 1, 3)


def _merge_heads(t):
    b, h, s, d = t.shape
    return t.transpose(0, 2, 1, 3).reshape(b, s, h * d)


def _short_conv_mixer(b_gate, c_gate, xt, w_conv):
    xc = c_gate * xt
    y = lax.conv_general_dilated(
        xc, w_conv[:, None, :].astype(xc.dtype), window_strides=(1,),
        padding=[(CONV_K - 1, 0)], dimension_numbers=('NWC', 'WIO', 'NWC'),
        feature_group_count=MIX_WIDTH)
    return b_gate * y


def _spatial_gating_mixer(uv, w_s, b_s, ln_g, ln_b):
    u, v = jnp.split(jax.nn.gelu(uv), 2, axis=-1)
    v = _layer_norm(v, ln_g, ln_b)
    bsz, s, _ = v.shape
    v = v.reshape(bsz, s // GMLP_CHUNK, GMLP_CHUNK, GMLP_GROUPS, MIX_WIDTH // GMLP_GROUPS)
    w_causal = jnp.tril(w_s).astype(v.dtype)
    mixed = jnp.einsum('gts,bcsge->bctge', w_causal, v) + b_s.T[None, None, :, :, None].astype(v.dtype)
    return u * mixed.reshape(bsz, s, MIX_WIDTH)


def _stick_breaking_attention(q, k, v):
    seq = q.shape[2]
    scale = HEAD_DIM ** -0.5
    outs = []
    for i in range(seq // Q_BLOCK):
        start, end = i * Q_BLOCK, (i + 1) * Q_BLOCK
        z = jnp.einsum('bhqd,bhkd->bhqk', q[:, :, start:end], k[:, :, :end]).astype(jnp.float32) * scale
        q_pos = start + jnp.arange(Q_BLOCK)[:, None]
        k_pos = jnp.arange(end)[None, :]
        strict = k_pos < q_pos
        log_1m_beta = jnp.where(strict, jax.nn.log_sigmoid(-z), 0.0)
        later = lax.cumsum(log_1m_beta, axis=3, reverse=True) - log_1m_beta
        w = jnp.where(strict, jnp.exp(jax.nn.log_sigmoid(z) + later), 0.0)
        outs.append(jnp.einsum('bhqk,bhkd->bhqd', w.astype(v.dtype), v[:, :, :end]))
    return jnp.concatenate(outs, axis=2)


def _forgetting_attention(q, k, v, log_f_cum):
    seq = q.shape[2]
    scale = HEAD_DIM ** -0.5
    neg = jnp.finfo(jnp.float32).min
    outs = []
    for i in range(seq // Q_BLOCK):
        start, end = i * Q_BLOCK, (i + 1) * Q_BLOCK
        logits = jnp.einsum('bhqd,bhkd->bhqk', q[:, :, start:end], k[:, :, :end]).astype(jnp.float32) * scale
        logits = logits + log_f_cum[:, :, start:end, None] - log_f_cum[:, :, None, :end]
        causal = jnp.arange(end)[None, :] <= (start + jnp.arange(Q_BLOCK)[:, None])
        p = jax.nn.softmax(jnp.where(causal, logits, neg), axis=-1)
        outs.append(jnp.einsum('bhqk,bhkd->bhqd', p.astype(v.dtype), v[:, :, :end]))
    return jnp.concatenate(outs, axis=2)


def _mixer_block(xn, w_in, w_conv, w_s, b_s, ln_g, ln_b, q_norm_g, k_norm_g, b_f, w_branch, w_out):
    bsz, s, _ = xn.shape
    proj = xn @ w_in
    idx, acc = [], 0
    for size in IN_SPLIT_SIZES[:-1]:
        acc += size
        idx.append(acc)
    (cb, cc, cx, uv, sb_q, sb_k, sb_v, fx_q, fx_k, fx_v, f_raw, gate_raw) = jnp.split(proj, idx, axis=-1)

    y_a = _short_conv_mixer(cb, cc, cx, w_conv)
    y_b = _spatial_gating_mixer(uv, w_s, b_s, ln_g, ln_b)
    y_c = _merge_heads(_stick_breaking_attention(
        _split_heads(sb_q, SB_HEADS), _split_heads(sb_k, SB_HEADS), _split_heads(sb_v, SB_HEADS)))
    fq = _rms_norm(_split_heads(fx_q, FOX_HEADS), q_norm_g)
    fk = _rms_norm(_split_heads(fx_k, FOX_HEADS), k_norm_g)
    log_f = jax.nn.log_sigmoid((f_raw + b_f).astype(jnp.float32)).transpose(0, 2, 1)
    y_d = _merge_heads(_forgetting_attention(fq, fk, _split_heads(fx_v, FOX_HEADS), lax.cumsum(log_f, axis=2)))

    ys = jnp.stack([y_a, y_b, y_c, y_d], axis=2)
    branches = jnp.einsum('bsnc,ncd->bsnd', ys, w_branch)
    gates = jax.nn.sigmoid(gate_raw.reshape(bsz, s, N_BRANCH, D_MODEL).astype(jnp.float32)).astype(xn.dtype)
    merged = jnp.sum(gates * branches, axis=2)
    return merged @ w_out


def _swiglu(x, w_in, w_out):
    g, u = jnp.split(x @ w_in, 2, axis=-1)
    return (jax.nn.silu(g) * u) @ w_out


def setup_inputs(seed: int = 0) -> dict:
    key = jax.random.key(seed)
    ks = jax.random.split(key, 16)
    f32 = jnp.float32
    nrm = lambda k, shape: jax.random.normal(k, shape, f32)
    return {
        'x': nrm(ks[0], (BATCH, SEQ, D_MODEL)),
        'norm_mix_g': 1.0 + 0.05 * nrm(ks[1], (DEPTH, D_MODEL)),
        'w_in': nrm(ks[2], (DEPTH, D_MODEL, N_IN)) * D_MODEL ** -0.5,
        'w_conv': nrm(ks[3], (DEPTH, CONV_K, MIX_WIDTH)) * CONV_K ** -0.5,
        'w_spatial': nrm(ks[4], (DEPTH, GMLP_GROUPS, GMLP_CHUNK, GMLP_CHUNK)) * GMLP_CHUNK ** -0.5,
        'b_spatial': 1.0 + 0.1 * nrm(ks[5], (DEPTH, GMLP_GROUPS, GMLP_CHUNK)),
        'gmlp_ln_g': 1.0 + 0.05 * nrm(ks[6], (DEPTH, MIX_WIDTH)),
        'gmlp_ln_b': 0.02 * nrm(ks[7], (DEPTH, MIX_WIDTH)),
        'fox_q_norm_g': 1.0 + 0.05 * nrm(ks[8], (DEPTH, HEAD_DIM)),
        'fox_k_norm_g': 1.0 + 0.05 * nrm(ks[9], (DEPTH, HEAD_DIM)),
        'fox_forget_b': 2.0 + 0.5 * nrm(ks[10], (DEPTH, FOX_HEADS)),
        'w_branch': nrm(ks[11], (DEPTH, N_BRANCH, MIX_WIDTH, D_MODEL)) * MIX_WIDTH ** -0.5,
        'w_out': nrm(ks[12], (DEPTH, D_MODEL, D_MODEL)) * D_MODEL ** -0.5,
        'norm_ffn_g': 1.0 + 0.05 * nrm(ks[13], (DEPTH, D_MODEL)),
        'w_ffn_in': nrm(ks[14], (DEPTH, D_MODEL, 2 * FFN_HIDDEN)) * D_MODEL ** -0.5,
        'w_ffn_out': nrm(ks[15], (DEPTH, FFN_HIDDEN, D_MODEL)) * FFN_HIDDEN ** -0.5,
    }


def reference(x, norm_mix_g, w_in, w_conv, w_spatial, b_spatial, gmlp_ln_g, gmlp_ln_b,
              fox_q_norm_g, fox_k_norm_g, fox_forget_b, w_branch, w_out, norm_ffn_g,
              w_ffn_in, w_ffn_out):
    h = x
    for l in range(DEPTH):
        xn = _rms_norm(h, norm_mix_g[l])
        h = h + _mixer_block(xn, w_in[l], w_conv[l], w_spatial[l], b_spatial[l], gmlp_ln_g[l],
                             gmlp_ln_b[l], fox_q_norm_g[l], fox_k_norm_g[l], fox_forget_b[l],
                             w_branch[l], w_out[l])
        h = h + _swiglu(_rms_norm(h, norm_ffn_g[l]), w_ffn_in[l], w_ffn_out[l])
    return h
```

```python
import functools

import jax
import jax.numpy as jnp
from jax import lax
from jax.experimental import pallas as pl
from jax.experimental.pallas import tpu as pltpu

D_MODEL = 1024
MIX_WIDTH = 256
HEAD_DIM = 64
N_HEADS = MIX_WIDTH // HEAD_DIM
N_BRANCH = 4
CONV_K = 3
GMLP_GROUPS = 4
GMLP_CHUNK = 128
GROUP_WIDTH = MIX_WIDTH // GMLP_GROUPS
FFN_HIDDEN = 2816
EPS = 1e-6
N_MAIN = 11 * MIX_WIDTH
LANES = 128
SUBLANES = 8
VMEM_LIMIT_BYTES = 56 * 1024 * 1024

TOKEN_TILE = 512
COL_CHUNK = 256
ATTN_TQ = 256
ATTN_TK = 256
SCALE = HEAD_DIM ** -0.5
NEG_BIG = -1e30

F32 = jnp.float32
BF16 = jnp.bfloat16


def _dot(a, b):
    return jnp.dot(a, b, preferred_element_type=F32)


def _dot_nt(a, b):
    return lax.dot_general(a, b, (((1,), (1,)), ((), ())), preferred_element_type=F32)


def _split_hi_lo(x):
    hi = x.astype(BF16)
    lo = (x - hi.astype(F32)).astype(BF16)
    return hi, lo


def _rms_norm_rows(x, g):
    ms = jnp.mean(x * x, axis=-1, keepdims=True)
    return x * lax.rsqrt(ms + EPS) * g


def _resident(shape):
    return pl.BlockSpec(shape, lambda *_: (0,) * len(shape), pipeline_mode=pl.Buffered(1))


def _params(n_axes):
    return pltpu.CompilerParams(dimension_semantics=("arbitrary",) * n_axes,
                                vmem_limit_bytes=VMEM_LIMIT_BYTES)


def _proj_kernel(h_ref, g_ref, wm_ref, wf_ref, main_ref, f_ref):
    xn = _rms_norm_rows(h_ref[...], g_ref[...]).astype(BF16)
    for c in range(N_MAIN // COL_CHUNK):
        cs = slice(c * COL_CHUNK, (c + 1) * COL_CHUNK)
        main_ref[:, cs] = _dot(xn, wm_ref[:, cs]).astype(BF16)
    f_ref[...] = _dot(xn, wf_ref[...])


def _proj(h, g, w_main, w_f):
    t = h.shape[0]
    return pl.pallas_call(
        _proj_kernel,
        grid=(t // TOKEN_TILE,),
        in_specs=[pl.BlockSpec((TOKEN_TILE, D_MODEL), lambda i: (i, 0)),
                  _resident((1, D_MODEL)),
                  _resident((D_MODEL, N_MAIN)),
                  _resident((D_MODEL, LANES))],
        out_specs=[pl.BlockSpec((TOKEN_TILE, N_MAIN), lambda i: (i, 0)),
                   pl.BlockSpec((TOKEN_TILE, LANES), lambda i: (i, 0))],
        out_shape=[jax.ShapeDtypeStruct((t, N_MAIN), BF16),
                   jax.ShapeDtypeStruct((t, LANES), F32)],
        compiler_params=_params(1),
        name="proj",
    )(h, g, w_main, w_f)


def _convgmlp_kernel(cb_ref, cc_ref, cx_ref, u_ref, v_ref, wconv_ref, ws_ref, bs_ref,
                     lng_ref, lnb_ref, o_ref):
    seq = cb_ref.shape[0]
    n_chunks = seq // GMLP_CHUNK
    row = lax.broadcasted_iota(jnp.int32, (GMLP_CHUNK, GMLP_CHUNK), 0)
    col = lax.broadcasted_iota(jnp.int32, (GMLP_CHUNK, GMLP_CHUNK), 1)
    w_tril = [jnp.where(col <= row, ws_ref[gi], 0.0).astype(BF16) for gi in range(GMLP_GROUPS)]
    lane = lax.broadcasted_iota(jnp.int32, (GMLP_CHUNK, LANES), 1)
    first_group = lane < GROUP_WIDTH
    w0 = wconv_ref[0:1, :]
    w1 = wconv_ref[1:2, :]
    w2 = wconv_ref[2:3, :]

    def chunk(c, prev_tail):
        rows = pl.ds(pl.multiple_of(c * GMLP_CHUNK, GMLP_CHUNK), GMLP_CHUNK)
        xc = cc_ref[rows, :].astype(F32) * cx_ref[rows, :].astype(F32)
        win = jnp.concatenate([prev_tail, xc], axis=0)
        xc1 = pltpu.roll(win, 1, 0)[SUBLANES:, :]
        xc2 = pltpu.roll(win, 2, 0)[SUBLANES:, :]
        ya = cb_ref[rows, :].astype(F32) * (w0 * xc2 + w1 * xc1 + w2 * xc)
        o_ref[rows, 0:MIX_WIDTH] = ya.astype(BF16)

        gu = jax.nn.gelu(u_ref[rows, :].astype(F32))
        gv = jax.nn.gelu(v_ref[rows, :].astype(F32))
        mu = jnp.mean(gv, axis=-1, keepdims=True)
        cen = gv - mu
        var = jnp.mean(cen * cen, axis=-1, keepdims=True)
        vn = (cen * lax.rsqrt(var + EPS) * lng_ref[...] + lnb_ref[...]).astype(BF16)
        halves = []
        for lb in range(MIX_WIDTH // LANES):
            vb = vn[:, lb * LANES:(lb + 1) * LANES]
            m0 = _dot(w_tril[2 * lb], vb)
            m1 = _dot(w_tril[2 * lb + 1], vb)
            halves.append(jnp.where(first_group, m0, m1))
        mixed = jnp.concatenate(halves, axis=1) + bs_ref[...]
        o_ref[rows, MIX_WIDTH:2 * MIX_WIDTH] = (gu * mixed).astype(BF16)
        return xc[GMLP_CHUNK - SUBLANES:, :]

    lax.fori_loop(0, n_chunks, chunk, jnp.zeros((SUBLANES, MIX_WIDTH), F32))


def _convgmlp(main3, w_conv, w_s, bs_rows, ln_g, ln_b):
    b, s, _ = main3.shape
    col = lambda k: pl.BlockSpec((None, s, MIX_WIDTH), lambda i, k=k: (i, 0, k))
    return pl.pallas_call(
        _convgmlp_kernel,
        grid=(b,),
        in_specs=[col(0), col(1), col(2), col(3), col(4),
                  _resident((CONV_K, MIX_WIDTH)),
                  _resident((GMLP_GROUPS, GMLP_CHUNK, GMLP_CHUNK)),
                  _resident((GMLP_CHUNK, MIX_WIDTH)),
                  _resident((1, MIX_WIDTH)),
                  _resident((1, MIX_WIDTH))],
        out_specs=pl.BlockSpec((None, s, 2 * MIX_WIDTH), lambda i: (i, 0, 0)),
        out_shape=jax.ShapeDtypeStruct((b, s, 2 * MIX_WIDTH), BF16),
        compiler_params=_params(1),
        name="convgmlp",
    )(main3, main3, main3, main3, main3, w_conv, w_s, bs_rows, ln_g, ln_b)


def _pair(x, h):
    p = (h * HEAD_DIM) // LANES
    return x[:, p * LANES:(p + 1) * LANES]


def _head_in_pair_mask(rows, h):
    lane = lax.broadcasted_iota(jnp.int32, (rows, LANES), 1)
    first = lane < HEAD_DIM
    return first if (h * HEAD_DIM) % LANES == 0 else jnp.logical_not(first)


def _masked_head(x, h):
    xp = _pair(x, h)
    return jnp.where(_head_in_pair_mask(x.shape[0], h), xp, jnp.zeros_like(xp))


def _merge_heads(per_head):
    rows = per_head[0].shape[0]
    blocks = []
    for p in range(MIX_WIDTH // LANES):
        h0 = p * (LANES // HEAD_DIM)
        blocks.append(jnp.where(_head_in_pair_mask(rows, h0), per_head[h0], per_head[h0 + 1]))
    return jnp.concatenate(blocks, axis=1)


def _sb_kernel(q_ref, k_ref, v_ref, o_ref):
    i = pl.program_id(1)
    q = q_ref[pl.ds(pl.multiple_of(i * ATTN_TQ, ATTN_TQ), ATTN_TQ), :]
    qs = [_masked_head(q, h) * SCALE for h in range(N_HEADS)]
    r = lax.broadcasted_iota(jnp.int32, (ATTN_TK, ATTN_TK), 0)
    c = lax.broadcasted_iota(jnp.int32, (ATTN_TK, ATTN_TK), 1)
    suffix = jnp.where(r > c, 1.0, 0.0).astype(BF16)
    strict = c < r

    def block(j, state, diag):
        ks = pl.ds(pl.multiple_of(j * ATTN_TK, ATTN_TK), ATTN_TK)
        k = k_ref[ks, :]
        v = v_ref[ks, :]
        new = []
        for h in range(N_HEADS):
            carry, acc = state[h]
            z = _dot_nt(qs[h], _pair(k, h))
            l = -(jnp.maximum(z, 0.0) + jnp.log1p(jnp.exp(-jnp.abs(z))))
            if diag:
                l = jnp.where(strict, l, 0.0)
            hi, lo = _split_hi_lo(l)
            later = _dot(hi, suffix) + _dot(lo, suffix) + carry
            w = jnp.exp(z + l + later)
            if diag:
                w = jnp.where(strict, w, 0.0)
            acc = acc + _dot(w.astype(BF16), _pair(v, h))
            carry = carry + jnp.sum(l, axis=1, keepdims=True)
            new.append((carry, acc))
        return tuple(new)

    init = tuple((jnp.zeros((ATTN_TQ, 1), F32), jnp.zeros((ATTN_TQ, LANES), F32))
                 for _ in range(N_HEADS))
    state = block(i, init, True)
    state = lax.fori_loop(0, i, lambda t, st: block(i - 1 - t, st, False), state)
    o_ref[...] = _merge_heads([state[h][1] for h in range(N_HEADS)]).astype(BF16)


def _sb_attention(main3):
    b, s, _ = main3.shape
    col = lambda k: pl.BlockSpec((None, s, MIX_WIDTH), lambda bi, i, k=k: (bi, 0, k))
    return pl.pallas_call(
        _sb_kernel,
        grid=(b, s // ATTN_TQ),
        in_specs=[col(5), col(6), col(7)],
        out_specs=pl.BlockSpec((None, ATTN_TQ, MIX_WIDTH), lambda bi, i: (bi, i, 0)),
        out_shape=jax.ShapeDtypeStruct((b, s, MIX_WIDTH), BF16),
        compiler_params=_params(2),
        name="sb",
    )(main3, main3, main3)


def _fox_kernel(q_ref, k_ref, v_ref, f_ref, bf_ref, qg_ref, kg_ref, o_ref,
                qn_scr, kn_scr, ccol_scr, crow_scr):
    i = pl.program_id(1)
    seq = q_ref.shape[0]

    @pl.when(i == 0)
    def _prologue():
        r = lax.broadcasted_iota(jnp.int32, (MIX_WIDTH, MIX_WIDTH), 0) // HEAD_DIM
        c = lax.broadcasted_iota(jnp.int32, (MIX_WIDTH, MIX_WIDTH), 1) // HEAD_DIM
        same_head = jnp.where(r == c, 1.0, 0.0).astype(BF16)

        def norm_rows(x_ref, g, scale, dst, rows):
            x = x_ref[rows, :].astype(F32)
            hi, lo = _split_hi_lo(x * x)
            ms = (_dot(hi, same_head) + _dot(lo, same_head)) * (1.0 / HEAD_DIM)
            dst[rows, :] = (x * lax.rsqrt(ms + EPS) * g * scale).astype(BF16)

        tr = lax.broadcasted_iota(jnp.int32, (LANES, LANES), 0)
        tc = lax.broadcasted_iota(jnp.int32, (LANES, LANES), 1)
        prefix = jnp.where(tc <= tr, 1.0, 0.0).astype(BF16)

        def cum_block(blk, run):
            rows = pl.ds(pl.multiple_of(blk * LANES, LANES), LANES)
            norm_rows(q_ref, qg_ref[...], SCALE, qn_scr, rows)
            norm_rows(k_ref, kg_ref[...], 1.0, kn_scr, rows)
            logf = jax.nn.log_sigmoid(f_ref[rows, :] + bf_ref[...])
            hi = logf.astype(BF16)
            rem = logf - hi.astype(F32)
            mid = rem.astype(BF16)
            lo = (rem - mid.astype(F32)).astype(BF16)
            cblk = _dot(prefix, hi) + _dot(prefix, mid) + _dot(prefix, lo) + run
            ccol_scr[rows, :] = cblk
            crow_scr[:, rows] = cblk.T[0:SUBLANES, :]
            return cblk[LANES - 1:LANES, :]

        lax.fori_loop(0, seq // LANES, cum_block, jnp.zeros((1, LANES), F32))

    qrows = pl.ds(pl.multiple_of(i * ATTN_TQ, ATTN_TQ), ATTN_TQ)
    q = qn_scr[qrows, :]
    qs = [_masked_head(q, h) for h in range(N_HEADS)]
    cq = ccol_scr[qrows, :]
    ct = [cq[:, h:h + 1] for h in range(N_HEADS)]
    r = lax.broadcasted_iota(jnp.int32, (ATTN_TQ, ATTN_TK), 0)
    c = lax.broadcasted_iota(jnp.int32, (ATTN_TQ, ATTN_TK), 1)
    causal = c <= r

    def block(j, state, diag):
        ks = pl.ds(pl.multiple_of(j * ATTN_TK, ATTN_TK), ATTN_TK)
        k = kn_scr[ks, :]
        v = v_ref[ks, :]
        new = []
        for h in range(N_HEADS):
            m, l, acc = state[h]
            s = _dot_nt(qs[h], _pair(k, h)) + (ct[h] - crow_scr[h:h + 1, ks])
            if diag:
                s = jnp.where(causal, s, NEG_BIG)
            m_new = jnp.maximum(m, jnp.max(s, axis=1, keepdims=True))
            alpha = jnp.exp(m - m_new)
            p = jnp.exp(s - m_new)
            l = alpha * l + jnp.sum(p, axis=1, keepdims=True)
            acc = alpha * acc + _dot(p.astype(BF16), _pair(v, h))
            new.append((m_new, l, acc))
        return tuple(new)

    init = tuple((jnp.full((ATTN_TQ, 1), NEG_BIG, F32), jnp.zeros((ATTN_TQ, 1), F32),
                  jnp.zeros((ATTN_TQ, LANES), F32)) for _ in range(N_HEADS))
    state = block(i, init, True)
    state = lax.fori_loop(0, i, lambda t, st: block(i - 1 - t, st, False), state)
    o_ref[...] = _merge_heads([state[h][2] / state[h][1] for h in range(N_HEADS)]).astype(BF16)


def _fox_attention(main3, f3, bf_row, qg_row, kg_row):
    b, s, _ = main3.shape
    col = lambda k: pl.BlockSpec((None, s, MIX_WIDTH), lambda bi, i, k=k: (bi, 0, k))
    return pl.pallas_call(
        _fox_kernel,
        grid=(b, s // ATTN_TQ),
        in_specs=[col(8), col(9), col(10),
                  pl.BlockSpec((None, s, LANES), lambda bi, i: (bi, 0, 0)),
                  _resident((1, LANES)),
                  _resident((1, MIX_WIDTH)),
                  _resident((1, MIX_WIDTH))],
        out_specs=pl.BlockSpec((None, ATTN_TQ, MIX_WIDTH), lambda bi, i: (bi, i, 0)),
        out_shape=jax.ShapeDtypeStruct((b, s, MIX_WIDTH), BF16),
        scratch_shapes=[pltpu.VMEM((s, MIX_WIDTH), BF16),
                        pltpu.VMEM((s, MIX_WIDTH), BF16),
                        pltpu.VMEM((s, LANES), F32),
                        pltpu.VMEM((SUBLANES, s), F32)],
        compiler_params=_params(2),
        name="fox",
    )(main3, main3, main3, f3, bf_row, qg_row, kg_row)


def _merge_kernel(h_ref, g_ref, yab_ref, yc_ref, yd_ref, wg_ref, wb_ref, wo_ref, o_ref, merged_scr):
    xn = _rms_norm_rows(h_ref[...], g_ref[...]).astype(BF16)
    ys = [yab_ref[:, 0:MIX_WIDTH], yab_ref[:, MIX_WIDTH:2 * MIX_WIDTH], yc_ref[...], yd_ref[...]]
    for c in range(D_MODEL // COL_CHUNK):
        cs = slice(c * COL_CHUNK, (c + 1) * COL_CHUNK)
        acc = None
        for n in range(N_BRANCH):
            gs = slice(n * D_MODEL + c * COL_CHUNK, n * D_MODEL + (c + 1) * COL_CHUNK)
            term = jax.nn.sigmoid(_dot(xn, wg_ref[:, gs])) * _dot(ys[n], wb_ref[n, :, cs])
            acc = term if acc is None else acc + term
        merged_scr[:, cs] = acc.astype(BF16)
    merged = merged_scr[...]
    for c in range(D_MODEL // COL_CHUNK):
        cs = slice(c * COL_CHUNK, (c + 1) * COL_CHUNK)
        o_ref[:, cs] = h_ref[:, cs] + _dot(merged, wo_ref[:, cs])


def _merge(h, g, yab, yc, yd, w_gate, w_branch, w_out):
    t = h.shape[0]
    rows = lambda w: pl.BlockSpec((TOKEN_TILE, w), lambda i: (i, 0))
    return pl.pallas_call(
        _merge_kernel,
        grid=(t // TOKEN_TILE,),
        in_specs=[rows(D_MODEL), _resident((1, D_MODEL)),
                  rows(2 * MIX_WIDTH), rows(MIX_WIDTH), rows(MIX_WIDTH),
                  _resident((D_MODEL, N_BRANCH * D_MODEL)),
                  _resident((N_BRANCH, MIX_WIDTH, D_MODEL)),
                  _resident((D_MODEL, D_MODEL))],
        out_specs=rows(D_MODEL),
        out_shape=jax.ShapeDtypeStruct((t, D_MODEL), F32),
        scratch_shapes=[pltpu.VMEM((TOKEN_TILE, D_MODEL), BF16)],
        compiler_params=_params(1),
        name="merge",
    )(h, g, yab, yc, yd, w_gate, w_branch, w_out)


def _ffn_kernel(h_ref, g_ref, wi_ref, wo_ref, o_ref, acc_scr):
    xn = _rms_norm_rows(h_ref[...], g_ref[...]).astype(BF16)
    for c in range(FFN_HIDDEN // COL_CHUNK):
        gate = _dot(xn, wi_ref[:, c * COL_CHUNK:(c + 1) * COL_CHUNK])
        up = _dot(xn, wi_ref[:, FFN_HIDDEN + c * COL_CHUNK:FFN_HIDDEN + (c + 1) * COL_CHUNK])
        act = (jax.nn.silu(gate) * up).astype(BF16)
        part = _dot(act, wo_ref[c * COL_CHUNK:(c + 1) * COL_CHUNK, :])
        if c == 0:
            acc_scr[...] = part
        else:
            acc_scr[...] += part
    o_ref[...] = h_ref[...] + acc_scr[...]


def _ffn(h, g, w_in, w_out):
    t = h.shape[0]
    rows = pl.BlockSpec((TOKEN_TILE, D_MODEL), lambda i: (i, 0))
    return pl.pallas_call(
        _ffn_kernel,
        grid=(t // TOKEN_TILE,),
        in_specs=[rows, _resident((1, D_MODEL)),
                  _resident((D_MODEL, 2 * FFN_HIDDEN)),
                  _resident((FFN_HIDDEN, D_MODEL))],
        out_specs=rows,
        out_shape=jax.ShapeDtypeStruct((t, D_MODEL), F32),
        scratch_shapes=[pltpu.VMEM((TOKEN_TILE, D_MODEL), F32)],
        compiler_params=_params(1),
        name="ffn",
    )(h, g, w_in, w_out)


def kernel(x, norm_mix_g, w_in, w_conv, w_spatial, b_spatial, gmlp_ln_g, gmlp_ln_b,
           fox_q_norm_g, fox_k_norm_g, fox_forget_b, w_branch, w_out, norm_ffn_g,
           w_ffn_in, w_ffn_out):
    b, s, d = x.shape
    depth = w_in.shape[0]
    assert d == D_MODEL and s % max(ATTN_TQ, ATTN_TK, GMLP_CHUNK) == 0 and (b * s) % TOKEN_TILE == 0
    t = b * s
    h = x.reshape(t, d)
    for l in range(depth):
        w_main = w_in[l, :, :N_MAIN].astype(BF16)
        w_f = jnp.pad(w_in[l, :, N_MAIN:N_MAIN + N_HEADS], ((0, 0), (0, LANES - N_HEADS))).astype(BF16)
        w_gate = w_in[l, :, N_MAIN + N_HEADS:].astype(BF16)
        bf_row = jnp.pad(fox_forget_b[l], (0, LANES - N_HEADS)).reshape(1, LANES)
        qg_row = jnp.tile(fox_q_norm_g[l], N_HEADS).reshape(1, MIX_WIDTH)
        kg_row = jnp.tile(fox_k_norm_g[l], N_HEADS).reshape(1, MIX_WIDTH)
        bs_rows = jnp.repeat(b_spatial[l].T, GROUP_WIDTH, axis=1)

        main, f_raw = _proj(h, norm_mix_g[l].reshape(1, d), w_main, w_f)
        main3 = main.reshape(b, s, N_MAIN)
        yab = _convgmlp(main3, w_conv[l], w_spatial[l], bs_rows,
                        gmlp_ln_g[l].reshape(1, MIX_WIDTH), gmlp_ln_b[l].reshape(1, MIX_WIDTH))
        yc = _sb_attention(main3)
        yd = _fox_attention(main3, f_raw.reshape(b, s, LANES), bf_row, qg_row, kg_row)
        h = _merge(h, norm_mix_g[l].reshape(1, d), yab.reshape(t, 2 * MIX_WIDTH),
                   yc.reshape(t, MIX_WIDTH), yd.reshape(t, MIX_WIDTH),
                   w_gate, w_branch[l].astype(BF16), w_out[l].astype(BF16))
        h = _ffn(h, norm_ffn_g[l].reshape(1, d), w_ffn_in[l].astype(BF16), w_ffn_out[l].astype(BF16))
    return h.reshape(b, s, d)
```

```python
import functools

import jax
import jax.numpy as jnp
from jax import lax
from jax.experimental import pallas as pl
from jax.experimental.pallas import tpu as pltpu

D_MODEL = 1024
MIX_WIDTH = 256
HEAD_DIM = 64
N_HEADS = MIX_WIDTH // HEAD_DIM
N_BRANCH = 4
CONV_K = 3
GMLP_GROUPS = 4
GMLP_CHUNK = 128
GROUP_WIDTH = MIX_WIDTH // GMLP_GROUPS
FFN_HIDDEN = 2816
EPS = 1e-6
N_MAIN = 11 * MIX_WIDTH
LANES = 128
HEADS_PER_PAIR = LANES // HEAD_DIM
N_PAIRS = MIX_WIDTH // LANES
SUBLANES = 8
VMEM_LIMIT_BYTES = 56 * 1024 * 1024

TOKEN_TILE = 512
COL_CHUNK = 256
ATTN_TQ = 256
ATTN_TK = 256
SCALE = HEAD_DIM ** -0.5
LOG2_E = 1.4426950408889634
NEG_BIG = -1e30

F32 = jnp.float32
BF16 = jnp.bfloat16


def _dot(a, b):
    return jnp.dot(a, b, preferred_element_type=F32)


def _dot_nt(a, b):
    return lax.dot_general(a, b, (((1,), (1,)), ((), ())), preferred_element_type=F32)


def _split_hi_lo(x):
    hi = x.astype(BF16)
    lo = (x - hi.astype(F32)).astype(BF16)
    return hi, lo


def _rms_norm_rows(x, g):
    ms = jnp.mean(x * x, axis=-1, keepdims=True)
    return x * lax.rsqrt(ms + EPS) * g


def _resident(shape):
    return pl.BlockSpec(shape, lambda *_: (0,) * len(shape), pipeline_mode=pl.Buffered(1))


def _params(n_axes):
    return pltpu.CompilerParams(dimension_semantics=("arbitrary",) * n_axes,
                                vmem_limit_bytes=VMEM_LIMIT_BYTES)


def _proj_kernel(h_ref, g_ref, wm_ref, wf_ref, main_ref, f_ref):
    xn = _rms_norm_rows(h_ref[...], g_ref[...]).astype(BF16)
    for c in range(N_MAIN // COL_CHUNK):
        cs = slice(c * COL_CHUNK, (c + 1) * COL_CHUNK)
        main_ref[:, cs] = _dot(xn, wm_ref[:, cs]).astype(BF16)
    f_ref[...] = _dot(xn, wf_ref[...])


def _proj(h, g, w_main, w_f):
    t = h.shape[0]
    return pl.pallas_call(
        _proj_kernel,
        grid=(t // TOKEN_TILE,),
        in_specs=[pl.BlockSpec((TOKEN_TILE, D_MODEL), lambda i: (i, 0)),
                  _resident((1, D_MODEL)),
                  _resident((D_MODEL, N_MAIN)),
                  _resident((D_MODEL, LANES))],
        out_specs=[pl.BlockSpec((TOKEN_TILE, N_MAIN), lambda i: (i, 0)),
                   pl.BlockSpec((TOKEN_TILE, LANES), lambda i: (i, 0))],
        out_shape=[jax.ShapeDtypeStruct((t, N_MAIN), BF16),
                   jax.ShapeDtypeStruct((t, LANES), F32)],
        compiler_params=_params(1),
        name="proj",
    )(h, g, w_main, w_f)


def _convgmlp_kernel(cb_ref, cc_ref, cx_ref, u_ref, v_ref, wconv_ref, ws_ref, bs_ref,
                     lng_ref, lnb_ref, o_ref):
    seq = cb_ref.shape[0]
    n_chunks = seq // GMLP_CHUNK
    row = lax.broadcasted_iota(jnp.int32, (GMLP_CHUNK, GMLP_CHUNK), 0)
    col = lax.broadcasted_iota(jnp.int32, (GMLP_CHUNK, GMLP_CHUNK), 1)
    w_tril = [jnp.where(col <= row, ws_ref[gi], 0.0).astype(BF16) for gi in range(GMLP_GROUPS)]
    lane = lax.broadcasted_iota(jnp.int32, (GMLP_CHUNK, LANES), 1)
    first_group = lane < GROUP_WIDTH
    w0 = wconv_ref[0:1, :]
    w1 = wconv_ref[1:2, :]
    w2 = wconv_ref[2:3, :]

    def chunk(c, prev_tail):
        rows = pl.ds(pl.multiple_of(c * GMLP_CHUNK, GMLP_CHUNK), GMLP_CHUNK)
        xc = cc_ref[rows, :].astype(F32) * cx_ref[rows, :].astype(F32)
        win = jnp.concatenate([prev_tail, xc], axis=0)
        xc1 = pltpu.roll(win, 1, 0)[SUBLANES:, :]
        xc2 = pltpu.roll(win, 2, 0)[SUBLANES:, :]
        ya = cb_ref[rows, :].astype(F32) * (w0 * xc2 + w1 * xc1 + w2 * xc)
        o_ref[rows, 0:MIX_WIDTH] = ya.astype(BF16)

        gu = jax.nn.gelu(u_ref[rows, :].astype(F32))
        gv = jax.nn.gelu(v_ref[rows, :].astype(F32))
        mu = jnp.mean(gv, axis=-1, keepdims=True)
        cen = gv - mu
        var = jnp.mean(cen * cen, axis=-1, keepdims=True)
        vn = (cen * lax.rsqrt(var + EPS) * lng_ref[...] + lnb_ref[...]).astype(BF16)
        halves = []
        for lb in range(MIX_WIDTH // LANES):
            vb = vn[:, lb * LANES:(lb + 1) * LANES]
            m0 = _dot(w_tril[2 * lb], vb)
            m1 = _dot(w_tril[2 * lb + 1], vb)
            halves.append(jnp.where(first_group, m0, m1))
        mixed = jnp.concatenate(halves, axis=1) + bs_ref[...]
        o_ref[rows, MIX_WIDTH:2 * MIX_WIDTH] = (gu * mixed).astype(BF16)
        return xc[GMLP_CHUNK - SUBLANES:, :]

    lax.fori_loop(0, n_chunks, chunk, jnp.zeros((SUBLANES, MIX_WIDTH), F32))


def _convgmlp(main3, w_conv, w_s, bs_rows, ln_g, ln_b):
    b, s, _ = main3.shape
    col = lambda k: pl.BlockSpec((None, s, MIX_WIDTH), lambda i, k=k: (i, 0, k))
    return pl.pallas_call(
        _convgmlp_kernel,
        grid=(b,),
        in_specs=[col(0), col(1), col(2), col(3), col(4),
                  _resident((CONV_K, MIX_WIDTH)),
                  _resident((GMLP_GROUPS, GMLP_CHUNK, GMLP_CHUNK)),
                  _resident((GMLP_CHUNK, MIX_WIDTH)),
                  _resident((1, MIX_WIDTH)),
                  _resident((1, MIX_WIDTH))],
        out_specs=pl.BlockSpec((None, s, 2 * MIX_WIDTH), lambda i: (i, 0, 0)),
        out_shape=jax.ShapeDtypeStruct((b, s, 2 * MIX_WIDTH), BF16),
        compiler_params=_params(1),
        name="convgmlp",
    )(main3, main3, main3, main3, main3, w_conv, w_s, bs_rows, ln_g, ln_b)


def _pair(x, h):
    p = (h * HEAD_DIM) // LANES
    return x[:, p * LANES:(p + 1) * LANES]


def _head_in_pair_mask(rows, h):
    lane = lax.broadcasted_iota(jnp.int32, (rows, LANES), 1)
    first = lane < HEAD_DIM
    return first if (h * HEAD_DIM) % LANES == 0 else jnp.logical_not(first)


def _masked_head(x, h):
    xp = _pair(x, h)
    return jnp.where(_head_in_pair_mask(x.shape[0], h), xp, jnp.zeros_like(xp))


def _merge_heads(per_head):
    rows = per_head[0].shape[0]
    blocks = []
    for p in range(MIX_WIDTH // LANES):
        h0 = p * (LANES // HEAD_DIM)
        blocks.append(jnp.where(_head_in_pair_mask(rows, h0), per_head[h0], per_head[h0 + 1]))
    return jnp.concatenate(blocks, axis=1)


def _sb_kernel(q_ref, k_ref, v_ref, o_ref):
    i = pl.program_id(1)
    q = q_ref[pl.ds(pl.multiple_of(i * ATTN_TQ, ATTN_TQ), ATTN_TQ), :]
    qneg = [jnp.concatenate([_masked_head(q, HEADS_PER_PAIR * p + e) for e in range(HEADS_PER_PAIR)],
                            axis=0) * (-SCALE) for p in range(N_PAIRS)]
    r = lax.broadcasted_iota(jnp.int32, (ATTN_TK, ATTN_TK), 0)
    c = lax.broadcasted_iota(jnp.int32, (ATTN_TK, ATTN_TK), 1)
    suffix = jnp.where(r > c, 1.0, 0.0).astype(BF16)
    rows_all = N_HEADS * ATTN_TQ
    rq = lax.broadcasted_iota(jnp.int32, (rows_all, ATTN_TK), 0) & (ATTN_TQ - 1)
    cq = lax.broadcasted_iota(jnp.int32, (rows_all, ATTN_TK), 1)
    strict = cq < rq

    pair_rows = HEADS_PER_PAIR * ATTN_TQ

    def blocks(js, state, diag):
        carry, acc = state
        staged = []
        for j in js:
            ks = pl.ds(pl.multiple_of(j * ATTN_TK, ATTN_TK), ATTN_TK)
            k = k_ref[ks, :]
            zn = jnp.concatenate(
                [_dot_nt(qneg[p], k[:, p * LANES:(p + 1) * LANES]) for p in range(N_PAIRS)], axis=0)
            zn2 = zn * LOG2_E
            l2 = jnp.minimum(zn2, 0.0) - jnp.log2(1.0 + jnp.exp2(-jnp.abs(zn2)))
            if diag:
                l2 = jnp.where(strict, l2, 0.0)
            staged.append((ks, l2 - zn2, _dot(l2.astype(BF16), suffix), jnp.sum(l2, axis=1, keepdims=True)))
        for ks, log2_beta, later_in, row_sum in staged:
            w = jnp.exp2(later_in + carry + log2_beta)
            if diag:
                w = jnp.where(strict, w, 0.0)
            wb = w.astype(BF16)
            v = v_ref[ks, :]
            acc = acc + jnp.concatenate(
                [_dot(wb[p * pair_rows:(p + 1) * pair_rows, :], v[:, p * LANES:(p + 1) * LANES])
                 for p in range(N_PAIRS)], axis=0)
            carry = carry + row_sum
        return carry, acc

    init = (jnp.zeros((rows_all, 1), F32), jnp.zeros((rows_all, LANES), F32))
    state = blocks([i], init, True)
    state = lax.fori_loop(0, i // 2, lambda t, st: blocks([i - 1 - 2 * t, i - 2 - 2 * t], st, False), state)
    _, acc = lax.cond(i % 2 == 1, lambda st: blocks([0], st, False), lambda st: st, state)
    o_ref[...] = _merge_heads([acc[h * ATTN_TQ:(h + 1) * ATTN_TQ, :] for h in range(N_HEADS)]).astype(BF16)


def _sb_attention(main3):
    b, s, _ = main3.shape
    col = lambda k: pl.BlockSpec((None, s, MIX_WIDTH), lambda bi, i, k=k: (bi, 0, k))
    return pl.pallas_call(
        _sb_kernel,
        grid=(b, s // ATTN_TQ),
        in_specs=[col(5), col(6), col(7)],
        out_specs=pl.BlockSpec((None, ATTN_TQ, MIX_WIDTH), lambda bi, i: (bi, i, 0)),
        out_shape=jax.ShapeDtypeStruct((b, s, MIX_WIDTH), BF16),
        compiler_params=_params(2),
        name="sb",
    )(main3, main3, main3)


def _fox_kernel(q_ref, k_ref, v_ref, f_ref, bf_ref, qg_ref, kg_ref, o_ref,
                qn_scr, kn_scr, ccol_scr, crow_scr):
    i = pl.program_id(1)
    seq = q_ref.shape[0]

    @pl.when(i == 0)
    def _prologue():
        r = lax.broadcasted_iota(jnp.int32, (MIX_WIDTH, MIX_WIDTH), 0) // HEAD_DIM
        c = lax.broadcasted_iota(jnp.int32, (MIX_WIDTH, MIX_WIDTH), 1) // HEAD_DIM
        same_head = jnp.where(r == c, 1.0, 0.0).astype(BF16)

        def norm_rows(x_ref, g, scale, dst, rows):
            x = x_ref[rows, :].astype(F32)
            hi, lo = _split_hi_lo(x * x)
            ms = (_dot(hi, same_head) + _dot(lo, same_head)) * (1.0 / HEAD_DIM)
            dst[rows, :] = (x * lax.rsqrt(ms + EPS) * g * scale).astype(BF16)

        tr = lax.broadcasted_iota(jnp.int32, (LANES, LANES), 0)
        tc = lax.broadcasted_iota(jnp.int32, (LANES, LANES), 1)
        prefix = jnp.where(tc <= tr, 1.0, 0.0).astype(BF16)

        def cum_block(blk, run):
            rows = pl.ds(pl.multiple_of(blk * LANES, LANES), LANES)
            norm_rows(q_ref, qg_ref[...], SCALE, qn_scr, rows)
            norm_rows(k_ref, kg_ref[...], 1.0, kn_scr, rows)
            logf = jax.nn.log_sigmoid(f_ref[rows, :] + bf_ref[...])
            hi = logf.astype(BF16)
            rem = logf - hi.astype(F32)
            mid = rem.astype(BF16)
            lo = (rem - mid.astype(F32)).astype(BF16)
            cblk = _dot(prefix, hi) + _dot(prefix, mid) + _dot(prefix, lo) + run
            ccol_scr[rows, :] = cblk
            crow_scr[:, rows] = cblk.T[0:SUBLANES, :]
            return cblk[LANES - 1:LANES, :]

        lax.fori_loop(0, seq // LANES, cum_block, jnp.zeros((1, LANES), F32))

    qrows = pl.ds(pl.multiple_of(i * ATTN_TQ, ATTN_TQ), ATTN_TQ)
    q = qn_scr[qrows, :]
    qs = [jnp.concatenate([_masked_head(q, HEADS_PER_PAIR * p + e) for e in range(HEADS_PER_PAIR)], axis=0)
          for p in range(N_PAIRS)]
    cq = ccol_scr[qrows, :]
    ct = [cq[:, h:h + 1] for h in range(N_HEADS)]
    rows_all = N_HEADS * ATTN_TQ
    pair_rows = HEADS_PER_PAIR * ATTN_TQ
    r = lax.broadcasted_iota(jnp.int32, (rows_all, ATTN_TK), 0) & (ATTN_TQ - 1)
    c = lax.broadcasted_iota(jnp.int32, (rows_all, ATTN_TK), 1)
    causal = c <= r

    def blocks(js, state, diag):
        m, l, acc = state
        scores = []
        m_new = m
        for j in js:
            ks = pl.ds(pl.multiple_of(j * ATTN_TK, ATTN_TK), ATTN_TK)
            k = kn_scr[ks, :]
            qk = jnp.concatenate(
                [_dot_nt(qs[p], k[:, p * LANES:(p + 1) * LANES]) for p in range(N_PAIRS)], axis=0)
            decay = jnp.concatenate([ct[h] - crow_scr[h:h + 1, ks] for h in range(N_HEADS)], axis=0)
            s = qk + decay
            if diag:
                s = jnp.where(causal, s, NEG_BIG)
            m_new = jnp.maximum(m_new, jnp.max(s, axis=1, keepdims=True))
            scores.append((ks, s))
        alpha = jnp.exp(m - m_new)
        l = alpha * l
        acc = alpha * acc
        for ks, s in scores:
            p = jnp.exp(s - m_new)
            l = l + jnp.sum(p, axis=1, keepdims=True)
            pb = p.astype(BF16)
            v = v_ref[ks, :]
            acc = acc + jnp.concatenate(
                [_dot(pb[pr * pair_rows:(pr + 1) * pair_rows, :], v[:, pr * LANES:(pr + 1) * LANES])
                 for pr in range(N_PAIRS)], axis=0)
        return m_new, l, acc

    init = (jnp.full((rows_all, 1), NEG_BIG, F32), jnp.zeros((rows_all, 1), F32),
            jnp.zeros((rows_all, LANES), F32))
    state = blocks([i], init, True)
    state = lax.fori_loop(0, i // 2, lambda t, st: blocks([i - 1 - 2 * t, i - 2 - 2 * t], st, False), state)
    _, l, acc = lax.cond(i % 2 == 1, lambda st: blocks([0], st, False), lambda st: st, state)
    out = acc / l
    o_ref[...] = _merge_heads([out[h * ATTN_TQ:(h + 1) * ATTN_TQ, :] for h in range(N_HEADS)]).astype(BF16)


def _fox_attention(main3, f3, bf_row, qg_row, kg_row):
    b, s, _ = main3.shape
    col = lambda k: pl.BlockSpec((None, s, MIX_WIDTH), lambda bi, i, k=k: (bi, 0, k))
    return pl.pallas_call(
        _fox_kernel,
        grid=(b, s // ATTN_TQ),
        in_specs=[col(8), col(9), col(10),
                  pl.BlockSpec((None, s, LANES), lambda bi, i: (bi, 0, 0)),
                  _resident((1, LANES)),
                  _resident((1, MIX_WIDTH)),
                  _resident((1, MIX_WIDTH))],
        out_specs=pl.BlockSpec((None, ATTN_TQ, MIX_WIDTH), lambda bi, i: (bi, i, 0)),
        out_shape=jax.ShapeDtypeStruct((b, s, MIX_WIDTH), BF16),
        scratch_shapes=[pltpu.VMEM((s, MIX_WIDTH), BF16),
                        pltpu.VMEM((s, MIX_WIDTH), BF16),
                        pltpu.VMEM((s, LANES), F32),
                        pltpu.VMEM((SUBLANES, s), F32)],
        compiler_params=_params(2),
        name="fox",
    )(main3, main3, main3, f3, bf_row, qg_row, kg_row)


def _merge_kernel(h_ref, g_ref, yab_ref, yc_ref, yd_ref, wg_ref, wb_ref, wo_ref, o_ref, merged_scr):
    xn = _rms_norm_rows(h_ref[...], g_ref[...]).astype(BF16)
    ys = [yab_ref[:, 0:MIX_WIDTH], yab_ref[:, MIX_WIDTH:2 * MIX_WIDTH], yc_ref[...], yd_ref[...]]
    for c in range(D_MODEL // COL_CHUNK):
        cs = slice(c * COL_CHUNK, (c + 1) * COL_CHUNK)
        acc = None
        for n in range(N_BRANCH):
            gs = slice(n * D_MODEL + c * COL_CHUNK, n * D_MODEL + (c + 1) * COL_CHUNK)
            term = jax.nn.sigmoid(_dot(xn, wg_ref[:, gs])) * _dot(ys[n], wb_ref[n, :, cs])
            acc = term if acc is None else acc + term
        merged_scr[:, cs] = acc.astype(BF16)
    merged = merged_scr[...]
    for c in range(D_MODEL // COL_CHUNK):
        cs = slice(c * COL_CHUNK, (c + 1) * COL_CHUNK)
        o_ref[:, cs] = h_ref[:, cs] + _dot(merged, wo_ref[:, cs])


def _merge(h, g, yab, yc, yd, w_gate, w_branch, w_out):
    t = h.shape[0]
    rows = lambda w: pl.BlockSpec((TOKEN_TILE, w), lambda i: (i, 0))
    return pl.pallas_call(
        _merge_kernel,
        grid=(t // TOKEN_TILE,),
        in_specs=[rows(D_MODEL), _resident((1, D_MODEL)),
                  rows(2 * MIX_WIDTH), rows(MIX_WIDTH), rows(MIX_WIDTH),
                  _resident((D_MODEL, N_BRANCH * D_MODEL)),
                  _resident((N_BRANCH, MIX_WIDTH, D_MODEL)),
                  _resident((D_MODEL, D_MODEL))],
        out_specs=rows(D_MODEL),
        out_shape=jax.ShapeDtypeStruct((t, D_MODEL), F32),
        scratch_shapes=[pltpu.VMEM((TOKEN_TILE, D_MODEL), BF16)],
        compiler_params=_params(1),
        name="merge",
    )(h, g, yab, yc, yd, w_gate, w_branch, w_out)


def _ffn_kernel(h_ref, g_ref, wi_ref, wo_ref, o_ref, acc_scr):
    xn = _rms_norm_rows(h_ref[...], g_ref[...]).astype(BF16)
    for c in range(FFN_HIDDEN // COL_CHUNK):
        gate = _dot(xn, wi_ref[:, c * COL_CHUNK:(c + 1) * COL_CHUNK])
        up = _dot(xn, wi_ref[:, FFN_HIDDEN + c * COL_CHUNK:FFN_HIDDEN + (c + 1) * COL_CHUNK])
        act = (jax.nn.silu(gate) * up).astype(BF16)
        part = _dot(act, wo_ref[c * COL_CHUNK:(c + 1) * COL_CHUNK, :])
        if c == 0:
            acc_scr[...] = part
        else:
            acc_scr[...] += part
    o_ref[...] = h_ref[...] + acc_scr[...]


def _ffn(h, g, w_in, w_out):
    t = h.shape[0]
    rows = pl.BlockSpec((TOKEN_TILE, D_MODEL), lambda i: (i, 0))
    return pl.pallas_call(
        _ffn_kernel,
        grid=(t // TOKEN_TILE,),
        in_specs=[rows, _resident((1, D_MODEL)),
                  _resident((D_MODEL, 2 * FFN_HIDDEN)),
                  _resident((FFN_HIDDEN, D_MODEL))],
        out_specs=rows,
        out_shape=jax.ShapeDtypeStruct((t, D_MODEL), F32),
        scratch_shapes=[pltpu.VMEM((TOKEN_TILE, D_MODEL), F32)],
        compiler_params=_params(1),
        name="ffn",
    )(h, g, w_in, w_out)


def kernel(x, norm_mix_g, w_in, w_conv, w_spatial, b_spatial, gmlp_ln_g, gmlp_ln_b,
           fox_q_norm_g, fox_k_norm_g, fox_forget_b, w_branch, w_out, norm_ffn_g,
           w_ffn_in, w_ffn_out):
    b, s, d = x.shape
    depth = w_in.shape[0]
    assert d == D_MODEL and s % max(ATTN_TQ, ATTN_TK, GMLP_CHUNK) == 0 and (b * s) % TOKEN_TILE == 0
    t = b * s
    h = x.reshape(t, d)
    for l in range(depth):
        w_main = w_in[l, :, :N_MAIN].astype(BF16)
        w_f = jnp.pad(w_in[l, :, N_MAIN:N_MAIN + N_HEADS], ((0, 0), (0, LANES - N_HEADS))).astype(BF16)
        w_gate = w_in[l, :, N_MAIN + N_HEADS:].astype(BF16)
        bf_row = jnp.pad(fox_forget_b[l], (0, LANES - N_HEADS)).reshape(1, LANES)
        qg_row = jnp.tile(fox_q_norm_g[l], N_HEADS).reshape(1, MIX_WIDTH)
        kg_row = jnp.tile(fox_k_norm_g[l], N_HEADS).reshape(1, MIX_WIDTH)
        bs_rows = jnp.repeat(b_spatial[l].T, GROUP_WIDTH, axis=1)

        main, f_raw = _proj(h, norm_mix_g[l].reshape(1, d), w_main, w_f)
        main3 = main.reshape(b, s, N_MAIN)
        yab = _convgmlp(main3, w_conv[l], w_spatial[l], bs_rows,
                        gmlp_ln_g[l].reshape(1, MIX_WIDTH), gmlp_ln_b[l].reshape(1, MIX_WIDTH))
        yc = _sb_attention(main3)
        yd = _fox_attention(main3, f_raw.reshape(b, s, LANES), bf_row, qg_row, kg_row)
        h = _merge(h, norm_mix_g[l].reshape(1, d), yab.reshape(t, 2 * MIX_WIDTH),
                   yc.reshape(t, MIX_WIDTH), yd.reshape(t, MIX_WIDTH),
                   w_gate, w_branch[l].astype(BF16), w_out[l].astype(BF16))
        h = _ffn(h, norm_ffn_g[l].reshape(1, d), w_ffn_in[l].astype(BF16), w_ffn_out[l].astype(BF16))
    return h.reshape(b, s, d)
```

```python
import functools

import jax
import jax.numpy as jnp
from jax import lax
from jax.experimental import pallas as pl
from jax.experimental.pallas import tpu as pltpu

D_MODEL = 1024
MIX_WIDTH = 256
HEAD_DIM = 64
N_HEADS = MIX_WIDTH // HEAD_DIM
N_BRANCH = 4
CONV_K = 3
GMLP_GROUPS = 4
GMLP_CHUNK = 128
GROUP_WIDTH = MIX_WIDTH // GMLP_GROUPS
FFN_HIDDEN = 2816
EPS = 1e-6
N_MAIN = 11 * MIX_WIDTH
LANES = 128
HEADS_PER_PAIR = LANES // HEAD_DIM
N_PAIRS = MIX_WIDTH // LANES
SUBLANES = 8
VMEM_LIMIT_BYTES = 56 * 1024 * 1024

TOKEN_TILE = 512
COL_CHUNK = 256
ATTN_TQ = 256
ATTN_TK = 256
SCALE = HEAD_DIM ** -0.5
LOG2_E = 1.4426950408889634
LOG2_F32_UNDERFLOW = -151.0
NEG_BIG = -1e30

F32 = jnp.float32
BF16 = jnp.bfloat16


def _dot(a, b):
    return jnp.dot(a, b, preferred_element_type=F32)


def _dot_nt(a, b):
    return lax.dot_general(a, b, (((1,), (1,)), ((), ())), preferred_element_type=F32)


def _split_hi_lo(x):
    hi = x.astype(BF16)
    lo = (x - hi.astype(F32)).astype(BF16)
    return hi, lo


def _rms_norm_rows(x, g):
    ms = jnp.mean(x * x, axis=-1, keepdims=True)
    return x * lax.rsqrt(ms + EPS) * g


def _resident(shape):
    return pl.BlockSpec(shape, lambda *_: (0,) * len(shape), pipeline_mode=pl.Buffered(1))


def _params(n_axes):
    return pltpu.CompilerParams(dimension_semantics=("arbitrary",) * n_axes,
                                vmem_limit_bytes=VMEM_LIMIT_BYTES)


def _proj_kernel(h_ref, g_ref, wm_ref, wf_ref, main_ref, f_ref):
    xn = _rms_norm_rows(h_ref[...], g_ref[...]).astype(BF16)
    for c in range(N_MAIN // COL_CHUNK):
        cs = slice(c * COL_CHUNK, (c + 1) * COL_CHUNK)
        main_ref[:, cs] = _dot(xn, wm_ref[:, cs]).astype(BF16)
    f_ref[...] = _dot(xn, wf_ref[...])


def _proj(h, g, w_main, w_f):
    t = h.shape[0]
    return pl.pallas_call(
        _proj_kernel,
        grid=(t // TOKEN_TILE,),
        in_specs=[pl.BlockSpec((TOKEN_TILE, D_MODEL), lambda i: (i, 0)),
                  _resident((1, D_MODEL)),
                  _resident((D_MODEL, N_MAIN)),
                  _resident((D_MODEL, LANES))],
        out_specs=[pl.BlockSpec((TOKEN_TILE, N_MAIN), lambda i: (i, 0)),
                   pl.BlockSpec((TOKEN_TILE, LANES), lambda i: (i, 0))],
        out_shape=[jax.ShapeDtypeStruct((t, N_MAIN), BF16),
                   jax.ShapeDtypeStruct((t, LANES), F32)],
        compiler_params=_params(1),
        name="proj",
    )(h, g, w_main, w_f)


def _convgmlp_kernel(cb_ref, cc_ref, cx_ref, u_ref, v_ref, wconv_ref, ws_ref, bs_ref,
                     lng_ref, lnb_ref, o_ref):
    seq = cb_ref.shape[0]
    n_chunks = seq // GMLP_CHUNK
    row = lax.broadcasted_iota(jnp.int32, (GMLP_CHUNK, GMLP_CHUNK), 0)
    col = lax.broadcasted_iota(jnp.int32, (GMLP_CHUNK, GMLP_CHUNK), 1)
    w_tril = [jnp.where(col <= row, ws_ref[gi], 0.0).astype(BF16) for gi in range(GMLP_GROUPS)]
    lane = lax.broadcasted_iota(jnp.int32, (GMLP_CHUNK, LANES), 1)
    first_group = lane < GROUP_WIDTH
    w0 = wconv_ref[0:1, :]
    w1 = wconv_ref[1:2, :]
    w2 = wconv_ref[2:3, :]

    def chunk(c, prev_tail):
        rows = pl.ds(pl.multiple_of(c * GMLP_CHUNK, GMLP_CHUNK), GMLP_CHUNK)
        xc = cc_ref[rows, :].astype(F32) * cx_ref[rows, :].astype(F32)
        win = jnp.concatenate([prev_tail, xc], axis=0)
        xc1 = pltpu.roll(win, 1, 0)[SUBLANES:, :]
        xc2 = pltpu.roll(win, 2, 0)[SUBLANES:, :]
        ya = cb_ref[rows, :].astype(F32) * (w0 * xc2 + w1 * xc1 + w2 * xc)
        o_ref[rows, 0:MIX_WIDTH] = ya.astype(BF16)

        gu = jax.nn.gelu(u_ref[rows, :].astype(F32))
        gv = jax.nn.gelu(v_ref[rows, :].astype(F32))
        mu = jnp.mean(gv, axis=-1, keepdims=True)
        cen = gv - mu
        var = jnp.mean(cen * cen, axis=-1, keepdims=True)
        vn = (cen * lax.rsqrt(var + EPS) * lng_ref[...] + lnb_ref[...]).astype(BF16)
        halves = []
        for lb in range(MIX_WIDTH // LANES):
            vb = vn[:, lb * LANES:(lb + 1) * LANES]
            m0 = _dot(w_tril[2 * lb], vb)
            m1 = _dot(w_tril[2 * lb + 1], vb)
            halves.append(jnp.where(first_group, m0, m1))
        mixed = jnp.concatenate(halves, axis=1) + bs_ref[...]
        o_ref[rows, MIX_WIDTH:2 * MIX_WIDTH] = (gu * mixed).astype(BF16)
        return xc[GMLP_CHUNK - SUBLANES:, :]

    lax.fori_loop(0, n_chunks, chunk, jnp.zeros((SUBLANES, MIX_WIDTH), F32))


def _convgmlp(main3, w_conv, w_s, bs_rows, ln_g, ln_b):
    b, s, _ = main3.shape
    col = lambda k: pl.BlockSpec((None, s, MIX_WIDTH), lambda i, k=k: (i, 0, k))
    return pl.pallas_call(
        _convgmlp_kernel,
        grid=(b,),
        in_specs=[col(0), col(1), col(2), col(3), col(4),
                  _resident((CONV_K, MIX_WIDTH)),
                  _resident((GMLP_GROUPS, GMLP_CHUNK, GMLP_CHUNK)),
                  _resident((GMLP_CHUNK, MIX_WIDTH)),
                  _resident((1, MIX_WIDTH)),
                  _resident((1, MIX_WIDTH))],
        out_specs=pl.BlockSpec((None, s, 2 * MIX_WIDTH), lambda i: (i, 0, 0)),
        out_shape=jax.ShapeDtypeStruct((b, s, 2 * MIX_WIDTH), BF16),
        compiler_params=_params(1),
        name="convgmlp",
    )(main3, main3, main3, main3, main3, w_conv, w_s, bs_rows, ln_g, ln_b)


def _pair(x, h):
    p = (h * HEAD_DIM) // LANES
    return x[:, p * LANES:(p + 1) * LANES]


def _head_in_pair_mask(rows, h):
    lane = lax.broadcasted_iota(jnp.int32, (rows, LANES), 1)
    first = lane < HEAD_DIM
    return first if (h * HEAD_DIM) % LANES == 0 else jnp.logical_not(first)


def _masked_head(x, h):
    xp = _pair(x, h)
    return jnp.where(_head_in_pair_mask(x.shape[0], h), xp, jnp.zeros_like(xp))


def _merge_heads(per_head):
    rows = per_head[0].shape[0]
    blocks = []
    for p in range(MIX_WIDTH // LANES):
        h0 = p * (LANES // HEAD_DIM)
        blocks.append(jnp.where(_head_in_pair_mask(rows, h0), per_head[h0], per_head[h0 + 1]))
    return jnp.concatenate(blocks, axis=1)


def _sb_kernel(q_ref, k_ref, v_ref, o_ref):
    i = pl.program_id(1)
    q = q_ref[pl.ds(pl.multiple_of(i * ATTN_TQ, ATTN_TQ), ATTN_TQ), :]
    qneg = [jnp.concatenate([_masked_head(q, HEADS_PER_PAIR * p + e) for e in range(HEADS_PER_PAIR)],
                            axis=0) * (-SCALE) for p in range(N_PAIRS)]
    r = lax.broadcasted_iota(jnp.int32, (ATTN_TK, ATTN_TK), 0)
    c = lax.broadcasted_iota(jnp.int32, (ATTN_TK, ATTN_TK), 1)
    suffix = jnp.where(r > c, 1.0, 0.0).astype(BF16)
    rows_all = N_HEADS * ATTN_TQ
    rq = lax.broadcasted_iota(jnp.int32, (rows_all, ATTN_TK), 0) & (ATTN_TQ - 1)
    cq = lax.broadcasted_iota(jnp.int32, (rows_all, ATTN_TK), 1)
    strict = cq < rq

    pair_rows = HEADS_PER_PAIR * ATTN_TQ

    def blocks(js, state, diag):
        carry, acc = state
        staged = []
        for j in js:
            ks = pl.ds(pl.multiple_of(j * ATTN_TK, ATTN_TK), ATTN_TK)
            k = k_ref[ks, :]
            zn = jnp.concatenate(
                [_dot_nt(qneg[p], k[:, p * LANES:(p + 1) * LANES]) for p in range(N_PAIRS)], axis=0)
            zn2 = zn * LOG2_E
            l2 = jnp.minimum(zn2, 0.0) - jnp.log2(1.0 + jnp.exp2(-jnp.abs(zn2)))
            if diag:
                l2 = jnp.where(strict, l2, 0.0)
            staged.append((ks, l2 - zn2, _dot(l2.astype(BF16), suffix), jnp.sum(l2, axis=1, keepdims=True)))
        for ks, log2_beta, later_in, row_sum in staged:
            w = jnp.exp2(later_in + carry + log2_beta)
            if diag:
                w = jnp.where(strict, w, 0.0)
            wb = w.astype(BF16)
            v = v_ref[ks, :]
            acc = acc + jnp.concatenate(
                [_dot(wb[p * pair_rows:(p + 1) * pair_rows, :], v[:, p * LANES:(p + 1) * LANES])
                 for p in range(N_PAIRS)], axis=0)
            carry = carry + row_sum
        return carry, acc

    def live(carry):
        return jnp.max(carry) > LOG2_F32_UNDERFLOW

    def pair_step(st):
        t, _, carry, acc = st
        carry, acc = blocks([i - 1 - 2 * t, i - 2 - 2 * t], (carry, acc), False)
        return t + 1, live(carry), carry, acc

    init = (jnp.zeros((rows_all, 1), F32), jnp.zeros((rows_all, LANES), F32))
    carry, acc = blocks([i], init, True)
    n_pairs = i // 2
    t, alive, carry, acc = lax.while_loop(lambda st: (st[0] < n_pairs) & st[1], pair_step,
                                          (jnp.int32(0), live(carry), carry, acc))
    _, acc = lax.cond((i % 2 == 1) & alive, lambda st: blocks([0], st, False), lambda st: st, (carry, acc))
    o_ref[...] = _merge_heads([acc[h * ATTN_TQ:(h + 1) * ATTN_TQ, :] for h in range(N_HEADS)]).astype(BF16)


def _sb_attention(main3):
    b, s, _ = main3.shape
    col = lambda k: pl.BlockSpec((None, s, MIX_WIDTH), lambda bi, i, k=k: (bi, 0, k))
    return pl.pallas_call(
        _sb_kernel,
        grid=(b, s // ATTN_TQ),
        in_specs=[col(5), col(6), col(7)],
        out_specs=pl.BlockSpec((None, ATTN_TQ, MIX_WIDTH), lambda bi, i: (bi, i, 0)),
        out_shape=jax.ShapeDtypeStruct((b, s, MIX_WIDTH), BF16),
        compiler_params=_params(2),
        name="sb",
    )(main3, main3, main3)


def _split3(x):
    hi = x.astype(BF16).astype(F32)
    rem = x - hi
    mid = rem.astype(BF16).astype(F32)
    return hi, mid, rem - mid


def _place_lanes(rows, entries):
    lane = lax.broadcasted_iota(jnp.int32, (rows, LANES), 1)
    out = jnp.zeros((rows, LANES), F32)
    for idx, val in entries.items():
        out = jnp.where(lane == idx, val, out)
    return out


def _fox_kernel(q_ref, k_ref, v_ref, f_ref, bf_ref, qg_ref, kg_ref, o_ref, qa_scr, ka_scr):
    i = pl.program_id(1)
    seq = q_ref.shape[0]
    parts = 3

    @pl.when(i == 0)
    def _prologue():
        r = lax.broadcasted_iota(jnp.int32, (MIX_WIDTH, MIX_WIDTH), 0) // HEAD_DIM
        c = lax.broadcasted_iota(jnp.int32, (MIX_WIDTH, MIX_WIDTH), 1) // HEAD_DIM
        same_head = jnp.where(r == c, 1.0, 0.0).astype(BF16)

        def norm_rows(x_ref, g, scale, rows):
            x = x_ref[rows, :].astype(F32)
            hi, lo = _split_hi_lo(x * x)
            ms = (_dot(hi, same_head) + _dot(lo, same_head)) * (1.0 / HEAD_DIM)
            return (x * lax.rsqrt(ms + EPS) * g * scale).astype(BF16)

        tr = lax.broadcasted_iota(jnp.int32, (LANES, LANES), 0)
        tc = lax.broadcasted_iota(jnp.int32, (LANES, LANES), 1)
        prefix = jnp.where(tc <= tr, 1.0, 0.0).astype(BF16)

        def cum_block(blk, run):
            rows = pl.ds(pl.multiple_of(blk * LANES, LANES), LANES)
            qn = norm_rows(q_ref, qg_ref[...], SCALE, rows)
            kn = norm_rows(k_ref, kg_ref[...], 1.0, rows)
            logf = jax.nn.log_sigmoid(f_ref[rows, :] + bf_ref[...])
            cblk = run
            for part in _split3(logf):
                cblk = cblk + _dot(prefix, part.astype(BF16))
            c_parts = _split3(cblk)
            for p in range(N_PAIRS):
                k_entries = {}
                for e in range(HEADS_PER_PAIR):
                    h = HEADS_PER_PAIR * p + e
                    base = 2 * parts * e
                    q_entries = {}
                    for s in range(parts):
                        q_entries[base + s] = c_parts[s][:, h:h + 1]
                        q_entries[base + parts + s] = 1.0
                        k_entries[base + s] = 1.0
                        k_entries[base + parts + s] = -c_parts[s][:, h:h + 1]
                    qa_scr[h, rows, 0:LANES] = _masked_head(qn, h)
                    qa_scr[h, rows, LANES:2 * LANES] = _place_lanes(LANES, q_entries).astype(BF16)
                ka_scr[p, rows, 0:LANES] = kn[:, p * LANES:(p + 1) * LANES]
                ka_scr[p, rows, LANES:2 * LANES] = _place_lanes(LANES, k_entries).astype(BF16)
            return cblk[LANES - 1:LANES, :]

        lax.fori_loop(0, seq // LANES, cum_block, jnp.zeros((1, LANES), F32))

    qrows = pl.ds(pl.multiple_of(i * ATTN_TQ, ATTN_TQ), ATTN_TQ)
    qs = [jnp.concatenate([qa_scr[HEADS_PER_PAIR * p + e, qrows, :] for e in range(HEADS_PER_PAIR)], axis=0)
          for p in range(N_PAIRS)]
    rows_all = N_HEADS * ATTN_TQ
    pair_rows = HEADS_PER_PAIR * ATTN_TQ
    r = lax.broadcasted_iota(jnp.int32, (rows_all, ATTN_TK), 0) & (ATTN_TQ - 1)
    c = lax.broadcasted_iota(jnp.int32, (rows_all, ATTN_TK), 1)
    causal = c <= r

    def blocks(js, state, diag):
        m, l, acc = state
        scores = []
        for j in js:
            ks = pl.ds(pl.multiple_of(j * ATTN_TK, ATTN_TK), ATTN_TK)
            s = jnp.concatenate([_dot_nt(qs[p], ka_scr[p, ks, :]) for p in range(N_PAIRS)], axis=0)
            if diag:
                s = jnp.where(causal, s, NEG_BIG)
            scores.append((ks, s))
        s_max = scores[0][1]
        for _, s in scores[1:]:
            s_max = jnp.maximum(s_max, s)
        m_new = jnp.maximum(m, jnp.max(s_max, axis=1, keepdims=True))
        alpha = jnp.exp(m - m_new)
        acc = alpha * acc
        p_sum = None
        for ks, s in scores:
            p = jnp.exp(s - m_new)
            p_sum = p if p_sum is None else p_sum + p
            pb = p.astype(BF16)
            v = v_ref[ks, :]
            acc = acc + jnp.concatenate(
                [_dot(pb[pr * pair_rows:(pr + 1) * pair_rows, :], v[:, pr * LANES:(pr + 1) * LANES])
                 for pr in range(N_PAIRS)], axis=0)
        l = alpha * l + jnp.sum(p_sum, axis=1, keepdims=True)
        return m_new, l, acc

    init = (jnp.full((rows_all, 1), NEG_BIG, F32), jnp.zeros((rows_all, 1), F32),
            jnp.zeros((rows_all, LANES), F32))
    state = blocks([i], init, True)
    state = lax.fori_loop(0, i // 2, lambda t, st: blocks([i - 1 - 2 * t, i - 2 - 2 * t], st, False), state)
    _, l, acc = lax.cond(i % 2 == 1, lambda st: blocks([0], st, False), lambda st: st, state)
    out = acc / l
    o_ref[...] = _merge_heads([out[h * ATTN_TQ:(h + 1) * ATTN_TQ, :] for h in range(N_HEADS)]).astype(BF16)


def _fox_attention(main3, f3, bf_row, qg_row, kg_row):
    b, s, _ = main3.shape
    col = lambda k: pl.BlockSpec((None, s, MIX_WIDTH), lambda bi, i, k=k: (bi, 0, k))
    return pl.pallas_call(
        _fox_kernel,
        grid=(b, s // ATTN_TQ),
        in_specs=[col(8), col(9), col(10),
                  pl.BlockSpec((None, s, LANES), lambda bi, i: (bi, 0, 0)),
                  _resident((1, LANES)),
                  _resident((1, MIX_WIDTH)),
                  _resident((1, MIX_WIDTH))],
        out_specs=pl.BlockSpec((None, ATTN_TQ, MIX_WIDTH), lambda bi, i: (bi, i, 0)),
        out_shape=jax.ShapeDtypeStruct((b, s, MIX_WIDTH), BF16),
        scratch_shapes=[pltpu.VMEM((N_HEADS, s, 2 * LANES), BF16),
                        pltpu.VMEM((N_PAIRS, s, 2 * LANES), BF16)],
        compiler_params=_params(2),
        name="fox",
    )(main3, main3, main3, f3, bf_row, qg_row, kg_row)


def _merge_kernel(h_ref, g_ref, yab_ref, yc_ref, yd_ref, wg_ref, wb_ref, wo_ref, o_ref, merged_scr):
    xn = _rms_norm_rows(h_ref[...], g_ref[...]).astype(BF16)
    ys = [yab_ref[:, 0:MIX_WIDTH], yab_ref[:, MIX_WIDTH:2 * MIX_WIDTH], yc_ref[...], yd_ref[...]]
    for c in range(D_MODEL // COL_CHUNK):
        cs = slice(c * COL_CHUNK, (c + 1) * COL_CHUNK)
        acc = None
        for n in range(N_BRANCH):
            gs = slice(n * D_MODEL + c * COL_CHUNK, n * D_MODEL + (c + 1) * COL_CHUNK)
            term = jax.nn.sigmoid(_dot(xn, wg_ref[:, gs])) * _dot(ys[n], wb_ref[n, :, cs])
            acc = term if acc is None else acc + term
        merged_scr[:, cs] = acc.astype(BF16)
    merged = merged_scr[...]
    for c in range(D_MODEL // COL_CHUNK):
        cs = slice(c * COL_CHUNK, (c + 1) * COL_CHUNK)
        o_ref[:, cs] = h_ref[:, cs] + _dot(merged, wo_ref[:, cs])


def _merge(h, g, yab, yc, yd, w_gate, w_branch, w_out):
    t = h.shape[0]
    rows = lambda w: pl.BlockSpec((TOKEN_TILE, w), lambda i: (i, 0))
    return pl.pallas_call(
        _merge_kernel,
        grid=(t // TOKEN_TILE,),
        in_specs=[rows(D_MODEL), _resident((1, D_MODEL)),
                  rows(2 * MIX_WIDTH), rows(MIX_WIDTH), rows(MIX_WIDTH),
                  _resident((D_MODEL, N_BRANCH * D_MODEL)),
                  _resident((N_BRANCH, MIX_WIDTH, D_MODEL)),
                  _resident((D_MODEL, D_MODEL))],
        out_specs=rows(D_MODEL),
        out_shape=jax.ShapeDtypeStruct((t, D_MODEL), F32),
        scratch_shapes=[pltpu.VMEM((TOKEN_TILE, D_MODEL), BF16)],
        compiler_params=_params(1),
        name="merge",
    )(h, g, yab, yc, yd, w_gate, w_branch, w_out)


def _ffn_kernel(h_ref, g_ref, wi_ref, wo_ref, o_ref, acc_scr):
    xn = _rms_norm_rows(h_ref[...], g_ref[...]).astype(BF16)
    for c in range(FFN_HIDDEN // COL_CHUNK):
        gate = _dot(xn, wi_ref[:, c * COL_CHUNK:(c + 1) * COL_CHUNK])
        up = _dot(xn, wi_ref[:, FFN_HIDDEN + c * COL_CHUNK:FFN_HIDDEN + (c + 1) * COL_CHUNK])
        act = (jax.nn.silu(gate) * up).astype(BF16)
        part = _dot(act, wo_ref[c * COL_CHUNK:(c + 1) * COL_CHUNK, :])
        if c == 0:
            acc_scr[...] = part
        else:
            acc_scr[...] += part
    o_ref[...] = h_ref[...] + acc_scr[...]


def _ffn(h, g, w_in, w_out):
    t = h.shape[0]
    rows = pl.BlockSpec((TOKEN_TILE, D_MODEL), lambda i: (i, 0))
    return pl.pallas_call(
        _ffn_kernel,
        grid=(t // TOKEN_TILE,),
        in_specs=[rows, _resident((1, D_MODEL)),
                  _resident((D_MODEL, 2 * FFN_HIDDEN)),
                  _resident((FFN_HIDDEN, D_MODEL))],
        out_specs=rows,
        out_shape=jax.ShapeDtypeStruct((t, D_MODEL), F32),
        scratch_shapes=[pltpu.VMEM((TOKEN_TILE, D_MODEL), F32)],
        compiler_params=_params(1),
        name="ffn",
    )(h, g, w_in, w_out)


def kernel(x, norm_mix_g, w_in, w_conv, w_spatial, b_spatial, gmlp_ln_g, gmlp_ln_b,
           fox_q_norm_g, fox_k_norm_g, fox_forget_b, w_branch, w_out, norm_ffn_g,
           w_ffn_in, w_ffn_out):
    b, s, d = x.shape
    depth = w_in.shape[0]
    assert d == D_MODEL and s % max(ATTN_TQ, ATTN_TK, GMLP_CHUNK) == 0 and (b * s) % TOKEN_TILE == 0
    t = b * s
    h = x.reshape(t, d)
    for l in range(depth):
        w_main = w_in[l, :, :N_MAIN].astype(BF16)
        w_f = jnp.pad(w_in[l, :, N_MAIN:N_MAIN + N_HEADS], ((0, 0), (0, LANES - N_HEADS))).astype(BF16)
        w_gate = w_in[l, :, N_MAIN + N_HEADS:].astype(BF16)
        bf_row = jnp.pad(fox_forget_b[l], (0, LANES - N_HEADS)).reshape(1, LANES)
        qg_row = jnp.tile(fox_q_norm_g[l], N_HEADS).reshape(1, MIX_WIDTH)
        kg_row = jnp.tile(fox_k_norm_g[l], N_HEADS).reshape(1, MIX_WIDTH)
        bs_rows = jnp.repeat(b_spatial[l].T, GROUP_WIDTH, axis=1)

        main, f_raw = _proj(h, norm_mix_g[l].reshape(1, d), w_main, w_f)
        main3 = main.reshape(b, s, N_MAIN)
        yab = _convgmlp(main3, w_conv[l], w_spatial[l], bs_rows,
                        gmlp_ln_g[l].reshape(1, MIX_WIDTH), gmlp_ln_b[l].reshape(1, MIX_WIDTH))
        yc = _sb_attention(main3)
        yd = _fox_attention(main3, f_raw.reshape(b, s, LANES), bf_row, qg_row, kg_row)
        h = _merge(h, norm_mix_g[l].reshape(1, d), yab.reshape(t, 2 * MIX_WIDTH),
                   yc.reshape(t, MIX_WIDTH), yd.reshape(t, MIX_WIDTH),
                   w_gate, w_branch[l].astype(BF16), w_out[l].astype(BF16))
        h = _ffn(h, norm_ffn_g[l].reshape(1, d), w_ffn_in[l].astype(BF16), w_ffn_out[l].astype(BF16))
    return h.reshape(b, s, d)
```

```python
import functools

import jax
import jax.numpy as jnp
from jax import lax
from jax.experimental import pallas as pl
from jax.experimental.pallas import tpu as pltpu

D_MODEL = 1024
MIX_WIDTH = 256
HEAD_DIM = 64
N_HEADS = MIX_WIDTH // HEAD_DIM
N_BRANCH = 4
CONV_K = 3
GMLP_GROUPS = 4
GMLP_CHUNK = 128
GROUP_WIDTH = MIX_WIDTH // GMLP_GROUPS
FFN_HIDDEN = 2816
EPS = 1e-6
N_MAIN = 11 * MIX_WIDTH
FOX_Q_CHUNK = 8
FOX_K_CHUNK = 9
LANES = 128
HEADS_PER_PAIR = LANES // HEAD_DIM
N_PAIRS = MIX_WIDTH // LANES
FORGET_COPIES = 6
FOX_PREP_ROWS = 256
SUBLANES = 8
VMEM_LIMIT_BYTES = 56 * 1024 * 1024

TOKEN_TILE = 512
COL_CHUNK = 256
ATTN_TQ = 256
ATTN_TK = 256
SCALE = HEAD_DIM ** -0.5
LOG2_E = 1.4426950408889634
LOG2_F32_UNDERFLOW = -151.0
NEG_BIG = -1e30

F32 = jnp.float32
BF16 = jnp.bfloat16


def _dot(a, b):
    return jnp.dot(a, b, preferred_element_type=F32)


def _dot_nt(a, b):
    return lax.dot_general(a, b, (((1,), (1,)), ((), ())), preferred_element_type=F32)


def _split_hi_lo(x):
    hi = x.astype(BF16)
    lo = (x - hi.astype(F32)).astype(BF16)
    return hi, lo


def _rms_norm_rows(x, g):
    ms = jnp.mean(x * x, axis=-1, keepdims=True)
    return x * lax.rsqrt(ms + EPS) * g


def _resident(shape):
    return pl.BlockSpec(shape, lambda *_: (0,) * len(shape), pipeline_mode=pl.Buffered(1))


def _params(n_axes):
    return pltpu.CompilerParams(dimension_semantics=("arbitrary",) * n_axes,
                                vmem_limit_bytes=VMEM_LIMIT_BYTES)


def _split3(x):
    hi = x.astype(BF16).astype(F32)
    rem = x - hi
    mid = rem.astype(BF16).astype(F32)
    return hi, mid, rem - mid


def _proj_kernel(h_ref, g_ref, wm_ref, wf_ref, main_ref, f_ref):
    xn = _rms_norm_rows(h_ref[...], g_ref[...]).astype(BF16)
    for c in range(N_MAIN // COL_CHUNK):
        cs = slice(c * COL_CHUNK, (c + 1) * COL_CHUNK)
        main_ref[:, cs] = _dot(xn, wm_ref[:, cs]).astype(BF16)
    f_ref[...] = _dot(xn, wf_ref[...])


def _proj(h, g, w_main, w_f):
    t = h.shape[0]
    return pl.pallas_call(
        _proj_kernel,
        grid=(t // TOKEN_TILE,),
        in_specs=[pl.BlockSpec((TOKEN_TILE, D_MODEL), lambda i: (i, 0)),
                  _resident((1, D_MODEL)),
                  _resident((D_MODEL, N_MAIN)),
                  _resident((D_MODEL, LANES))],
        out_specs=[pl.BlockSpec((TOKEN_TILE, N_MAIN), lambda i: (i, 0)),
                   pl.BlockSpec((TOKEN_TILE, LANES), lambda i: (i, 0))],
        out_shape=[jax.ShapeDtypeStruct((t, N_MAIN), BF16),
                   jax.ShapeDtypeStruct((t, LANES), F32)],
        compiler_params=_params(1),
        name="proj",
    )(h, g, w_main, w_f)


def _convgmlp_kernel(cb_ref, cc_ref, cx_ref, u_ref, v_ref, wconv_ref, ws_ref, bs_ref,
                     lng_ref, lnb_ref, o_ref):
    seq = cb_ref.shape[0]
    n_chunks = seq // GMLP_CHUNK
    row = lax.broadcasted_iota(jnp.int32, (GMLP_CHUNK, GMLP_CHUNK), 0)
    col = lax.broadcasted_iota(jnp.int32, (GMLP_CHUNK, GMLP_CHUNK), 1)
    w_tril = [jnp.where(col <= row, ws_ref[gi], 0.0).astype(BF16) for gi in range(GMLP_GROUPS)]
    lane = lax.broadcasted_iota(jnp.int32, (GMLP_CHUNK, LANES), 1)
    first_group = lane < GROUP_WIDTH
    w0 = wconv_ref[0:1, :]
    w1 = wconv_ref[1:2, :]
    w2 = wconv_ref[2:3, :]

    def chunk(c, prev_tail):
        rows = pl.ds(pl.multiple_of(c * GMLP_CHUNK, GMLP_CHUNK), GMLP_CHUNK)
        xc = cc_ref[rows, :].astype(F32) * cx_ref[rows, :].astype(F32)
        win = jnp.concatenate([prev_tail, xc], axis=0)
        xc1 = pltpu.roll(win, 1, 0)[SUBLANES:, :]
        xc2 = pltpu.roll(win, 2, 0)[SUBLANES:, :]
        ya = cb_ref[rows, :].astype(F32) * (w0 * xc2 + w1 * xc1 + w2 * xc)
        o_ref[rows, 0:MIX_WIDTH] = ya.astype(BF16)

        gu = jax.nn.gelu(u_ref[rows, :].astype(F32))
        gv = jax.nn.gelu(v_ref[rows, :].astype(F32))
        mu = jnp.mean(gv, axis=-1, keepdims=True)
        cen = gv - mu
        var = jnp.mean(cen * cen, axis=-1, keepdims=True)
        vn = (cen * lax.rsqrt(var + EPS) * lng_ref[...] + lnb_ref[...]).astype(BF16)
        halves = []
        for lb in range(MIX_WIDTH // LANES):
            vb = vn[:, lb * LANES:(lb + 1) * LANES]
            m0 = _dot(w_tril[2 * lb], vb)
            m1 = _dot(w_tril[2 * lb + 1], vb)
            halves.append(jnp.where(first_group, m0, m1))
        mixed = jnp.concatenate(halves, axis=1) + bs_ref[...]
        o_ref[rows, MIX_WIDTH:2 * MIX_WIDTH] = (gu * mixed).astype(BF16)
        return xc[GMLP_CHUNK - SUBLANES:, :]

    lax.fori_loop(0, n_chunks, chunk, jnp.zeros((SUBLANES, MIX_WIDTH), F32))


def _convgmlp(main3, w_conv, w_s, bs_rows, ln_g, ln_b):
    b, s, _ = main3.shape
    col = lambda k: pl.BlockSpec((None, s, MIX_WIDTH), lambda i, k=k: (i, 0, k))
    return pl.pallas_call(
        _convgmlp_kernel,
        grid=(b,),
        in_specs=[col(0), col(1), col(2), col(3), col(4),
                  _resident((CONV_K, MIX_WIDTH)),
                  _resident((GMLP_GROUPS, GMLP_CHUNK, GMLP_CHUNK)),
                  _resident((GMLP_CHUNK, MIX_WIDTH)),
                  _resident((1, MIX_WIDTH)),
                  _resident((1, MIX_WIDTH))],
        out_specs=pl.BlockSpec((None, s, 2 * MIX_WIDTH), lambda i: (i, 0, 0)),
        out_shape=jax.ShapeDtypeStruct((b, s, 2 * MIX_WIDTH), BF16),
        compiler_params=_params(1),
        name="convgmlp",
    )(main3, main3, main3, main3, main3, w_conv, w_s, bs_rows, ln_g, ln_b)


def _pair(x, h):
    p = (h * HEAD_DIM) // LANES
    return x[:, p * LANES:(p + 1) * LANES]


def _head_in_pair_mask(rows, h):
    lane = lax.broadcasted_iota(jnp.int32, (rows, LANES), 1)
    first = lane < HEAD_DIM
    return first if (h * HEAD_DIM) % LANES == 0 else jnp.logical_not(first)


def _masked_head(x, h):
    xp = _pair(x, h)
    return jnp.where(_head_in_pair_mask(x.shape[0], h), xp, jnp.zeros_like(xp))


def _merge_heads(per_head):
    rows = per_head[0].shape[0]
    blocks = []
    for p in range(MIX_WIDTH // LANES):
        h0 = p * (LANES // HEAD_DIM)
        blocks.append(jnp.where(_head_in_pair_mask(rows, h0), per_head[h0], per_head[h0 + 1]))
    return jnp.concatenate(blocks, axis=1)


def _sb_kernel(q_ref, k_ref, v_ref, o_ref):
    i = pl.program_id(1)
    q = q_ref[pl.ds(pl.multiple_of(i * ATTN_TQ, ATTN_TQ), ATTN_TQ), :]
    qneg = [jnp.concatenate([_masked_head(q, HEADS_PER_PAIR * p + e) for e in range(HEADS_PER_PAIR)],
                            axis=0) * (-SCALE) for p in range(N_PAIRS)]
    r = lax.broadcasted_iota(jnp.int32, (ATTN_TK, ATTN_TK), 0)
    c = lax.broadcasted_iota(jnp.int32, (ATTN_TK, ATTN_TK), 1)
    suffix = jnp.where(r > c, 1.0, 0.0).astype(BF16)
    rows_all = N_HEADS * ATTN_TQ
    rq = lax.broadcasted_iota(jnp.int32, (rows_all, ATTN_TK), 0) & (ATTN_TQ - 1)
    cq = lax.broadcasted_iota(jnp.int32, (rows_all, ATTN_TK), 1)
    strict = cq < rq

    pair_rows = HEADS_PER_PAIR * ATTN_TQ

    def blocks(js, state, diag_first):
        carry, acc = state
        staged = []
        for n, j in enumerate(js):
            diag = diag_first and n == 0
            ks = pl.ds(pl.multiple_of(j * ATTN_TK, ATTN_TK), ATTN_TK)
            k = k_ref[ks, :]
            zn = jnp.concatenate(
                [_dot_nt(qneg[p], k[:, p * LANES:(p + 1) * LANES]) for p in range(N_PAIRS)], axis=0)
            zn2 = zn * LOG2_E
            l2 = jnp.minimum(zn2, 0.0) - jnp.log2(1.0 + jnp.exp2(-jnp.abs(zn2)))
            if diag:
                l2 = jnp.where(strict, l2, 0.0)
            staged.append((ks, diag, l2 - zn2, _dot(l2.astype(BF16), suffix),
                           jnp.sum(l2, axis=1, keepdims=True)))
        for ks, diag, log2_beta, later_in, row_sum in staged:
            w = jnp.exp2(later_in + carry + log2_beta)
            if diag:
                w = jnp.where(strict, w, 0.0)
            wb = w.astype(BF16)
            v = v_ref[ks, :]
            acc = acc + jnp.concatenate(
                [_dot(wb[p * pair_rows:(p + 1) * pair_rows, :], v[:, p * LANES:(p + 1) * LANES])
                 for p in range(N_PAIRS)], axis=0)
            carry = carry + row_sum
        return carry, acc

    def live(carry):
        return jnp.max(carry) > LOG2_F32_UNDERFLOW

    init = (jnp.zeros((rows_all, 1), F32), jnp.zeros((rows_all, LANES), F32))
    carry, acc = lax.cond(i == 0, lambda st: blocks([i], st, True),
                          lambda st: blocks([i, i - 1], st, True), init)
    first_left = i - 2
    n_pairs = (first_left + 1) // 2

    def pair_step(st):
        t, _, carry, acc = st
        carry, acc = blocks([first_left - 2 * t, first_left - 1 - 2 * t], (carry, acc), False)
        return t + 1, live(carry), carry, acc

    _, alive, carry, acc = lax.while_loop(lambda st: (st[0] < n_pairs) & st[1], pair_step,
                                          (jnp.int32(0), live(carry), carry, acc))
    odd_left = (first_left >= 0) & (first_left % 2 == 0)
    _, acc = lax.cond(odd_left & alive, lambda st: blocks([0], st, False), lambda st: st, (carry, acc))
    o_ref[...] = _merge_heads([acc[h * ATTN_TQ:(h + 1) * ATTN_TQ, :] for h in range(N_HEADS)]).astype(BF16)


def _sb_attention(main3):
    b, s, _ = main3.shape
    col = lambda k: pl.BlockSpec((None, s, MIX_WIDTH), lambda bi, i, k=k: (bi, 0, k))
    return pl.pallas_call(
        _sb_kernel,
        grid=(b, s // ATTN_TQ),
        in_specs=[col(5), col(6), col(7)],
        out_specs=pl.BlockSpec((None, ATTN_TQ, MIX_WIDTH), lambda bi, i: (bi, i, 0)),
        out_shape=jax.ShapeDtypeStruct((b, s, MIX_WIDTH), BF16),
        compiler_params=_params(2),
        name="sb",
    )(main3, main3, main3)


def _run_blocks(n, body, carry):
    carry = lax.fori_loop(0, n // 2, lambda t, c: body([2 * t, 2 * t + 1], c), carry)
    return lax.cond(n % 2 == 1, lambda c: body([n - 1], c), lambda c: c, carry)


def _fox_kernel(q_ref, k_ref, v_ref, f_ref, bf_ref, qg_ref, kg_ref, o_ref,
                qn_scr, kn_scr, qaug_scr, kaug_scr, s_scr):
    i = pl.program_id(1)
    seq = q_ref.shape[0]

    @pl.when(i == 0)
    def _prepare():
        r = lax.broadcasted_iota(jnp.int32, (MIX_WIDTH, MIX_WIDTH), 0) // HEAD_DIM
        c = lax.broadcasted_iota(jnp.int32, (MIX_WIDTH, MIX_WIDTH), 1) // HEAD_DIM
        same_head = jnp.where(r == c, 1.0, 0.0).astype(BF16)
        tr = lax.broadcasted_iota(jnp.int32, (FOX_PREP_ROWS, FOX_PREP_ROWS), 0)
        tc = lax.broadcasted_iota(jnp.int32, (FOX_PREP_ROWS, FOX_PREP_ROWS), 1)
        prefix = jnp.where(tc <= tr, 1.0, 0.0).astype(BF16)
        lane = lax.broadcasted_iota(jnp.int32, (FOX_PREP_ROWS, LANES), 1)
        n = N_HEADS

        def tile(tix, run):
            rows = pl.ds(pl.multiple_of(tix * FOX_PREP_ROWS, FOX_PREP_ROWS), FOX_PREP_ROWS)
            for x_ref, g_ref, scale, dst in ((q_ref, qg_ref, SCALE, qn_scr), (k_ref, kg_ref, 1.0, kn_scr)):
                x = x_ref[rows, :].astype(F32)
                hi, lo = _split_hi_lo(x * x)
                ms = (_dot(hi, same_head) + _dot(lo, same_head)) * (1.0 / HEAD_DIM)
                dst[rows, :] = (x * lax.rsqrt(ms + EPS) * (g_ref[...] * scale)).astype(BF16)
            hi, lo = _split_hi_lo(jax.nn.log_sigmoid(f_ref[rows, :] + bf_ref[...]))
            cum = run + _dot(prefix, hi) + _dot(prefix, lo)
            c_hi, c_mid, c_lo = _split3(cum)
            q_aug = jnp.where(lane < n, c_hi, jnp.where(lane < 2 * n, c_mid, jnp.where(
                lane < 3 * n, c_lo, jnp.where(lane < 6 * n, 1.0, 0.0))))
            k_aug = jnp.where(lane < 3 * n, 1.0, jnp.where(lane < 4 * n, -c_hi, jnp.where(
                lane < 5 * n, -c_mid, jnp.where(lane < 6 * n, -c_lo, 0.0))))
            qaug_scr[rows, :] = q_aug.astype(BF16)
            kaug_scr[rows, :] = k_aug.astype(BF16)
            return cum[FOX_PREP_ROWS - 1:FOX_PREP_ROWS, :]

        lax.fori_loop(0, seq // FOX_PREP_ROWS, tile, jnp.zeros((1, LANES), F32))

    qrows = pl.ds(pl.multiple_of(i * ATTN_TQ, ATTN_TQ), ATTN_TQ)
    q = qn_scr[qrows, :]
    qaug = qaug_scr[qrows, :]
    lane =lax.broadcasted_iota(jnp.int32, (ATTN_TQ, LANES), 1)
    qs = []
    for p in range(N_PAIRS):
        rows = []
        for e in range(HEADS_PER_PAIR):
            h = HEADS_PER_PAIR * p + e
            aug_h = jnp.where((lane & (N_HEADS - 1)) == h, qaug, jnp.zeros_like(qaug))
            rows.append(jnp.concatenate([_masked_head(q, h), aug_h], axis=1))
        qs.append(jnp.concatenate(rows, axis=0))
    rows_all = N_HEADS * ATTN_TQ
    pair_rows = HEADS_PER_PAIR * ATTN_TQ
    r = lax.broadcasted_iota(jnp.int32, (rows_all, ATTN_TK), 0) & (ATTN_TQ - 1)
    c = lax.broadcasted_iota(jnp.int32, (rows_all, ATTN_TK), 1)
    causal = c <= r

    def scores(j):
        ks = pl.ds(pl.multiple_of(j * ATTN_TK, ATTN_TK), ATTN_TK)
        kaug = kaug_scr[ks, :]
        return jnp.concatenate(
            [_dot_nt(qs[p], jnp.concatenate([kn_scr[ks, p * LANES:(p + 1) * LANES], kaug], axis=1))
             for p in range(N_PAIRS)], axis=0)

    def fold(x):
        return [x[:, n * LANES:(n + 1) * LANES] for n in range(ATTN_TK // LANES)]

    def stage(js, m_run):
        for j in js:
            s = scores(j)
            s_scr[:, pl.ds(pl.multiple_of(j * ATTN_TK, ATTN_TK), ATTN_TK)] = s
            for part in fold(s):
                m_run = jnp.maximum(m_run, part)
        return m_run

    s_diag = jnp.where(causal, scores(i), NEG_BIG)
    s_scr[:, pl.ds(pl.multiple_of(i * ATTN_TK, ATTN_TK), ATTN_TK)] = s_diag
    m_run = functools.reduce(jnp.maximum, fold(s_diag))
    m_run = _run_blocks(i, stage, m_run)
    m = jnp.max(m_run, axis=1, keepdims=True)

    def weigh(js, carry):
        l_run, acc = carry
        for j in js:
            ks = pl.ds(pl.multiple_of(j * ATTN_TK, ATTN_TK), ATTN_TK)
            p = jnp.exp(s_scr[:, ks] - m)
            for part in fold(p):
                l_run = l_run + part
            pb = p.astype(BF16)
            acc = acc + jnp.concatenate(
                [_dot(pb[pr * pair_rows:(pr + 1) * pair_rows, :], v_ref[ks, pr * LANES:(pr + 1) * LANES])
                 for pr in range(N_PAIRS)], axis=0)
        return l_run, acc

    zeros = jnp.zeros((rows_all, LANES), F32)
    l_run, acc = _run_blocks(i + 1, weigh, (zeros, zeros))
    out = acc / jnp.sum(l_run, axis=1, keepdims=True)
    o_ref[...] = _merge_heads([out[h * ATTN_TQ:(h + 1) * ATTN_TQ, :] for h in range(N_HEADS)]).astype(BF16)


def _fox_attention(main3, f3, bf_row, qg_row, kg_row):
    b, s, _ = main3.shape
    col = lambda k: pl.BlockSpec((None, s, MIX_WIDTH), lambda bi, i, k=k: (bi, 0, k))
    return pl.pallas_call(
        _fox_kernel,
        grid=(b, s // ATTN_TQ),
        in_specs=[col(8), col(9), col(10),
                  pl.BlockSpec((None, s, LANES), lambda bi, i: (bi, 0, 0)),
                  _resident((1, LANES)),
                  _resident((1, MIX_WIDTH)),
                  _resident((1, MIX_WIDTH))],
        out_specs=pl.BlockSpec((None, ATTN_TQ, MIX_WIDTH), lambda bi, i: (bi, i, 0)),
        out_shape=jax.ShapeDtypeStruct((b, s, MIX_WIDTH), BF16),
        scratch_shapes=[pltpu.VMEM((s, MIX_WIDTH), BF16),
                        pltpu.VMEM((s, MIX_WIDTH), BF16),
                        pltpu.VMEM((s, LANES), BF16),
                        pltpu.VMEM((s, LANES), BF16),
                        pltpu.VMEM((N_HEADS * ATTN_TQ, s), F32)],
        compiler_params=_params(2),
        name="fox",
    )(main3, main3, main3, f3, bf_row, qg_row, kg_row)


def _merge_kernel(h_ref, g_ref, yab_ref, yc_ref, yd_ref, wg_ref, wb_ref, wo_ref, o_ref, merged_scr):
    xn = _rms_norm_rows(h_ref[...], g_ref[...]).astype(BF16)
    ys = [yab_ref[:, 0:MIX_WIDTH], yab_ref[:, MIX_WIDTH:2 * MIX_WIDTH], yc_ref[...], yd_ref[...]]
    for c in range(D_MODEL // COL_CHUNK):
        cs = slice(c * COL_CHUNK, (c + 1) * COL_CHUNK)
        acc = None
        for n in range(N_BRANCH):
            gs = slice(n * D_MODEL + c * COL_CHUNK, n * D_MODEL + (c + 1) * COL_CHUNK)
            term = jax.nn.sigmoid(_dot(xn, wg_ref[:, gs])) * _dot(ys[n], wb_ref[n, :, cs])
            acc = term if acc is None else acc + term
        merged_scr[:, cs] = acc.astype(BF16)
    merged = merged_scr[...]
    for c in range(D_MODEL // COL_CHUNK):
        cs = slice(c * COL_CHUNK, (c + 1) * COL_CHUNK)
        o_ref[:, cs] = h_ref[:, cs] + _dot(merged, wo_ref[:, cs])


def _merge(h, g, yab, yc, yd, w_gate, w_branch, w_out):
    t = h.shape[0]
    rows = lambda w: pl.BlockSpec((TOKEN_TILE, w), lambda i: (i, 0))
    return pl.pallas_call(
        _merge_kernel,
        grid=(t // TOKEN_TILE,),
        in_specs=[rows(D_MODEL), _resident((1, D_MODEL)),
                  rows(2 * MIX_WIDTH), rows(MIX_WIDTH), rows(MIX_WIDTH),
                  _resident((D_MODEL, N_BRANCH * D_MODEL)),
                  _resident((N_BRANCH, MIX_WIDTH, D_MODEL)),
                  _resident((D_MODEL, D_MODEL))],
        out_specs=rows(D_MODEL),
        out_shape=jax.ShapeDtypeStruct((t, D_MODEL), F32),
        scratch_shapes=[pltpu.VMEM((TOKEN_TILE, D_MODEL), BF16)],
        compiler_params=_params(1),
        name="merge",
    )(h, g, yab, yc, yd, w_gate, w_branch, w_out)


def _ffn_kernel(h_ref, g_ref, wi_ref, wo_ref, o_ref, acc_scr):
    xn = _rms_norm_rows(h_ref[...], g_ref[...]).astype(BF16)
    for c in range(FFN_HIDDEN // COL_CHUNK):
        gate = _dot(xn, wi_ref[:, c * COL_CHUNK:(c + 1) * COL_CHUNK])
        up = _dot(xn, wi_ref[:, FFN_HIDDEN + c * COL_CHUNK:FFN_HIDDEN + (c + 1) * COL_CHUNK])
        act = (jax.nn.silu(gate) * up).astype(BF16)
        part = _dot(act, wo_ref[c * COL_CHUNK:(c + 1) * COL_CHUNK, :])
        if c == 0:
            acc_scr[...] = part
        else:
            acc_scr[...] += part
    o_ref[...] = h_ref[...] + acc_scr[...]


def _ffn(h, g, w_in, w_out):
    t = h.shape[0]
    rows = pl.BlockSpec((TOKEN_TILE, D_MODEL), lambda i: (i, 0))
    return pl.pallas_call(
        _ffn_kernel,
        grid=(t // TOKEN_TILE,),
        in_specs=[rows, _resident((1, D_MODEL)),
                  _resident((D_MODEL, 2 * FFN_HIDDEN)),
                  _resident((FFN_HIDDEN, D_MODEL))],
        out_specs=rows,
        out_shape=jax.ShapeDtypeStruct((t, D_MODEL), F32),
        scratch_shapes=[pltpu.VMEM((TOKEN_TILE, D_MODEL), F32)],
        compiler_params=_params(1),
        name="ffn",
    )(h, g, w_in, w_out)


def kernel(x, norm_mix_g, w_in, w_conv, w_spatial, b_spatial, gmlp_ln_g, gmlp_ln_b,
           fox_q_norm_g, fox_k_norm_g, fox_forget_b, w_branch, w_out, norm_ffn_g,
           w_ffn_in, w_ffn_out):
    b, s, d = x.shape
    depth = w_in.shape[0]
    assert d == D_MODEL and s % max(ATTN_TQ, ATTN_TK, GMLP_CHUNK, TOKEN_TILE) == 0
    t = b * s
    h = x.reshape(t, d)
    for l in range(depth):
        w_main = w_in[l, :, :N_MAIN].astype(BF16)
        n_f = FORGET_COPIES * N_HEADS
        w_f = jnp.pad(jnp.tile(w_in[l, :, N_MAIN:N_MAIN + N_HEADS], (1, FORGET_COPIES)),
                      ((0, 0), (0, LANES - n_f))).astype(BF16)
        w_gate = w_in[l, :, N_MAIN + N_HEADS:].astype(BF16)
        bf_row = jnp.pad(jnp.tile(fox_forget_b[l], FORGET_COPIES), (0, LANES - n_f)).reshape(1, LANES)
        qg_row = jnp.tile(fox_q_norm_g[l], N_HEADS).reshape(1, MIX_WIDTH)
        kg_row = jnp.tile(fox_k_norm_g[l], N_HEADS).reshape(1, MIX_WIDTH)
        bs_rows = jnp.repeat(b_spatial[l].T, GROUP_WIDTH, axis=1)

        main, f_raw = _proj(h, norm_mix_g[l].reshape(1, d), w_main, w_f)
        main3 = main.reshape(b, s, N_MAIN)
        yab = _convgmlp(main3, w_conv[l], w_spatial[l], bs_rows,
                        gmlp_ln_g[l].reshape(1, MIX_WIDTH), gmlp_ln_b[l].reshape(1, MIX_WIDTH))
        yc = _sb_attention(main3)
        yd = _fox_attention(main3, f_raw.reshape(b, s, LANES), bf_row, qg_row, kg_row)
        h = _merge(h, norm_mix_g[l].reshape(1, d), yab.reshape(t, 2 * MIX_WIDTH),
                   yc.reshape(t, MIX_WIDTH), yd.reshape(t, MIX_WIDTH),
                   w_gate, w_branch[l].astype(BF16), w_out[l].astype(BF16))
        h = _ffn(h, norm_ffn_g[l].reshape(1, d), w_ffn_in[l].astype(BF16), w_ffn_out[l].astype(BF16))
    return h.reshape(b, s, d)
```

```python
import functools

import jax
import jax.numpy as jnp
from jax import lax
from jax.experimental import pallas as pl
from jax.experimental.pallas import tpu as pltpu

D_MODEL = 1024
MIX_WIDTH = 256
HEAD_DIM = 64
N_HEADS = MIX_WIDTH // HEAD_DIM
N_BRANCH = 4
CONV_K = 3
GMLP_GROUPS = 4
GMLP_CHUNK = 128
GROUP_WIDTH = MIX_WIDTH // GMLP_GROUPS
FFN_HIDDEN = 2816
EPS = 1e-6
N_MAIN = 11 * MIX_WIDTH
FOX_Q_CHUNK = 8
FOX_K_CHUNK = 9
LANES = 128
HEADS_PER_PAIR = LANES // HEAD_DIM
N_PAIRS = MIX_WIDTH // LANES
FORGET_COPIES = 6
FOX_PREP_ROWS = 256
SUBLANES = 8
VMEM_LIMIT_BYTES = 56 * 1024 * 1024

TOKEN_TILE = 512
COL_CHUNK = 256
ATTN_TQ = 256
ATTN_TK = 256
SCALE = HEAD_DIM ** -0.5
LOG2_E = 1.4426950408889634
LOG2_F32_UNDERFLOW = -151.0
NEG_BIG = -1e30

F32 = jnp.float32
BF16 = jnp.bfloat16


def _dot(a, b):
    return jnp.dot(a, b, preferred_element_type=F32)


def _dot_nt(a, b):
    return lax.dot_general(a, b, (((1,), (1,)), ((), ())), preferred_element_type=F32)


def _split_hi_lo(x):
    hi = x.astype(BF16)
    lo = (x - hi.astype(F32)).astype(BF16)
    return hi, lo


def _rms_norm_rows(x, g):
    ms = jnp.mean(x * x, axis=-1, keepdims=True)
    return x * lax.rsqrt(ms + EPS) * g


def _resident(shape):
    return pl.BlockSpec(shape, lambda *_: (0,) * len(shape), pipeline_mode=pl.Buffered(1))


def _params(n_axes):
    return pltpu.CompilerParams(dimension_semantics=("arbitrary",) * n_axes,
                                vmem_limit_bytes=VMEM_LIMIT_BYTES)


def _split3(x):
    hi = x.astype(BF16).astype(F32)
    rem = x - hi
    mid = rem.astype(BF16).astype(F32)
    return hi, mid, rem - mid


def _proj_kernel(h_ref, g_ref, wm_ref, wf_ref, main_ref, f_ref):
    xn = _rms_norm_rows(h_ref[...], g_ref[...]).astype(BF16)
    for c in range(N_MAIN // COL_CHUNK):
        cs = slice(c * COL_CHUNK, (c + 1) * COL_CHUNK)
        main_ref[:, cs] = _dot(xn, wm_ref[:, cs]).astype(BF16)
    f_ref[...] = _dot(xn, wf_ref[...])


def _proj(h, g, w_main, w_f):
    t = h.shape[0]
    return pl.pallas_call(
        _proj_kernel,
        grid=(t // TOKEN_TILE,),
        in_specs=[pl.BlockSpec((TOKEN_TILE, D_MODEL), lambda i: (i, 0)),
                  _resident((1, D_MODEL)),
                  _resident((D_MODEL, N_MAIN)),
                  _resident((D_MODEL, LANES))],
        out_specs=[pl.BlockSpec((TOKEN_TILE, N_MAIN), lambda i: (i, 0)),
                   pl.BlockSpec((TOKEN_TILE, LANES), lambda i: (i, 0))],
        out_shape=[jax.ShapeDtypeStruct((t, N_MAIN), BF16),
                   jax.ShapeDtypeStruct((t, LANES), F32)],
        compiler_params=_params(1),
        name="proj",
    )(h, g, w_main, w_f)


def _convgmlp_kernel(cb_ref, cc_ref, cx_ref, u_ref, v_ref, wconv_ref, ws_ref, bs_ref,
                     lng_ref, lnb_ref, o_ref):
    seq = cb_ref.shape[0]
    n_chunks = seq // GMLP_CHUNK
    row = lax.broadcasted_iota(jnp.int32, (GMLP_CHUNK, GMLP_CHUNK), 0)
    col = lax.broadcasted_iota(jnp.int32, (GMLP_CHUNK, GMLP_CHUNK), 1)
    w_tril = [jnp.where(col <= row, ws_ref[gi], 0.0).astype(BF16) for gi in range(GMLP_GROUPS)]
    lane = lax.broadcasted_iota(jnp.int32, (GMLP_CHUNK, LANES), 1)
    first_group = lane < GROUP_WIDTH
    w0 = wconv_ref[0:1, :]
    w1 = wconv_ref[1:2, :]
    w2 = wconv_ref[2:3, :]

    def chunk(c, prev_tail):
        rows = pl.ds(pl.multiple_of(c * GMLP_CHUNK, GMLP_CHUNK), GMLP_CHUNK)
        xc = cc_ref[rows, :].astype(F32) * cx_ref[rows, :].astype(F32)
        win = jnp.concatenate([prev_tail, xc], axis=0)
        xc1 = pltpu.roll(win, 1, 0)[SUBLANES:, :]
        xc2 = pltpu.roll(win, 2, 0)[SUBLANES:, :]
        ya = cb_ref[rows, :].astype(F32) * (w0 * xc2 + w1 * xc1 + w2 * xc)
        o_ref[rows, 0:MIX_WIDTH] = ya.astype(BF16)

        gu = jax.nn.gelu(u_ref[rows, :].astype(F32))
        gv = jax.nn.gelu(v_ref[rows, :].astype(F32))
        mu = jnp.mean(gv, axis=-1, keepdims=True)
        cen = gv - mu
        var = jnp.mean(cen * cen, axis=-1, keepdims=True)
        vn = (cen * lax.rsqrt(var + EPS) * lng_ref[...] + lnb_ref[...]).astype(BF16)
        halves = []
        for lb in range(MIX_WIDTH // LANES):
            vb = vn[:, lb * LANES:(lb + 1) * LANES]
            m0 = _dot(w_tril[2 * lb], vb)
            m1 = _dot(w_tril[2 * lb + 1], vb)
            halves.append(jnp.where(first_group, m0, m1))
        mixed = jnp.concatenate(halves, axis=1) + bs_ref[...]
        o_ref[rows, MIX_WIDTH:2 * MIX_WIDTH] = (gu * mixed).astype(BF16)
        return xc[GMLP_CHUNK - SUBLANES:, :]

    lax.fori_loop(0, n_chunks, chunk, jnp.zeros((SUBLANES, MIX_WIDTH), F32))


def _convgmlp(main3, w_conv, w_s, bs_rows, ln_g, ln_b):
    b, s, _ = main3.shape
    col = lambda k: pl.BlockSpec((None, s, MIX_WIDTH), lambda i, k=k: (i, 0, k))
    return pl.pallas_call(
        _convgmlp_kernel,
        grid=(b,),
        in_specs=[col(0), col(1), col(2), col(3), col(4),
                  _resident((CONV_K, MIX_WIDTH)),
                  _resident((GMLP_GROUPS, GMLP_CHUNK, GMLP_CHUNK)),
                  _resident((GMLP_CHUNK, MIX_WIDTH)),
                  _resident((1, MIX_WIDTH)),
                  _resident((1, MIX_WIDTH))],
        out_specs=pl.BlockSpec((None, s, 2 * MIX_WIDTH), lambda i: (i, 0, 0)),
        out_shape=jax.ShapeDtypeStruct((b, s, 2 * MIX_WIDTH), BF16),
        compiler_params=_params(1),
        name="convgmlp",
    )(main3, main3, main3, main3, main3, w_conv, w_s, bs_rows, ln_g, ln_b)


def _pair(x, h):
    p = (h * HEAD_DIM) // LANES
    return x[:, p * LANES:(p + 1) * LANES]


def _head_in_pair_mask(rows, h):
    lane = lax.broadcasted_iota(jnp.int32, (rows, LANES), 1)
    first = lane < HEAD_DIM
    return first if (h * HEAD_DIM) % LANES == 0 else jnp.logical_not(first)


def _masked_head(x, h):
    xp = _pair(x, h)
    return jnp.where(_head_in_pair_mask(x.shape[0], h), xp, jnp.zeros_like(xp))


def _merge_heads(per_head):
    rows = per_head[0].shape[0]
    blocks = []
    for p in range(MIX_WIDTH // LANES):
        h0 = p * (LANES // HEAD_DIM)
        blocks.append(jnp.where(_head_in_pair_mask(rows, h0), per_head[h0], per_head[h0 + 1]))
    return jnp.concatenate(blocks, axis=1)


def _sb_kernel(q_ref, k_ref, v_ref, o_ref, carry_scr, acc_scr):
    i = pl.program_id(1)
    q = q_ref[pl.ds(pl.multiple_of(i * ATTN_TQ, ATTN_TQ), ATTN_TQ), :]
    qneg = [jnp.concatenate([_masked_head(q, HEADS_PER_PAIR * p + e) for e in range(HEADS_PER_PAIR)],
                            axis=0) * (-SCALE) for p in range(N_PAIRS)]
    r = lax.broadcasted_iota(jnp.int32, (ATTN_TK, ATTN_TK), 0)
    c = lax.broadcasted_iota(jnp.int32, (ATTN_TK, ATTN_TK), 1)
    suffix = jnp.where(r > c, 1.0, 0.0).astype(BF16)
    rows_all = N_HEADS * ATTN_TQ
    rq = lax.broadcasted_iota(jnp.int32, (rows_all, ATTN_TK), 0) & (ATTN_TQ - 1)
    cq = lax.broadcasted_iota(jnp.int32, (rows_all, ATTN_TK), 1)
    strict = cq < rq

    pair_rows = HEADS_PER_PAIR * ATTN_TQ

    def group(j_near, j_far, far_valid, diag_near):
        carry = carry_scr[...]
        staged = []
        for n, j in enumerate((j_near, j_far)):
            diag = diag_near and n == 0
            ks = pl.ds(pl.multiple_of(j * ATTN_TK, ATTN_TK), ATTN_TK)
            k = k_ref[ks, :]
            zn = jnp.concatenate(
                [_dot_nt(qneg[p], k[:, p * LANES:(p + 1) * LANES]) for p in range(N_PAIRS)], axis=0)
            zn2 = zn * LOG2_E
            l2 = jnp.minimum(zn2, 0.0) - jnp.log2(1.0 + jnp.exp2(-jnp.abs(zn2)))
            if diag:
                l2 = jnp.where(strict, l2, 0.0)
            staged.append((ks, diag, l2 - zn2, _dot(l2.astype(BF16), suffix),
                           jnp.sum(l2, axis=1, keepdims=True)))
        pv = None
        for n, (ks, diag, log2_beta, later_in, row_sum) in enumerate(staged):
            if n == 1:
                w = jnp.exp2(later_in + (carry + jnp.where(far_valid, 0.0, NEG_BIG)) + log2_beta)
                row_sum = jnp.where(far_valid, row_sum, 0.0)
            else:
                w = jnp.exp2(later_in + carry + log2_beta)
            if diag:
                w = jnp.where(strict, w, 0.0)
            wb = w.astype(BF16)
            v = v_ref[ks, :]
            contrib = jnp.concatenate(
                [_dot(wb[p * pair_rows:(p + 1) * pair_rows, :], v[:, p * LANES:(p + 1) * LANES])
                 for p in range(N_PAIRS)], axis=0)
            pv = contrib if pv is None else pv + contrib
            carry = carry + row_sum
        acc_scr[...] += pv
        carry_scr[...] = carry
        return jnp.max(carry) > LOG2_F32_UNDERFLOW

    carry_scr[...] = jnp.zeros_like(carry_scr)
    acc_scr[...] = jnp.zeros_like(acc_scr)
    alive = group(i, jnp.maximum(i - 1, 0), i >= 1, True)
    first_left = i - 2
    n_groups = i // 2

    def step(st):
        t, _ = st
        j_near = first_left - 2 * t
        return t + 1, group(j_near, jnp.maximum(j_near - 1, 0), j_near >= 1, False)

    lax.while_loop(lambda st: (st[0] < n_groups) & st[1], step, (jnp.int32(0), alive))
    acc = acc_scr[...]
    o_ref[...] = _merge_heads([acc[h * ATTN_TQ:(h + 1) * ATTN_TQ, :] for h in range(N_HEADS)]).astype(BF16)


def _sb_attention(main3):
    b, s, _ = main3.shape
    col = lambda k: pl.BlockSpec((None, s, MIX_WIDTH), lambda bi, i, k=k: (bi, 0, k))
    return pl.pallas_call(
        _sb_kernel,
        grid=(b, s // ATTN_TQ),
        in_specs=[col(5), col(6), col(7)],
        out_specs=pl.BlockSpec((None, ATTN_TQ, MIX_WIDTH), lambda bi, i: (bi, i, 0)),
        out_shape=jax.ShapeDtypeStruct((b, s, MIX_WIDTH), BF16),
        scratch_shapes=[pltpu.VMEM((N_HEADS * ATTN_TQ, 1), F32),
                        pltpu.VMEM((N_HEADS * ATTN_TQ, LANES), F32)],
        compiler_params=_params(2),
        name="sb",
    )(main3, main3, main3)


def _fox_kernel(q_ref, k_ref, v_ref, f_ref, bf_ref, qg_ref, kg_ref, o_ref,
                qn_scr, kn_scr, qaug_scr, kaug_scr, s_scr, mrun_scr, lrun_scr, acc_scr):
    i = pl.program_id(1)
    seq = q_ref.shape[0]

    @pl.when(i == 0)
    def _prepare():
        r = lax.broadcasted_iota(jnp.int32, (MIX_WIDTH, MIX_WIDTH), 0) // HEAD_DIM
        c = lax.broadcasted_iota(jnp.int32, (MIX_WIDTH, MIX_WIDTH), 1) // HEAD_DIM
        same_head = jnp.where(r == c, 1.0, 0.0).astype(BF16)
        tr = lax.broadcasted_iota(jnp.int32, (FOX_PREP_ROWS, FOX_PREP_ROWS), 0)
        tc = lax.broadcasted_iota(jnp.int32, (FOX_PREP_ROWS, FOX_PREP_ROWS), 1)
        prefix = jnp.where(tc <= tr, 1.0, 0.0).astype(BF16)
        lane = lax.broadcasted_iota(jnp.int32, (FOX_PREP_ROWS, LANES), 1)
        n = N_HEADS

        def tile(tix, run):
            rows = pl.ds(pl.multiple_of(tix * FOX_PREP_ROWS, FOX_PREP_ROWS), FOX_PREP_ROWS)
            for x_ref, g_ref, scale, dst in ((q_ref, qg_ref, SCALE, qn_scr), (k_ref, kg_ref, 1.0, kn_scr)):
                x = x_ref[rows, :].astype(F32)
                hi, lo = _split_hi_lo(x * x)
                ms = (_dot(hi, same_head) + _dot(lo, same_head)) * (1.0 / HEAD_DIM)
                dst[rows, :] = (x * lax.rsqrt(ms + EPS) * (g_ref[...] * scale)).astype(BF16)
            hi, lo = _split_hi_lo(jax.nn.log_sigmoid(f_ref[rows, :] + bf_ref[...]))
            cum = run + _dot(prefix, hi) + _dot(prefix, lo)
            c_hi, c_mid, c_lo = _split3(cum)
            q_aug = jnp.where(lane < n, c_hi, jnp.where(lane < 2 * n, c_mid, jnp.where(
                lane < 3 * n, c_lo, jnp.where(lane < 6 * n, 1.0, 0.0))))
            k_aug = jnp.where(lane < 3 * n, 1.0, jnp.where(lane < 4 * n, -c_hi, jnp.where(
                lane < 5 * n, -c_mid, jnp.where(lane < 6 * n, -c_lo, 0.0))))
            qaug_scr[rows, :] = q_aug.astype(BF16)
            kaug_scr[rows, :] = k_aug.astype(BF16)
            return cum[FOX_PREP_ROWS - 1:FOX_PREP_ROWS, :]

        lax.fori_loop(0, seq // FOX_PREP_ROWS, tile, jnp.zeros((1, LANES), F32))

    qrows = pl.ds(pl.multiple_of(i * ATTN_TQ, ATTN_TQ), ATTN_TQ)
    q = qn_scr[qrows, :]
    qaug = qaug_scr[qrows, :]
    lane =lax.broadcasted_iota(jnp.int32, (ATTN_TQ, LANES), 1)
    qs = []
    for p in range(N_PAIRS):
        rows = []
        for e in range(HEADS_PER_PAIR):
            h = HEADS_PER_PAIR * p + e
            aug_h = jnp.where((lane & (N_HEADS - 1)) == h, qaug, jnp.zeros_like(qaug))
            rows.append(jnp.concatenate([_masked_head(q, h), aug_h], axis=1))
        qs.append(jnp.concatenate(rows, axis=0))
    rows_all = N_HEADS * ATTN_TQ
    pair_rows = HEADS_PER_PAIR * ATTN_TQ
    r = lax.broadcasted_iota(jnp.int32, (rows_all, ATTN_TK), 0) & (ATTN_TQ - 1)
    c = lax.broadcasted_iota(jnp.int32, (rows_all, ATTN_TK), 1)
    causal = c <= r

    def scores(j):
        ks = pl.ds(pl.multiple_of(j * ATTN_TK, ATTN_TK), ATTN_TK)
        kaug = kaug_scr[ks, :]
        return jnp.concatenate(
            [_dot_nt(qs[p], jnp.concatenate([kn_scr[ks, p * LANES:(p + 1) * LANES], kaug], axis=1))
             for p in range(N_PAIRS)], axis=0)

    def fold(x):
        return [x[:, n * LANES:(n + 1) * LANES] for n in range(ATTN_TK // LANES)]

    def key_rows(j):
        return pl.ds(pl.multiple_of(j * ATTN_TK, ATTN_TK), ATTN_TK)

    s_diag = jnp.where(causal, scores(i), NEG_BIG)
    s_scr[:, key_rows(i)] = s_diag
    mrun_scr[...] = functools.reduce(jnp.maximum, fold(s_diag))

    def stage(t, _):
        parts = []
        for j in (2 * t, jnp.minimum(2 * t + 1, i - 1)):
            s = scores(j)
            s_scr[:, key_rows(j)] = s
            parts += fold(s)
        mrun_scr[...] = functools.reduce(jnp.maximum, parts, mrun_scr[...])
        return 0

    lax.fori_loop(0, (i + 1) // 2, stage, 0)
    m = jnp.max(mrun_scr[...], axis=1, keepdims=True)

    lrun_scr[...] = jnp.zeros_like(lrun_scr)
    acc_scr[...] = jnp.zeros_like(acc_scr)

    def weigh(t, _):
        j1 = 2 * t + 1
        m1 = m + jnp.where(j1 <= i, 0.0, -NEG_BIG)
        l_parts, pv = [], None
        for j, m_j in ((2 * t, m), (jnp.minimum(j1, i), m1)):
            ks = key_rows(j)
            p = jnp.exp(s_scr[:, ks] - m_j)
            l_parts += fold(p)
            pb = p.astype(BF16)
            contrib = jnp.concatenate(
                [_dot(pb[pr * pair_rows:(pr + 1) * pair_rows, :], v_ref[ks, pr * LANES:(pr + 1) * LANES])
                 for pr in range(N_PAIRS)], axis=0)
            pv = contrib if pv is None else pv + contrib
        lrun_scr[...] = functools.reduce(jnp.add, l_parts, lrun_scr[...])
        acc_scr[...] += pv
        return 0

    lax.fori_loop(0, (i + 2) // 2, weigh, 0)
    out = acc_scr[...] / jnp.sum(lrun_scr[...], axis=1, keepdims=True)
    o_ref[...] = _merge_heads([out[h * ATTN_TQ:(h + 1) * ATTN_TQ, :] for h in range(N_HEADS)]).astype(BF16)


def _fox_attention(main3, f3, bf_row, qg_row, kg_row):
    b, s, _ = main3.shape
    col = lambda k: pl.BlockSpec((None, s, MIX_WIDTH), lambda bi, i, k=k: (bi, 0, k))
    return pl.pallas_call(
        _fox_kernel,
        grid=(b, s // ATTN_TQ),
        in_specs=[col(8), col(9), col(10),
                  pl.BlockSpec((None, s, LANES), lambda bi, i: (bi, 0, 0)),
                  _resident((1, LANES)),
                  _resident((1, MIX_WIDTH)),
                  _resident((1, MIX_WIDTH))],
        out_specs=pl.BlockSpec((None, ATTN_TQ, MIX_WIDTH), lambda bi, i: (bi, i, 0)),
        out_shape=jax.ShapeDtypeStruct((b, s, MIX_WIDTH), BF16),
        scratch_shapes=[pltpu.VMEM((s, MIX_WIDTH), BF16),
                        pltpu.VMEM((s, MIX_WIDTH), BF16),
                        pltpu.VMEM((s, LANES), BF16),
                        pltpu.VMEM((s, LANES), BF16),
                        pltpu.VMEM((N_HEADS * ATTN_TQ, s), F32),
                        pltpu.VMEM((N_HEADS * ATTN_TQ, LANES), F32),
                        pltpu.VMEM((N_HEADS * ATTN_TQ, LANES), F32),
                        pltpu.VMEM((N_HEADS * ATTN_TQ, LANES), F32)],
        compiler_params=_params(2),
        name="fox",
    )(main3, main3, main3, f3, bf_row, qg_row, kg_row)


def _merge_kernel(h_ref, g_ref, yab_ref, yc_ref, yd_ref, wg_ref, wb_ref, wo_ref, o_ref, merged_scr):
    xn = _rms_norm_rows(h_ref[...], g_ref[...]).astype(BF16)
    ys = [yab_ref[:, 0:MIX_WIDTH], yab_ref[:, MIX_WIDTH:2 * MIX_WIDTH], yc_ref[...], yd_ref[...]]
    for c in range(D_MODEL // COL_CHUNK):
        cs = slice(c * COL_CHUNK, (c + 1) * COL_CHUNK)
        acc = None
        for n in range(N_BRANCH):
            gs = slice(n * D_MODEL + c * COL_CHUNK, n * D_MODEL + (c + 1) * COL_CHUNK)
            term = jax.nn.sigmoid(_dot(xn, wg_ref[:, gs])) * _dot(ys[n], wb_ref[n, :, cs])
            acc = term if acc is None else acc + term
        merged_scr[:, cs] = acc.astype(BF16)
    merged = merged_scr[...]
    for c in range(D_MODEL // COL_CHUNK):
        cs = slice(c * COL_CHUNK, (c + 1) * COL_CHUNK)
        o_ref[:, cs] = h_ref[:, cs] + _dot(merged, wo_ref[:, cs])


def _merge(h, g, yab, yc, yd, w_gate, w_branch, w_out):
    t = h.shape[0]
    rows = lambda w: pl.BlockSpec((TOKEN_TILE, w), lambda i: (i, 0))
    return pl.pallas_call(
        _merge_kernel,
        grid=(t // TOKEN_TILE,),
        in_specs=[rows(D_MODEL), _resident((1, D_MODEL)),
                  rows(2 * MIX_WIDTH), rows(MIX_WIDTH), rows(MIX_WIDTH),
                  _resident((D_MODEL, N_BRANCH * D_MODEL)),
                  _resident((N_BRANCH, MIX_WIDTH, D_MODEL)),
                  _resident((D_MODEL, D_MODEL))],
        out_specs=rows(D_MODEL),
        out_shape=jax.ShapeDtypeStruct((t, D_MODEL), F32),
        scratch_shapes=[pltpu.VMEM((TOKEN_TILE, D_MODEL), BF16)],
        compiler_params=_params(1),
        name="merge",
    )(h, g, yab, yc, yd, w_gate, w_branch, w_out)


def _ffn_kernel(h_ref, g_ref, wi_ref, wo_ref, o_ref, acc_scr):
    xn = _rms_norm_rows(h_ref[...], g_ref[...]).astype(BF16)
    for c in range(FFN_HIDDEN // COL_CHUNK):
        gate = _dot(xn, wi_ref[:, c * COL_CHUNK:(c + 1) * COL_CHUNK])
        up = _dot(xn, wi_ref[:, FFN_HIDDEN + c * COL_CHUNK:FFN_HIDDEN + (c + 1) * COL_CHUNK])
        act = (jax.nn.silu(gate) * up).astype(BF16)
        part = _dot(act, wo_ref[c * COL_CHUNK:(c + 1) * COL_CHUNK, :])
        if c == 0:
            acc_scr[...] = part
        else:
            acc_scr[...] += part
    o_ref[...] = h_ref[...] + acc_scr[...]


def _ffn(h, g, w_in, w_out):
    t = h.shape[0]
    rows = pl.BlockSpec((TOKEN_TILE, D_MODEL), lambda i: (i, 0))
    return pl.pallas_call(
        _ffn_kernel,
        grid=(t // TOKEN_TILE,),
        in_specs=[rows, _resident((1, D_MODEL)),
                  _resident((D_MODEL, 2 * FFN_HIDDEN)),
                  _resident((FFN_HIDDEN, D_MODEL))],
        out_specs=rows,
        out_shape=jax.ShapeDtypeStruct((t, D_MODEL), F32),
        scratch_shapes=[pltpu.VMEM((TOKEN_TILE, D_MODEL), F32)],
        compiler_params=_params(1),
        name="ffn",
    )(h, g, w_in, w_out)


def kernel(x, norm_mix_g, w_in, w_conv, w_spatial, b_spatial, gmlp_ln_g, gmlp_ln_b,
           fox_q_norm_g, fox_k_norm_g, fox_forget_b, w_branch, w_out, norm_ffn_g,
           w_ffn_in, w_ffn_out):
    b, s, d = x.shape
    depth = w_in.shape[0]
    assert d == D_MODEL and s % max(ATTN_TQ, ATTN_TK, GMLP_CHUNK, TOKEN_TILE) == 0
    t = b * s
    h = x.reshape(t, d)
    for l in range(depth):
        w_main = w_in[l, :, :N_MAIN].astype(BF16)
        n_f = FORGET_COPIES * N_HEADS
        w_f = jnp.pad(jnp.tile(w_in[l, :, N_MAIN:N_MAIN + N_HEADS], (1, FORGET_COPIES)),
                      ((0, 0), (0, LANES - n_f))).astype(BF16)
        w_gate = w_in[l, :, N_MAIN + N_HEADS:].astype(BF16)
        bf_row = jnp.pad(jnp.tile(fox_forget_b[l], FORGET_COPIES), (0, LANES - n_f)).reshape(1, LANES)
        qg_row = jnp.tile(fox_q_norm_g[l], N_HEADS).reshape(1, MIX_WIDTH)
        kg_row = jnp.tile(fox_k_norm_g[l], N_HEADS).reshape(1, MIX_WIDTH)
        bs_rows = jnp.repeat(b_spatial[l].T, GROUP_WIDTH, axis=1)

        main, f_raw = _proj(h, norm_mix_g[l].reshape(1, d), w_main, w_f)
        main3 = main.reshape(b, s, N_MAIN)
        yab = _convgmlp(main3, w_conv[l], w_spatial[l], bs_rows,
                        gmlp_ln_g[l].reshape(1, MIX_WIDTH), gmlp_ln_b[l].reshape(1, MIX_WIDTH))
        yc = _sb_attention(main3)
        yd = _fox_attention(main3, f_raw.reshape(b, s, LANES), bf_row, qg_row, kg_row)
        h = _merge(h, norm_mix_g[l].reshape(1, d), yab.reshape(t, 2 * MIX_WIDTH),
                   yc.reshape(t, MIX_WIDTH), yd.reshape(t, MIX_WIDTH),
                   w_gate, w_branch[l].astype(BF16), w_out[l].astype(BF16))
        h = _ffn(h, norm_ffn_g[l].reshape(1, d), w_ffn_in[l].astype(BF16), w_ffn_out[l].astype(BF16))
    return h.reshape(b, s, d)
```

```python
import functools

import jax
import jax.numpy as jnp
from jax import lax
from jax.experimental import pallas as pl
from jax.experimental.pallas import tpu as pltpu

D_MODEL = 1024
MIX_WIDTH = 256
HEAD_DIM = 64
N_HEADS = MIX_WIDTH // HEAD_DIM
N_BRANCH = 4
CONV_K = 3
GMLP_GROUPS = 4
GMLP_CHUNK = 128
GROUP_WIDTH = MIX_WIDTH // GMLP_GROUPS
FFN_HIDDEN = 2816
EPS = 1e-6
N_MAIN = 11 * MIX_WIDTH
FOX_Q_CHUNK = 8
FOX_K_CHUNK = 9
LANES = 128
HEADS_PER_PAIR = LANES // HEAD_DIM
N_PAIRS = MIX_WIDTH // LANES
FORGET_COPIES = 6
FOX_PREP_ROWS = 256
SUBLANES = 8
VMEM_LIMIT_BYTES = 56 * 1024 * 1024

TOKEN_TILE = 512
COL_CHUNK = 256
ATTN_TQ = 256
ATTN_TK = 256
SCALE = HEAD_DIM ** -0.5
LOG2_E = 1.4426950408889634
LOG2_F32_UNDERFLOW = -151.0
NEG_BIG = -1e30

F32 = jnp.float32
BF16 = jnp.bfloat16


def _dot(a, b):
    return jnp.dot(a, b, preferred_element_type=F32)


def _dot_nt(a, b):
    return lax.dot_general(a, b, (((1,), (1,)), ((), ())), preferred_element_type=F32)


def _split_hi_lo(x):
    hi = x.astype(BF16)
    lo = (x - hi.astype(F32)).astype(BF16)
    return hi, lo


def _rms_norm_rows(x, g):
    ms = jnp.mean(x * x, axis=-1, keepdims=True)
    return x * lax.rsqrt(ms + EPS) * g


def _resident(shape):
    return pl.BlockSpec(shape, lambda *_: (0,) * len(shape), pipeline_mode=pl.Buffered(1))


def _resident_layer(shape, layer):
    return pl.BlockSpec((None,) + tuple(shape), lambda *_: (layer,) + (0,) * len(shape),
                        pipeline_mode=pl.Buffered(1))


def _params(n_axes):
    return pltpu.CompilerParams(dimension_semantics=("arbitrary",) * n_axes,
                                vmem_limit_bytes=VMEM_LIMIT_BYTES)


def _split3(x):
    hi = x.astype(BF16).astype(F32)
    rem = x - hi
    mid = rem.astype(BF16).astype(F32)
    return hi, mid, rem - mid


def _proj_kernel(h_ref, g_ref, wm_ref, wf_ref, main_ref, f_ref):
    xn = _rms_norm_rows(h_ref[...], g_ref[...]).astype(BF16)
    for c in range(N_MAIN // COL_CHUNK):
        cs = slice(c * COL_CHUNK, (c + 1) * COL_CHUNK)
        main_ref[:, cs] = _dot(xn, wm_ref[:, cs]).astype(BF16)
    f_ref[...] = _dot(xn, wf_ref[...])


def _proj(h, g, w_in_all, layer, w_f):
    t = h.shape[0]
    return pl.pallas_call(
        _proj_kernel,
        grid=(t // TOKEN_TILE,),
        in_specs=[pl.BlockSpec((TOKEN_TILE, D_MODEL), lambda i: (i, 0)),
                  _resident((1, D_MODEL)),
                  _resident_layer((D_MODEL, N_MAIN), layer),
                  _resident((D_MODEL, LANES))],
        out_specs=[pl.BlockSpec((TOKEN_TILE, N_MAIN), lambda i: (i, 0)),
                   pl.BlockSpec((TOKEN_TILE, LANES), lambda i: (i, 0))],
        out_shape=[jax.ShapeDtypeStruct((t, N_MAIN), BF16),
                   jax.ShapeDtypeStruct((t, LANES), F32)],
        compiler_params=_params(1),
        name="proj",
    )(h, g, w_in_all, w_f)


def _convgmlp_kernel(cb_ref, cc_ref, cx_ref, u_ref, v_ref, wconv_ref, ws_ref, bs_ref,
                     lng_ref, lnb_ref, o_ref):
    seq = cb_ref.shape[0]
    n_chunks = seq // GMLP_CHUNK
    row = lax.broadcasted_iota(jnp.int32, (GMLP_CHUNK, GMLP_CHUNK), 0)
    col = lax.broadcasted_iota(jnp.int32, (GMLP_CHUNK, GMLP_CHUNK), 1)
    w_tril = [jnp.where(col <= row, ws_ref[gi], 0.0).astype(BF16) for gi in range(GMLP_GROUPS)]
    lane = lax.broadcasted_iota(jnp.int32, (GMLP_CHUNK, LANES), 1)
    first_group = lane < GROUP_WIDTH
    w0 = wconv_ref[0:1, :]
    w1 = wconv_ref[1:2, :]
    w2 = wconv_ref[2:3, :]

    def chunk(c, prev_tail):
        rows = pl.ds(pl.multiple_of(c * GMLP_CHUNK, GMLP_CHUNK), GMLP_CHUNK)
        xc = cc_ref[rows, :].astype(F32) * cx_ref[rows, :].astype(F32)
        win = jnp.concatenate([prev_tail, xc], axis=0)
        xc1 = pltpu.roll(win, 1, 0)[SUBLANES:, :]
        xc2 = pltpu.roll(win, 2, 0)[SUBLANES:, :]
        ya = cb_ref[rows, :].astype(F32) * (w0 * xc2 + w1 * xc1 + w2 * xc)
        o_ref[rows, 0:MIX_WIDTH] = ya.astype(BF16)

        gu = jax.nn.gelu(u_ref[rows, :].astype(F32))
        gv = jax.nn.gelu(v_ref[rows, :].astype(F32))
        mu = jnp.mean(gv, axis=-1, keepdims=True)
        cen = gv - mu
        var = jnp.mean(cen * cen, axis=-1, keepdims=True)
        vn = (cen * lax.rsqrt(var + EPS) * lng_ref[...] + lnb_ref[...]).astype(BF16)
        halves = []
        for lb in range(MIX_WIDTH // LANES):
            vb = vn[:, lb * LANES:(lb + 1) * LANES]
            m0 = _dot(w_tril[2 * lb], vb)
            m1 = _dot(w_tril[2 * lb + 1], vb)
            halves.append(jnp.where(first_group, m0, m1))
        mixed = jnp.concatenate(halves, axis=1) + bs_ref[...]
        o_ref[rows, MIX_WIDTH:2 * MIX_WIDTH] = (gu * mixed).astype(BF16)
        return xc[GMLP_CHUNK - SUBLANES:, :]

    lax.fori_loop(0, n_chunks // 2, lambda c2, tail: chunk(2 * c2 + 1, chunk(2 * c2, tail)),
                  jnp.zeros((SUBLANES, MIX_WIDTH), F32))


def _convgmlp(main3, w_conv, w_s, bs_rows, ln_g, ln_b):
    b, s, _ = main3.shape
    col = lambda k: pl.BlockSpec((None, s, MIX_WIDTH), lambda i, k=k: (i, 0, k))
    return pl.pallas_call(
        _convgmlp_kernel,
        grid=(b,),
        in_specs=[col(0), col(1), col(2), col(3), col(4),
                  _resident((CONV_K, MIX_WIDTH)),
                  _resident((GMLP_GROUPS, GMLP_CHUNK, GMLP_CHUNK)),
                  _resident((GMLP_CHUNK, MIX_WIDTH)),
                  _resident((1, MIX_WIDTH)),
                  _resident((1, MIX_WIDTH))],
        out_specs=pl.BlockSpec((None, s, 2 * MIX_WIDTH), lambda i: (i, 0, 0)),
        out_shape=jax.ShapeDtypeStruct((b, s, 2 * MIX_WIDTH), BF16),
        compiler_params=_params(1),
        name="convgmlp",
    )(main3, main3, main3, main3, main3, w_conv, w_s, bs_rows, ln_g, ln_b)


def _pair(x, h):
    p = (h * HEAD_DIM) // LANES
    return x[:, p * LANES:(p + 1) * LANES]


def _head_in_pair_mask(rows, h):
    lane = lax.broadcasted_iota(jnp.int32, (rows, LANES), 1)
    first = lane < HEAD_DIM
    return first if (h * HEAD_DIM) % LANES == 0 else jnp.logical_not(first)


def _masked_head(x, h):
    xp = _pair(x, h)
    return jnp.where(_head_in_pair_mask(x.shape[0], h), xp, jnp.zeros_like(xp))


def _merge_heads(per_head):
    rows = per_head[0].shape[0]
    blocks = []
    for p in range(MIX_WIDTH // LANES):
        h0 = p * (LANES // HEAD_DIM)
        blocks.append(jnp.where(_head_in_pair_mask(rows, h0), per_head[h0], per_head[h0 + 1]))
    return jnp.concatenate(blocks, axis=1)


def _sb_kernel(q_ref, k_ref, v_ref, o_ref, carry_scr, acc_scr):
    i = pl.program_id(1)
    q = q_ref[pl.ds(pl.multiple_of(i * ATTN_TQ, ATTN_TQ), ATTN_TQ), :]
    qneg = [jnp.concatenate([_masked_head(q, HEADS_PER_PAIR * p + e) for e in range(HEADS_PER_PAIR)],
                            axis=0) * (-SCALE) for p in range(N_PAIRS)]
    r = lax.broadcasted_iota(jnp.int32, (ATTN_TK, ATTN_TK), 0)
    c = lax.broadcasted_iota(jnp.int32, (ATTN_TK, ATTN_TK), 1)
    suffix = jnp.where(r > c, 1.0, 0.0).astype(BF16)
    rows_all = N_HEADS * ATTN_TQ
    rq = lax.broadcasted_iota(jnp.int32, (rows_all, ATTN_TK), 0) & (ATTN_TQ - 1)
    cq = lax.broadcasted_iota(jnp.int32, (rows_all, ATTN_TK), 1)
    strict = cq < rq

    pair_rows = HEADS_PER_PAIR * ATTN_TQ

    def group(j_near, j_far, far_valid, diag_near):
        carry = carry_scr[...]
        staged = []
        for n, j in enumerate((j_near, j_far)):
            diag = diag_near and n == 0
            ks = pl.ds(pl.multiple_of(j * ATTN_TK, ATTN_TK), ATTN_TK)
            k = k_ref[ks, :]
            zn = jnp.concatenate(
                [_dot_nt(qneg[p], k[:, p * LANES:(p + 1) * LANES]) for p in range(N_PAIRS)], axis=0)
            zn2 = zn * LOG2_E
            l2 = jnp.minimum(zn2, 0.0) - jnp.log2(1.0 + jnp.exp2(-jnp.abs(zn2)))
            if diag:
                l2 = jnp.where(strict, l2, 0.0)
            staged.append((ks, diag, l2 - zn2, _dot(l2.astype(BF16), suffix),
                           jnp.sum(l2, axis=1, keepdims=True)))
        pv = None
        for n, (ks, diag, log2_beta, later_in, row_sum) in enumerate(staged):
            if n == 1:
                w = jnp.exp2(later_in + (carry + jnp.where(far_valid, 0.0, NEG_BIG)) + log2_beta)
                row_sum = jnp.where(far_valid, row_sum, 0.0)
            else:
                w = jnp.exp2(later_in + carry + log2_beta)
            if diag:
                w = jnp.where(strict, w, 0.0)
            wb = w.astype(BF16)
            v = v_ref[ks, :]
            contrib = jnp.concatenate(
                [_dot(wb[p * pair_rows:(p + 1) * pair_rows, :], v[:, p * LANES:(p + 1) * LANES])
                 for p in range(N_PAIRS)], axis=0)
            pv = contrib if pv is None else pv + contrib
            carry = carry + row_sum
        acc_scr[...] += pv
        carry_scr[...] = carry
        return jnp.max(carry) > LOG2_F32_UNDERFLOW

    carry_scr[...] = jnp.zeros_like(carry_scr)
    acc_scr[...] = jnp.zeros_like(acc_scr)
    alive = group(i, jnp.maximum(i - 1, 0), i >= 1, True)
    first_left = i - 2
    n_groups = i // 2

    def step(st):
        t, _ = st
        j_near = first_left - 2 * t
        return t + 1, group(j_near, jnp.maximum(j_near - 1, 0), j_near >= 1, False)

    lax.while_loop(lambda st: (st[0] < n_groups) & st[1], step, (jnp.int32(0), alive))
    acc = acc_scr[...]
    o_ref[...] = _merge_heads([acc[h * ATTN_TQ:(h + 1) * ATTN_TQ, :] for h in range(N_HEADS)]).astype(BF16)


def _sb_attention(main3):
    b, s, _ = main3.shape
    col = lambda k: pl.BlockSpec((None, s, MIX_WIDTH), lambda bi, i, k=k: (bi, 0, k))
    return pl.pallas_call(
        _sb_kernel,
        grid=(b, s // ATTN_TQ),
        in_specs=[col(5), col(6), col(7)],
        out_specs=pl.BlockSpec((None, ATTN_TQ, MIX_WIDTH), lambda bi, i: (bi, i, 0)),
        out_shape=jax.ShapeDtypeStruct((b, s, MIX_WIDTH), BF16),
        scratch_shapes=[pltpu.VMEM((N_HEADS * ATTN_TQ, 1), F32),
                        pltpu.VMEM((N_HEADS * ATTN_TQ, LANES), F32)],
        compiler_params=_params(2),
        name="sb",
    )(main3, main3, main3)


def _fox_kernel(q_ref, k_ref, v_ref, f_ref, bf_ref, qg_ref, kg_ref, o_ref,
                qn_scr, kn_scr, qaug_scr, kaug_scr, s_scr, mrun_scr, lrun_scr, acc_scr):
    i = pl.program_id(1)
    seq = q_ref.shape[0]

    @pl.when(i == 0)
    def _prepare():
        r = lax.broadcasted_iota(jnp.int32, (MIX_WIDTH, MIX_WIDTH), 0) // HEAD_DIM
        c = lax.broadcasted_iota(jnp.int32, (MIX_WIDTH, MIX_WIDTH), 1) // HEAD_DIM
        same_head = jnp.where(r == c, 1.0, 0.0).astype(BF16)
        tr = lax.broadcasted_iota(jnp.int32, (FOX_PREP_ROWS, FOX_PREP_ROWS), 0)
        tc = lax.broadcasted_iota(jnp.int32, (FOX_PREP_ROWS, FOX_PREP_ROWS), 1)
        prefix = jnp.where(tc <= tr, 1.0, 0.0).astype(BF16)
        lane = lax.broadcasted_iota(jnp.int32, (FOX_PREP_ROWS, LANES), 1)
        n = N_HEADS

        def tile(tix, run):
            rows = pl.ds(pl.multiple_of(tix * FOX_PREP_ROWS, FOX_PREP_ROWS), FOX_PREP_ROWS)
            for x_ref, g_ref, scale, dst in ((q_ref, qg_ref, SCALE, qn_scr), (k_ref, kg_ref, 1.0, kn_scr)):
                x = x_ref[rows, :].astype(F32)
                hi, lo = _split_hi_lo(x * x)
                ms = (_dot(hi, same_head) + _dot(lo, same_head)) * (1.0 / HEAD_DIM)
                dst[rows, :] = (x * lax.rsqrt(ms + EPS) * (g_ref[...] * scale)).astype(BF16)
            hi, lo = _split_hi_lo(jax.nn.log_sigmoid(f_ref[rows, :] + bf_ref[...]))
            cum = run + _dot(prefix, hi) + _dot(prefix, lo)
            c_hi, c_mid, c_lo = _split3(cum)
            q_aug = jnp.where(lane < n, c_hi, jnp.where(lane < 2 * n, c_mid, jnp.where(
                lane < 3 * n, c_lo, jnp.where(lane < 6 * n, 1.0, 0.0))))
            k_aug = jnp.where(lane < 3 * n, 1.0, jnp.where(lane < 4 * n, -c_hi, jnp.where(
                lane < 5 * n, -c_mid, jnp.where(lane < 6 * n, -c_lo, 0.0))))
            qaug_scr[rows, :] = q_aug.astype(BF16)
            kaug_scr[rows, :] = k_aug.astype(BF16)
            return cum[FOX_PREP_ROWS - 1:FOX_PREP_ROWS, :]

        lax.fori_loop(0, seq // FOX_PREP_ROWS, tile, jnp.zeros((1, LANES), F32))

    qrows = pl.ds(pl.multiple_of(i * ATTN_TQ, ATTN_TQ), ATTN_TQ)
    q = qn_scr[qrows, :]
    qaug = qaug_scr[qrows, :]
    lane =lax.broadcasted_iota(jnp.int32, (ATTN_TQ, LANES), 1)
    qs = []
    for p in range(N_PAIRS):
        rows = []
        for e in range(HEADS_PER_PAIR):
            h = HEADS_PER_PAIR * p + e
            aug_h = jnp.where((lane & (N_HEADS - 1)) == h, qaug, jnp.zeros_like(qaug))
            rows.append(jnp.concatenate([_masked_head(q, h), aug_h], axis=1))
        qs.append(jnp.concatenate(rows, axis=0))
    rows_all = N_HEADS * ATTN_TQ
    pair_rows = HEADS_PER_PAIR * ATTN_TQ
    r = lax.broadcasted_iota(jnp.int32, (rows_all, ATTN_TK), 0) & (ATTN_TQ - 1)
    c = lax.broadcasted_iota(jnp.int32, (rows_all, ATTN_TK), 1)
    causal = c <= r

    def scores(j):
        ks = pl.ds(pl.multiple_of(j * ATTN_TK, ATTN_TK), ATTN_TK)
        kaug = kaug_scr[ks, :]
        return jnp.concatenate(
            [_dot_nt(qs[p], jnp.concatenate([kn_scr[ks, p * LANES:(p + 1) * LANES], kaug], axis=1))
             for p in range(N_PAIRS)], axis=0)

    def fold(x):
        return [x[:, n * LANES:(n + 1) * LANES] for n in range(ATTN_TK // LANES)]

    def key_rows(j):
        return pl.ds(pl.multiple_of(j * ATTN_TK, ATTN_TK), ATTN_TK)

    s_diag = jnp.where(causal, scores(i), NEG_BIG)
    s_scr[i] = s_diag
    mrun_scr[...] = functools.reduce(jnp.maximum, fold(s_diag))

    def stage(t, _):
        parts = []
        for j in (2 * t, jnp.minimum(2 * t + 1, i - 1)):
            s = scores(j)
            s_scr[j] = s
            parts += fold(s)
        mrun_scr[...] = functools.reduce(jnp.maximum, parts, mrun_scr[...])
        return 0

    lax.fori_loop(0, (i + 1) // 2, stage, 0)
    m = jnp.max(mrun_scr[...], axis=1, keepdims=True)

    lrun_scr[...] = jnp.zeros_like(lrun_scr)
    acc_scr[...] = jnp.zeros_like(acc_scr)

    def weigh(t, _):
        j1 = 2 * t + 1
        m1 = m + jnp.where(j1 <= i, 0.0, -NEG_BIG)
        l_parts, pv = [], None
        for j, m_j in ((2 * t, m), (jnp.minimum(j1, i), m1)):
            ks = key_rows(j)
            p = jnp.exp(s_scr[j] - m_j)
            l_parts += fold(p)
            pb = p.astype(BF16)
            contrib = jnp.concatenate(
                [_dot(pb[pr * pair_rows:(pr + 1) * pair_rows, :], v_ref[ks, pr * LANES:(pr + 1) * LANES])
                 for pr in range(N_PAIRS)], axis=0)
            pv = contrib if pv is None else pv + contrib
        lrun_scr[...] = functools.reduce(jnp.add, l_parts, lrun_scr[...])
        acc_scr[...] += pv
        return 0

    lax.fori_loop(0, (i + 2) // 2, weigh, 0)
    out = acc_scr[...] / jnp.sum(lrun_scr[...], axis=1, keepdims=True)
    o_ref[...] = _merge_heads([out[h * ATTN_TQ:(h + 1) * ATTN_TQ, :] for h in range(N_HEADS)]).astype(BF16)


def _fox_attention(main3, f3, bf_row, qg_row, kg_row):
    b, s, _ = main3.shape
    col = lambda k: pl.BlockSpec((None, s, MIX_WIDTH), lambda bi, i, k=k: (bi, 0, k))
    return pl.pallas_call(
        _fox_kernel,
        grid=(b, s // ATTN_TQ),
        in_specs=[col(8), col(9), col(10),
                  pl.BlockSpec((None, s, LANES), lambda bi, i: (bi, 0, 0)),
                  _resident((1, LANES)),
                  _resident((1, MIX_WIDTH)),
                  _resident((1, MIX_WIDTH))],
        out_specs=pl.BlockSpec((None, ATTN_TQ, MIX_WIDTH), lambda bi, i: (bi, i, 0)),
        out_shape=jax.ShapeDtypeStruct((b, s, MIX_WIDTH), BF16),
        scratch_shapes=[pltpu.VMEM((s, MIX_WIDTH), BF16),
                        pltpu.VMEM((s, MIX_WIDTH), BF16),
                        pltpu.VMEM((s, LANES), BF16),
                        pltpu.VMEM((s, LANES), BF16),
                        pltpu.VMEM((s // ATTN_TK, N_HEADS * ATTN_TQ, ATTN_TK), F32),
                        pltpu.VMEM((N_HEADS * ATTN_TQ, LANES), F32),
                        pltpu.VMEM((N_HEADS * ATTN_TQ, LANES), F32),
                        pltpu.VMEM((N_HEADS * ATTN_TQ, LANES), F32)],
        compiler_params=_params(2),
        name="fox",
    )(main3, main3, main3, f3, bf_row, qg_row, kg_row)


def _merge_kernel(h_ref, g_ref, yab_ref, yc_ref, yd_ref, wg_ref, wb_ref, wo_ref, o_ref, merged_scr):
    xn = _rms_norm_rows(h_ref[...], g_ref[...]).astype(BF16)
    ys = [yab_ref[:, 0:MIX_WIDTH], yab_ref[:, MIX_WIDTH:2 * MIX_WIDTH], yc_ref[...], yd_ref[...]]
    for c in range(D_MODEL // COL_CHUNK):
        cs = slice(c * COL_CHUNK, (c + 1) * COL_CHUNK)
        acc = None
        for n in range(N_BRANCH):
            gs = slice(n * D_MODEL + c * COL_CHUNK, n * D_MODEL + (c + 1) * COL_CHUNK)
            term = jax.nn.sigmoid(_dot(xn, wg_ref[:, gs])) * _dot(ys[n], wb_ref[n, :, cs])
            acc = term if acc is None else acc + term
        merged_scr[:, cs] = acc.astype(BF16)
    merged = merged_scr[...]
    for c in range(D_MODEL // COL_CHUNK):
        cs = slice(c * COL_CHUNK, (c + 1) * COL_CHUNK)
        o_ref[:, cs] = h_ref[:, cs] + _dot(merged, wo_ref[:, cs])


def _merge(h, g, yab, yc, yd, w_gate, w_branch, w_out, layer):
    t = h.shape[0]
    rows = lambda w: pl.BlockSpec((TOKEN_TILE, w), lambda i: (i, 0))
    return pl.pallas_call(
        _merge_kernel,
        grid=(t // TOKEN_TILE,),
        in_specs=[rows(D_MODEL), _resident((1, D_MODEL)),
                  rows(2 * MIX_WIDTH), rows(MIX_WIDTH), rows(MIX_WIDTH),
                  _resident_layer((D_MODEL, N_BRANCH * D_MODEL), layer),
                  _resident_layer((N_BRANCH, MIX_WIDTH, D_MODEL), layer),
                  _resident_layer((D_MODEL, D_MODEL), layer)],
        out_specs=rows(D_MODEL),
        out_shape=jax.ShapeDtypeStruct((t, D_MODEL), F32),
        scratch_shapes=[pltpu.VMEM((TOKEN_TILE, D_MODEL), BF16)],
        compiler_params=_params(1),
        name="merge",
    )(h, g, yab, yc, yd, w_gate, w_branch, w_out)


def _ffn_kernel(h_ref, g_ref, wi_ref, wo_ref, o_ref, acc_scr):
    xn = _rms_norm_rows(h_ref[...], g_ref[...]).astype(BF16)
    for c in range(FFN_HIDDEN // COL_CHUNK):
        gate = _dot(xn, wi_ref[:, c * COL_CHUNK:(c + 1) * COL_CHUNK])
        up = _dot(xn, wi_ref[:, FFN_HIDDEN + c * COL_CHUNK:FFN_HIDDEN + (c + 1) * COL_CHUNK])
        act = (jax.nn.silu(gate) * up).astype(BF16)
        part = _dot(act, wo_ref[c * COL_CHUNK:(c + 1) * COL_CHUNK, :])
        if c == 0:
            acc_scr[...] = part
        else:
            acc_scr[...] += part
    o_ref[...] = h_ref[...] + acc_scr[...]


def _ffn(h, g, w_in, w_out, layer):
    t = h.shape[0]
    rows = pl.BlockSpec((TOKEN_TILE, D_MODEL), lambda i: (i, 0))
    return pl.pallas_call(
        _ffn_kernel,
        grid=(t // TOKEN_TILE,),
        in_specs=[rows, _resident((1, D_MODEL)),
                  _resident_layer((D_MODEL, 2 * FFN_HIDDEN), layer),
                  _resident_layer((FFN_HIDDEN, D_MODEL), layer)],
        out_specs=rows,
        out_shape=jax.ShapeDtypeStruct((t, D_MODEL), F32),
        scratch_shapes=[pltpu.VMEM((TOKEN_TILE, D_MODEL), F32)],
        compiler_params=_params(1),
        name="ffn",
    )(h, g, w_in, w_out)


def kernel(x, norm_mix_g, w_in, w_conv, w_spatial, b_spatial, gmlp_ln_g, gmlp_ln_b,
           fox_q_norm_g, fox_k_norm_g, fox_forget_b, w_branch, w_out, norm_ffn_g,
           w_ffn_in, w_ffn_out):
    b, s, d = x.shape
    depth = w_in.shape[0]
    assert d == D_MODEL and s % max(ATTN_TQ, ATTN_TK, GMLP_CHUNK, TOKEN_TILE) == 0
    t = b * s
    h = x.reshape(t, d)
    w_in_b = w_in.astype(BF16)
    w_gate_b = w_in_b[:, :, N_MAIN + N_HEADS:]
    w_branch_b = w_branch.astype(BF16)
    w_out_b = w_out.astype(BF16)
    w_ffn_in_b = w_ffn_in.astype(BF16)
    w_ffn_out_b = w_ffn_out.astype(BF16)
    n_f = FORGET_COPIES * N_HEADS
    for l in range(depth):
        w_f = jnp.pad(jnp.tile(w_in_b[l, :, N_MAIN:N_MAIN + N_HEADS], (1, FORGET_COPIES)),
                      ((0, 0), (0, LANES - n_f)))
        bf_row = jnp.pad(jnp.tile(fox_forget_b[l], FORGET_COPIES), (0, LANES - n_f)).reshape(1, LANES)
        qg_row = jnp.tile(fox_q_norm_g[l], N_HEADS).reshape(1, MIX_WIDTH)
        kg_row = jnp.tile(fox_k_norm_g[l], N_HEADS).reshape(1, MIX_WIDTH)
        bs_rows = jnp.repeat(b_spatial[l].T, GROUP_WIDTH, axis=1)

        main, f_raw = _proj(h, norm_mix_g[l].reshape(1, d), w_in_b, l, w_f)
        main3 = main.reshape(b, s, N_MAIN)
        yab = _convgmlp(main3, w_conv[l], w_spatial[l], bs_rows,
                        gmlp_ln_g[l].reshape(1, MIX_WIDTH), gmlp_ln_b[l].reshape(1, MIX_WIDTH))
        yc = _sb_attention(main3)
        yd = _fox_attention(main3, f_raw.reshape(b, s, LANES), bf_row, qg_row, kg_row)
        h = _merge(h, norm_mix_g[l].reshape(1, d), yab.reshape(t, 2 * MIX_WIDTH),
                   yc.reshape(t, MIX_WIDTH), yd.reshape(t, MIX_WIDTH),
                   w_gate_b, w_branch_b, w_out_b, l)
        h = _ffn(h, norm_ffn_g[l].reshape(1, d), w_ffn_in_b, w_ffn_out_b, l)
    return h.reshape(b, s, d)
```

```python
import functools

import jax
import jax.numpy as jnp
from jax import lax
from jax.experimental import pallas as pl
from jax.experimental.pallas import tpu as pltpu

D_MODEL = 1024
MIX_WIDTH = 256
HEAD_DIM = 64
N_HEADS = MIX_WIDTH // HEAD_DIM
N_BRANCH = 4
CONV_K = 3
GMLP_GROUPS = 4
GMLP_CHUNK = 128
GROUP_WIDTH = MIX_WIDTH // GMLP_GROUPS
FFN_HIDDEN = 2816
EPS = 1e-6
N_MAIN = 11 * MIX_WIDTH
FOX_Q_CHUNK = 8
FOX_K_CHUNK = 9
LANES = 128
HEADS_PER_PAIR = LANES // HEAD_DIM
N_PAIRS = MIX_WIDTH // LANES
FORGET_COPIES = 6
FOX_PREP_ROWS = 256
SUBLANES = 8
VMEM_LIMIT_BYTES = 56 * 1024 * 1024

TOKEN_TILE = 1024
COL_CHUNK = 256
ATTN_TQ = 256
ATTN_TK = 256
SCALE = HEAD_DIM ** -0.5
LOG2_E = 1.4426950408889634
LOG2_F32_UNDERFLOW = -151.0
NEG_BIG = -1e30

F32 = jnp.float32
BF16 = jnp.bfloat16


def _dot(a, b):
    return jnp.dot(a, b, preferred_element_type=F32)


def _dot_nt(a, b):
    return lax.dot_general(a, b, (((1,), (1,)), ((), ())), preferred_element_type=F32)


def _split_hi_lo(x):
    hi = x.astype(BF16)
    lo = (x - hi.astype(F32)).astype(BF16)
    return hi, lo


def _rms_norm_rows(x, g):
    ms = jnp.mean(x * x, axis=-1, keepdims=True)
    return x * lax.rsqrt(ms + EPS) * g


def _resident(shape):
    return pl.BlockSpec(shape, lambda *_: (0,) * len(shape), pipeline_mode=pl.Buffered(1))


def _resident_layer(shape, layer):
    return pl.BlockSpec((None,) + tuple(shape), lambda *_: (layer,) + (0,) * len(shape),
                        pipeline_mode=pl.Buffered(1))


def _params(n_axes):
    return pltpu.CompilerParams(dimension_semantics=("arbitrary",) * n_axes,
                                vmem_limit_bytes=VMEM_LIMIT_BYTES)


def _split3(x):
    hi = x.astype(BF16).astype(F32)
    rem = x - hi
    mid = rem.astype(BF16).astype(F32)
    return hi, mid, rem - mid


def _proj_kernel(h_ref, g_ref, wm_ref, wf_ref, main_ref, f_ref):
    xn = _rms_norm_rows(h_ref[...], g_ref[...]).astype(BF16)
    for c in range(N_MAIN // COL_CHUNK):
        cs = slice(c * COL_CHUNK, (c + 1) * COL_CHUNK)
        main_ref[:, cs] = _dot(xn, wm_ref[:, cs]).astype(BF16)
    f_ref[...] = _dot(xn, wf_ref[...])


def _proj(h, g, w_in_all, layer, w_f):
    t = h.shape[0]
    return pl.pallas_call(
        _proj_kernel,
        grid=(t // TOKEN_TILE,),
        in_specs=[pl.BlockSpec((TOKEN_TILE, D_MODEL), lambda i: (i, 0)),
                  _resident((1, D_MODEL)),
                  _resident_layer((D_MODEL, N_MAIN), layer),
                  _resident((D_MODEL, LANES))],
        out_specs=[pl.BlockSpec((TOKEN_TILE, N_MAIN), lambda i: (i, 0)),
                   pl.BlockSpec((TOKEN_TILE, LANES), lambda i: (i, 0))],
        out_shape=[jax.ShapeDtypeStruct((t, N_MAIN), BF16),
                   jax.ShapeDtypeStruct((t, LANES), F32)],
        compiler_params=_params(1),
        name="proj",
    )(h, g, w_in_all, w_f)


def _convgmlp_kernel(cb_ref, cc_ref, cx_ref, u_ref, v_ref, wconv_ref, ws_ref, bs_ref,
                     lng_ref, lnb_ref, o_ref):
    seq = cb_ref.shape[0]
    n_chunks = seq // GMLP_CHUNK
    row = lax.broadcasted_iota(jnp.int32, (GMLP_CHUNK, GMLP_CHUNK), 0)
    col = lax.broadcasted_iota(jnp.int32, (GMLP_CHUNK, GMLP_CHUNK), 1)
    w_tril = [jnp.where(col <= row, ws_ref[gi], 0.0).astype(BF16) for gi in range(GMLP_GROUPS)]
    lane = lax.broadcasted_iota(jnp.int32, (GMLP_CHUNK, LANES), 1)
    first_group = lane < GROUP_WIDTH
    w0 = wconv_ref[0:1, :]
    w1 = wconv_ref[1:2, :]
    w2 = wconv_ref[2:3, :]

    def chunk(c, prev_tail):
        rows = pl.ds(pl.multiple_of(c * GMLP_CHUNK, GMLP_CHUNK), GMLP_CHUNK)
        xc = cc_ref[rows, :].astype(F32) * cx_ref[rows, :].astype(F32)
        win = jnp.concatenate([prev_tail, xc], axis=0)
        xc1 = pltpu.roll(win, 1, 0)[SUBLANES:, :]
        xc2 = pltpu.roll(win, 2, 0)[SUBLANES:, :]
        ya = cb_ref[rows, :].astype(F32) * (w0 * xc2 + w1 * xc1 + w2 * xc)
        o_ref[rows, 0:MIX_WIDTH] = ya.astype(BF16)

        gu = jax.nn.gelu(u_ref[rows, :].astype(F32))
        gv = jax.nn.gelu(v_ref[rows, :].astype(F32))
        mu = jnp.mean(gv, axis=-1, keepdims=True)
        cen = gv - mu
        var = jnp.mean(cen * cen, axis=-1, keepdims=True)
        vn = (cen * lax.rsqrt(var + EPS) * lng_ref[...] + lnb_ref[...]).astype(BF16)
        halves = []
        for lb in range(MIX_WIDTH // LANES):
            vb = vn[:, lb * LANES:(lb + 1) * LANES]
            m0 = _dot(w_tril[2 * lb], vb)
            m1 = _dot(w_tril[2 * lb + 1], vb)
            halves.append(jnp.where(first_group, m0, m1))
        mixed = jnp.concatenate(halves, axis=1) + bs_ref[...]
        o_ref[rows, MIX_WIDTH:2 * MIX_WIDTH] = (gu * mixed).astype(BF16)
        return xc[GMLP_CHUNK - SUBLANES:, :]

    lax.fori_loop(0, n_chunks // 2, lambda c2, tail: chunk(2 * c2 + 1, chunk(2 * c2, tail)),
                  jnp.zeros((SUBLANES, MIX_WIDTH), F32))


def _convgmlp(main3, w_conv, w_s, bs_rows, ln_g, ln_b):
    b, s, _ = main3.shape
    col = lambda k: pl.BlockSpec((None, s, MIX_WIDTH), lambda i, k=k: (i, 0, k))
    return pl.pallas_call(
        _convgmlp_kernel,
        grid=(b,),
        in_specs=[col(0), col(1), col(2), col(3), col(4),
                  _resident((CONV_K, MIX_WIDTH)),
                  _resident((GMLP_GROUPS, GMLP_CHUNK, GMLP_CHUNK)),
                  _resident((GMLP_CHUNK, MIX_WIDTH)),
                  _resident((1, MIX_WIDTH)),
                  _resident((1, MIX_WIDTH))],
        out_specs=pl.BlockSpec((None, s, 2 * MIX_WIDTH), lambda i: (i, 0, 0)),
        out_shape=jax.ShapeDtypeStruct((b, s, 2 * MIX_WIDTH), BF16),
        compiler_params=_params(1),
        name="convgmlp",
    )(main3, main3, main3, main3, main3, w_conv, w_s, bs_rows, ln_g, ln_b)


def _pair(x, h):
    p = (h * HEAD_DIM) // LANES
    return x[:, p * LANES:(p + 1) * LANES]


def _head_in_pair_mask(rows, h):
    lane = lax.broadcasted_iota(jnp.int32, (rows, LANES), 1)
    first = lane < HEAD_DIM
    return first if (h * HEAD_DIM) % LANES == 0 else jnp.logical_not(first)


def _masked_head(x, h):
    xp = _pair(x, h)
    return jnp.where(_head_in_pair_mask(x.shape[0], h), xp, jnp.zeros_like(xp))


def _merge_heads(per_head):
    rows = per_head[0].shape[0]
    blocks = []
    for p in range(MIX_WIDTH // LANES):
        h0 = p * (LANES // HEAD_DIM)
        blocks.append(jnp.where(_head_in_pair_mask(rows, h0), per_head[h0], per_head[h0 + 1]))
    return jnp.concatenate(blocks, axis=1)


def _sb_kernel(q_ref, k_ref, v_ref, o_ref, carry_scr, acc_scr):
    i = pl.program_id(1)
    q = q_ref[pl.ds(pl.multiple_of(i * ATTN_TQ, ATTN_TQ), ATTN_TQ), :]
    qneg = [jnp.concatenate([_masked_head(q, HEADS_PER_PAIR * p + e) for e in range(HEADS_PER_PAIR)],
                            axis=0) * (-SCALE) for p in range(N_PAIRS)]
    r = lax.broadcasted_iota(jnp.int32, (ATTN_TK, ATTN_TK), 0)
    c = lax.broadcasted_iota(jnp.int32, (ATTN_TK, ATTN_TK), 1)
    suffix = jnp.where(r > c, 1.0, 0.0).astype(BF16)
    rows_all = N_HEADS * ATTN_TQ
    rq = lax.broadcasted_iota(jnp.int32, (rows_all, ATTN_TK), 0) & (ATTN_TQ - 1)
    cq = lax.broadcasted_iota(jnp.int32, (rows_all, ATTN_TK), 1)
    strict = cq < rq

    pair_rows = HEADS_PER_PAIR * ATTN_TQ

    def group(j_near, j_far, far_valid, diag_near):
        carry = carry_scr[...]
        staged = []
        for n, j in enumerate((j_near, j_far)):
            diag = diag_near and n == 0
            ks = pl.ds(pl.multiple_of(j * ATTN_TK, ATTN_TK), ATTN_TK)
            k = k_ref[ks, :]
            zn = jnp.concatenate(
                [_dot_nt(qneg[p], k[:, p * LANES:(p + 1) * LANES]) for p in range(N_PAIRS)], axis=0)
            zn2 = zn * LOG2_E
            l2 = jnp.minimum(zn2, 0.0) - jnp.log2(1.0 + jnp.exp2(-jnp.abs(zn2)))
            if diag:
                l2 = jnp.where(strict, l2, 0.0)
            staged.append((ks, diag, l2 - zn2, _dot(l2.astype(BF16), suffix),
                           jnp.sum(l2, axis=1, keepdims=True)))
        pv = None
        for n, (ks, diag, log2_beta, later_in, row_sum) in enumerate(staged):
            if n == 1:
                w = jnp.exp2(later_in + (carry + jnp.where(far_valid, 0.0, NEG_BIG)) + log2_beta)
                row_sum = jnp.where(far_valid, row_sum, 0.0)
            else:
                w = jnp.exp2(later_in + carry + log2_beta)
            if diag:
                w = jnp.where(strict, w, 0.0)
            wb = w.astype(BF16)
            v = v_ref[ks, :]
            contrib = jnp.concatenate(
                [_dot(wb[p * pair_rows:(p + 1) * pair_rows, :], v[:, p * LANES:(p + 1) * LANES])
                 for p in range(N_PAIRS)], axis=0)
            pv = contrib if pv is None else pv + contrib
            carry = carry + row_sum
        acc_scr[...] += pv
        carry_scr[...] = carry
        return jnp.max(carry) > LOG2_F32_UNDERFLOW

    carry_scr[...] = jnp.zeros_like(carry_scr)
    acc_scr[...] = jnp.zeros_like(acc_scr)
    alive = group(i, jnp.maximum(i - 1, 0), i >= 1, True)
    first_left = i - 2
    n_groups = i // 2

    def step(st):
        t, _ = st
        j_near = first_left - 2 * t
        return t + 1, group(j_near, jnp.maximum(j_near - 1, 0), j_near >= 1, False)

    lax.while_loop(lambda st: (st[0] < n_groups) & st[1], step, (jnp.int32(0), alive))
    acc = acc_scr[...]
    o_ref[...] = _merge_heads([acc[h * ATTN_TQ:(h + 1) * ATTN_TQ, :] for h in range(N_HEADS)]).astype(BF16)


def _sb_attention(main3):
    b, s, _ = main3.shape
    col = lambda k: pl.BlockSpec((None, s, MIX_WIDTH), lambda bi, i, k=k: (bi, 0, k))
    return pl.pallas_call(
        _sb_kernel,
        grid=(b, s // ATTN_TQ),
        in_specs=[col(5), col(6), col(7)],
        out_specs=pl.BlockSpec((None, ATTN_TQ, MIX_WIDTH), lambda bi, i: (bi, i, 0)),
        out_shape=jax.ShapeDtypeStruct((b, s, MIX_WIDTH), BF16),
        scratch_shapes=[pltpu.VMEM((N_HEADS * ATTN_TQ, 1), F32),
                        pltpu.VMEM((N_HEADS * ATTN_TQ, LANES), F32)],
        compiler_params=_params(2),
        name="sb",
    )(main3, main3, main3)


def _fox_kernel(q_ref, k_ref, v_ref, f_ref, bf_ref, qg_ref, kg_ref, o_ref,
                qn_scr, kn_scr, qaug_scr, kaug_scr, s_scr, mrun_scr, lrun_scr, acc_scr):
    i = pl.program_id(1)
    seq = q_ref.shape[0]

    @pl.when(i == 0)
    def _prepare():
        r = lax.broadcasted_iota(jnp.int32, (MIX_WIDTH, MIX_WIDTH), 0) // HEAD_DIM
        c = lax.broadcasted_iota(jnp.int32, (MIX_WIDTH, MIX_WIDTH), 1) // HEAD_DIM
        same_head = jnp.where(r == c, 1.0, 0.0).astype(BF16)
        tr = lax.broadcasted_iota(jnp.int32, (FOX_PREP_ROWS, FOX_PREP_ROWS), 0)
        tc = lax.broadcasted_iota(jnp.int32, (FOX_PREP_ROWS, FOX_PREP_ROWS), 1)
        prefix = jnp.where(tc <= tr, 1.0, 0.0).astype(BF16)
        lane = lax.broadcasted_iota(jnp.int32, (FOX_PREP_ROWS, LANES), 1)
        n = N_HEADS

        def tile(tix, run):
            rows = pl.ds(pl.multiple_of(tix * FOX_PREP_ROWS, FOX_PREP_ROWS), FOX_PREP_ROWS)
            for x_ref, g_ref, scale, dst in ((q_ref, qg_ref, SCALE, qn_scr), (k_ref, kg_ref, 1.0, kn_scr)):
                x = x_ref[rows, :].astype(F32)
                hi, lo = _split_hi_lo(x * x)
                ms = (_dot(hi, same_head) + _dot(lo, same_head)) * (1.0 / HEAD_DIM)
                dst[rows, :] = (x * lax.rsqrt(ms + EPS) * (g_ref[...] * scale)).astype(BF16)
            hi, lo = _split_hi_lo(jax.nn.log_sigmoid(f_ref[rows, :] + bf_ref[...]))
            cum = run + _dot(prefix, hi) + _dot(prefix, lo)
            c_hi, c_mid, c_lo = _split3(cum)
            q_aug = jnp.where(lane < n, c_hi, jnp.where(lane < 2 * n, c_mid, jnp.where(
                lane < 3 * n, c_lo, jnp.where(lane < 6 * n, 1.0, 0.0))))
            k_aug = jnp.where(lane < 3 * n, 1.0, jnp.where(lane < 4 * n, -c_hi, jnp.where(
                lane < 5 * n, -c_mid, jnp.where(lane < 6 * n, -c_lo, 0.0))))
            qaug_scr[rows, :] = q_aug.astype(BF16)
            kaug_scr[rows, :] = k_aug.astype(BF16)
            return cum[FOX_PREP_ROWS - 1:FOX_PREP_ROWS, :]

        lax.fori_loop(0, seq // FOX_PREP_ROWS, tile, jnp.zeros((1, LANES), F32))

    qrows = pl.ds(pl.multiple_of(i * ATTN_TQ, ATTN_TQ), ATTN_TQ)
    q = qn_scr[qrows, :]
    qaug = qaug_scr[qrows, :]
    lane =lax.broadcasted_iota(jnp.int32, (ATTN_TQ, LANES), 1)
    qs = []
    for p in range(N_PAIRS):
        rows = []
        for e in range(HEADS_PER_PAIR):
            h = HEADS_PER_PAIR * p + e
            aug_h = jnp.where((lane & (N_HEADS - 1)) == h, qaug, jnp.zeros_like(qaug))
            rows.append(jnp.concatenate([_masked_head(q, h), aug_h], axis=1))
        qs.append(jnp.concatenate(rows, axis=0))
    rows_all = N_HEADS * ATTN_TQ
    pair_rows = HEADS_PER_PAIR * ATTN_TQ
    r = lax.broadcasted_iota(jnp.int32, (rows_all, ATTN_TK), 0) & (ATTN_TQ - 1)
    c = lax.broadcasted_iota(jnp.int32, (rows_all, ATTN_TK), 1)
    causal = c <= r

    def scores(j):
        ks = pl.ds(pl.multiple_of(j * ATTN_TK, ATTN_TK), ATTN_TK)
        kaug = kaug_scr[ks, :]
        return jnp.concatenate(
            [_dot_nt(qs[p], jnp.concatenate([kn_scr[ks, p * LANES:(p + 1) * LANES], kaug], axis=1))
             for p in range(N_PAIRS)], axis=0) * LOG2_E

    def fold(x):
        return [x[:, n * LANES:(n + 1) * LANES] for n in range(ATTN_TK // LANES)]

    def key_rows(j):
        return pl.ds(pl.multiple_of(j * ATTN_TK, ATTN_TK), ATTN_TK)

    s_diag = jnp.where(causal, scores(i), NEG_BIG)
    s_scr[i] = s_diag
    mrun_scr[...] = functools.reduce(jnp.maximum, fold(s_diag))

    def stage(t, _):
        parts = []
        for j in (2 * t, jnp.minimum(2 * t + 1, i - 1)):
            s = scores(j)
            s_scr[j] = s
            parts += fold(s)
        mrun_scr[...] = functools.reduce(jnp.maximum, parts, mrun_scr[...])
        return 0

    lax.fori_loop(0, (i + 1) // 2, stage, 0)
    m = jnp.max(mrun_scr[...], axis=1, keepdims=True)

    @pl.when(i % 2 == 0)
    def _():
        s_scr[i + 1] = jnp.full((rows_all, ATTN_TK), NEG_BIG, F32)

    lrun_scr[...] = jnp.zeros_like(lrun_scr)
    acc_scr[...] = jnp.zeros_like(acc_scr)

    def weigh(t, _):
        l_parts, pv = [], None
        for j in (2 * t, 2 * t + 1):
            ks = key_rows(jnp.minimum(j, seq // ATTN_TK - 1))
            p = jnp.exp2(s_scr[j] - m)
            l_parts += fold(p)
            pb = p.astype(BF16)
            contrib = jnp.concatenate(
                [_dot(pb[pr * pair_rows:(pr + 1) * pair_rows, :], v_ref[ks, pr * LANES:(pr + 1) * LANES])
                 for pr in range(N_PAIRS)], axis=0)
            pv = contrib if pv is None else pv + contrib
        lrun_scr[...] = functools.reduce(jnp.add, l_parts, lrun_scr[...])
        acc_scr[...] += pv
        return 0

    lax.fori_loop(0, (i + 2) // 2, weigh, 0)
    out = acc_scr[...] / jnp.sum(lrun_scr[...], axis=1, keepdims=True)
    o_ref[...] = _merge_heads([out[h * ATTN_TQ:(h + 1) * ATTN_TQ, :] for h in range(N_HEADS)]).astype(BF16)


def _fox_attention(main3, f3, bf_row, qg_row, kg_row):
    b, s, _ = main3.shape
    col = lambda k: pl.BlockSpec((None, s, MIX_WIDTH), lambda bi, i, k=k: (bi, 0, k))
    return pl.pallas_call(
        _fox_kernel,
        grid=(b, s // ATTN_TQ),
        in_specs=[col(8), col(9), col(10),
                  pl.BlockSpec((None, s, LANES), lambda bi, i: (bi, 0, 0)),
                  _resident((1, LANES)),
                  _resident((1, MIX_WIDTH)),
                  _resident((1, MIX_WIDTH))],
        out_specs=pl.BlockSpec((None, ATTN_TQ, MIX_WIDTH), lambda bi, i: (bi, i, 0)),
        out_shape=jax.ShapeDtypeStruct((b, s, MIX_WIDTH), BF16),
        scratch_shapes=[pltpu.VMEM((s, MIX_WIDTH), BF16),
                        pltpu.VMEM((s, MIX_WIDTH), BF16),
                        pltpu.VMEM((s, LANES), BF16),
                        pltpu.VMEM((s, LANES), BF16),
                        pltpu.VMEM((s // ATTN_TK + 1, N_HEADS * ATTN_TQ, ATTN_TK), F32),
                        pltpu.VMEM((N_HEADS * ATTN_TQ, LANES), F32),
                        pltpu.VMEM((N_HEADS * ATTN_TQ, LANES), F32),
                        pltpu.VMEM((N_HEADS * ATTN_TQ, LANES), F32)],
        compiler_params=_params(2),
        name="fox",
    )(main3, main3, main3, f3, bf_row, qg_row, kg_row)


def _merge_kernel(h_ref, g_ref, yab_ref, yc_ref, yd_ref, wg_ref, wb_ref, wo_ref, o_ref, merged_scr):
    xn = _rms_norm_rows(h_ref[...], g_ref[...]).astype(BF16)
    ys = [yab_ref[:, 0:MIX_WIDTH], yab_ref[:, MIX_WIDTH:2 * MIX_WIDTH], yc_ref[...], yd_ref[...]]
    for c in range(D_MODEL // COL_CHUNK):
        cs = slice(c * COL_CHUNK, (c + 1) * COL_CHUNK)
        acc = None
        for n in range(N_BRANCH):
            gs = slice(n * D_MODEL + c * COL_CHUNK, n * D_MODEL + (c + 1) * COL_CHUNK)
            term = jax.nn.sigmoid(_dot(xn, wg_ref[:, gs])) * _dot(ys[n], wb_ref[n, :, cs])
            acc = term if acc is None else acc + term
        merged_scr[:, cs] = acc.astype(BF16)
    merged = merged_scr[...]
    for c in range(D_MODEL // COL_CHUNK):
        cs = slice(c * COL_CHUNK, (c + 1) * COL_CHUNK)
        o_ref[:, cs] = h_ref[:, cs] + _dot(merged, wo_ref[:, cs])


def _merge(h, g, yab, yc, yd, w_gate, w_branch, w_out, layer):
    t = h.shape[0]
    rows = lambda w: pl.BlockSpec((TOKEN_TILE, w), lambda i: (i, 0))
    return pl.pallas_call(
        _merge_kernel,
        grid=(t // TOKEN_TILE,),
        in_specs=[rows(D_MODEL), _resident((1, D_MODEL)),
                  rows(2 * MIX_WIDTH), rows(MIX_WIDTH), rows(MIX_WIDTH),
                  _resident_layer((D_MODEL, N_BRANCH * D_MODEL), layer),
                  _resident_layer((N_BRANCH, MIX_WIDTH, D_MODEL), layer),
                  _resident_layer((D_MODEL, D_MODEL), layer)],
        out_specs=rows(D_MODEL),
        out_shape=jax.ShapeDtypeStruct((t, D_MODEL), F32),
        scratch_shapes=[pltpu.VMEM((TOKEN_TILE, D_MODEL), BF16)],
        compiler_params=_params(1),
        name="merge",
    )(h, g, yab, yc, yd, w_gate, w_branch, w_out)


def _ffn_kernel(h_ref, g_ref, wi_ref, wo_ref, o_ref, acc_scr):
    xn = _rms_norm_rows(h_ref[...], g_ref[...]).astype(BF16)
    for c in range(FFN_HIDDEN // COL_CHUNK):
        gate = _dot(xn, wi_ref[:, c * COL_CHUNK:(c + 1) * COL_CHUNK])
        up = _dot(xn, wi_ref[:, FFN_HIDDEN + c * COL_CHUNK:FFN_HIDDEN + (c + 1) * COL_CHUNK])
        act = (jax.nn.silu(gate) * up).astype(BF16)
        part = _dot(act, wo_ref[c * COL_CHUNK:(c + 1) * COL_CHUNK, :])
        if c == 0:
            acc_scr[...] = part
        else:
            acc_scr[...] += part
    o_ref[...] = h_ref[...] + acc_scr[...]


def _ffn(h, g, w_in, w_out, layer):
    t = h.shape[0]
    rows = pl.BlockSpec((TOKEN_TILE, D_MODEL), lambda i: (i, 0))
    return pl.pallas_call(
        _ffn_kernel,
        grid=(t // TOKEN_TILE,),
        in_specs=[rows, _resident((1, D_MODEL)),
                  _resident_layer((D_MODEL, 2 * FFN_HIDDEN), layer),
                  _resident_layer((FFN_HIDDEN, D_MODEL), layer)],
        out_specs=rows,
        out_shape=jax.ShapeDtypeStruct((t, D_MODEL), F32),
        scratch_shapes=[pltpu.VMEM((TOKEN_TILE, D_MODEL), F32)],
        compiler_params=_params(1),
        name="ffn",
    )(h, g, w_in, w_out)


def kernel(x, norm_mix_g, w_in, w_conv, w_spatial, b_spatial, gmlp_ln_g, gmlp_ln_b,
           fox_q_norm_g, fox_k_norm_g, fox_forget_b, w_branch, w_out, norm_ffn_g,
           w_ffn_in, w_ffn_out):
    b, s, d = x.shape
    depth = w_in.shape[0]
    assert d == D_MODEL and s % max(ATTN_TQ, ATTN_TK, GMLP_CHUNK, TOKEN_TILE) == 0
    t = b * s
    h = x.reshape(t, d)
    w_in_b = w_in.astype(BF16)
    w_gate_b = w_in_b[:, :, N_MAIN + N_HEADS:]
    w_branch_b = w_branch.astype(BF16)
    w_out_b = w_out.astype(BF16)
    w_ffn_in_b = w_ffn_in.astype(BF16)
    w_ffn_out_b = w_ffn_out.astype(BF16)
    n_f = FORGET_COPIES * N_HEADS
    for l in range(depth):
        w_f = jnp.pad(jnp.tile(w_in_b[l, :, N_MAIN:N_MAIN + N_HEADS], (1, FORGET_COPIES)),
                      ((0, 0), (0, LANES - n_f)))
        bf_row = jnp.pad(jnp.tile(fox_forget_b[l], FORGET_COPIES), (0, LANES - n_f)).reshape(1, LANES)
        qg_row = jnp.tile(fox_q_norm_g[l], N_HEADS).reshape(1, MIX_WIDTH)
        kg_row = jnp.tile(fox_k_norm_g[l], N_HEADS).reshape(1, MIX_WIDTH)
        bs_rows = jnp.repeat(b_spatial[l].T, GROUP_WIDTH, axis=1)

        main, f_raw = _proj(h, norm_mix_g[l].reshape(1, d), w_in_b, l, w_f)
        main3 = main.reshape(b, s, N_MAIN)
        yab = _convgmlp(main3, w_conv[l], w_spatial[l], bs_rows,
                        gmlp_ln_g[l].reshape(1, MIX_WIDTH), gmlp_ln_b[l].reshape(1, MIX_WIDTH))
        yc = _sb_attention(main3)
        yd = _fox_attention(main3, f_raw.reshape(b, s, LANES), bf_row, qg_row, kg_row)
        h = _merge(h, norm_mix_g[l].reshape(1, d), yab.reshape(t, 2 * MIX_WIDTH),
                   yc.reshape(t, MIX_WIDTH), yd.reshape(t, MIX_WIDTH),
                   w_gate_b, w_branch_b, w_out_b, l)
        h = _ffn(h, norm_ffn_g[l].reshape(1, d), w_ffn_in_b, w_ffn_out_b, l)
    return h.reshape(b, s, d)
```

```python
import functools

import jax
import jax.numpy as jnp
from jax import lax
from jax.experimental import pallas as pl
from jax.experimental.pallas import tpu as pltpu

D_MODEL = 1024
MIX_WIDTH = 256
HEAD_DIM = 64
N_HEADS = MIX_WIDTH // HEAD_DIM
N_BRANCH = 4
CONV_K = 3
GMLP_GROUPS = 4
GMLP_CHUNK = 128
GROUP_WIDTH = MIX_WIDTH // GMLP_GROUPS
FFN_HIDDEN = 2816
EPS = 1e-6
N_MAIN = 11 * MIX_WIDTH
FOX_Q_CHUNK = 8
FOX_K_CHUNK = 9
LANES = 128
HEADS_PER_PAIR = LANES // HEAD_DIM
N_PAIRS = MIX_WIDTH // LANES
FORGET_COPIES = 6
FOX_PREP_ROWS = 256
SUBLANES = 8
VMEM_LIMIT_BYTES = 56 * 1024 * 1024

TOKEN_TILE = 1024
COL_CHUNK = 256
ATTN_TQ = 256
ATTN_TK = 256
SCALE = HEAD_DIM ** -0.5
LOG2_E = 1.4426950408889634
LOG2_F32_UNDERFLOW = -151.0
NEG_BIG = -1e30

F32 = jnp.float32
BF16 = jnp.bfloat16


def _dot(a, b):
    return jnp.dot(a, b, preferred_element_type=F32)


def _dot_nt(a, b):
    return lax.dot_general(a, b, (((1,), (1,)), ((), ())), preferred_element_type=F32)


def _split_hi_lo(x):
    hi = x.astype(BF16)
    lo = (x - hi.astype(F32)).astype(BF16)
    return hi, lo


def _rms_norm_rows(x, g):
    ms = jnp.mean(x * x, axis=-1, keepdims=True)
    return x * lax.rsqrt(ms + EPS) * g


def _resident(shape):
    return pl.BlockSpec(shape, lambda *_: (0,) * len(shape), pipeline_mode=pl.Buffered(1))


def _resident_layer(shape, layer):
    return pl.BlockSpec((None,) + tuple(shape), lambda *_: (layer,) + (0,) * len(shape),
                        pipeline_mode=pl.Buffered(1))


def _params(n_axes):
    return pltpu.CompilerParams(dimension_semantics=("arbitrary",) * n_axes,
                                vmem_limit_bytes=VMEM_LIMIT_BYTES)


def _split3(x):
    hi = x.astype(BF16).astype(F32)
    rem = x - hi
    mid = rem.astype(BF16).astype(F32)
    return hi, mid, rem - mid


def _proj_kernel(h_ref, g_ref, wm_ref, wf_ref, main_ref, f_ref):
    xn = _rms_norm_rows(h_ref[...], g_ref[...]).astype(BF16)
    for c in range(N_MAIN // COL_CHUNK):
        cs = slice(c * COL_CHUNK, (c + 1) * COL_CHUNK)
        main_ref[:, cs] = _dot(xn, wm_ref[:, cs]).astype(BF16)
    f_ref[...] = _dot(xn, wf_ref[...])


def _proj(h, g, w_in_all, layer, w_f):
    t = h.shape[0]
    return pl.pallas_call(
        _proj_kernel,
        grid=(t // TOKEN_TILE,),
        in_specs=[pl.BlockSpec((TOKEN_TILE, D_MODEL), lambda i: (i, 0)),
                  _resident((1, D_MODEL)),
                  _resident_layer((D_MODEL, N_MAIN), layer),
                  _resident((D_MODEL, LANES))],
        out_specs=[pl.BlockSpec((TOKEN_TILE, N_MAIN), lambda i: (i, 0)),
                   pl.BlockSpec((TOKEN_TILE, LANES), lambda i: (i, 0))],
        out_shape=[jax.ShapeDtypeStruct((t, N_MAIN), BF16),
                   jax.ShapeDtypeStruct((t, LANES), F32)],
        compiler_params=_params(1),
        name="proj",
    )(h, g, w_in_all, w_f)


def _convgmlp_kernel(cb_ref, cc_ref, cx_ref, u_ref, v_ref, wconv_ref, ws_ref, bs_ref,
                     lng_ref, lnb_ref, o_ref):
    seq = cb_ref.shape[0]
    n_chunks = seq // GMLP_CHUNK
    row = lax.broadcasted_iota(jnp.int32, (GMLP_CHUNK, GMLP_CHUNK), 0)
    col = lax.broadcasted_iota(jnp.int32, (GMLP_CHUNK, GMLP_CHUNK), 1)
    w_tril = [jnp.where(col <= row, ws_ref[gi], 0.0).astype(BF16) for gi in range(GMLP_GROUPS)]
    lane = lax.broadcasted_iota(jnp.int32, (GMLP_CHUNK, LANES), 1)
    first_group = lane < GROUP_WIDTH
    w0 = wconv_ref[0:1, :]
    w1 = wconv_ref[1:2, :]
    w2 = wconv_ref[2:3, :]

    def chunk(c, prev_tail):
        rows = pl.ds(pl.multiple_of(c * GMLP_CHUNK, GMLP_CHUNK), GMLP_CHUNK)
        xc = cc_ref[rows, :].astype(F32) * cx_ref[rows, :].astype(F32)
        win = jnp.concatenate([prev_tail, xc], axis=0)
        xc1 = pltpu.roll(win, 1, 0)[SUBLANES:, :]
        xc2 = pltpu.roll(win, 2, 0)[SUBLANES:, :]
        ya = cb_ref[rows, :].astype(F32) * (w0 * xc2 + w1 * xc1 + w2 * xc)
        o_ref[rows, 0:MIX_WIDTH] = ya.astype(BF16)

        gu = jax.nn.gelu(u_ref[rows, :].astype(F32))
        gv = jax.nn.gelu(v_ref[rows, :].astype(F32))
        mu = jnp.mean(gv, axis=-1, keepdims=True)
        cen = gv - mu
        var = jnp.mean(cen * cen, axis=-1, keepdims=True)
        vn = (cen * lax.rsqrt(var + EPS) * lng_ref[...] + lnb_ref[...]).astype(BF16)
        halves = []
        for lb in range(MIX_WIDTH // LANES):
            vb = vn[:, lb * LANES:(lb + 1) * LANES]
            m0 = _dot(w_tril[2 * lb], vb)
            m1 = _dot(w_tril[2 * lb + 1], vb)
            halves.append(jnp.where(first_group, m0, m1))
        mixed = jnp.concatenate(halves, axis=1) + bs_ref[...]
        o_ref[rows, MIX_WIDTH:2 * MIX_WIDTH] = (gu * mixed).astype(BF16)
        return xc[GMLP_CHUNK - SUBLANES:, :]

    lax.fori_loop(0, n_chunks // 2, lambda c2, tail: chunk(2 * c2 + 1, chunk(2 * c2, tail)),
                  jnp.zeros((SUBLANES, MIX_WIDTH), F32))


def _convgmlp(main3, w_conv, w_s, bs_rows, ln_g, ln_b):
    b, s, _ = main3.shape
    col = lambda k: pl.BlockSpec((None, s, MIX_WIDTH), lambda i, k=k: (i, 0, k))
    return pl.pallas_call(
        _convgmlp_kernel,
        grid=(b,),
        in_specs=[col(0), col(1), col(2), col(3), col(4),
                  _resident((CONV_K, MIX_WIDTH)),
                  _resident((GMLP_GROUPS, GMLP_CHUNK, GMLP_CHUNK)),
                  _resident((GMLP_CHUNK, MIX_WIDTH)),
                  _resident((1, MIX_WIDTH)),
                  _resident((1, MIX_WIDTH))],
        out_specs=pl.BlockSpec((None, s, 2 * MIX_WIDTH), lambda i: (i, 0, 0)),
        out_shape=jax.ShapeDtypeStruct((b, s, 2 * MIX_WIDTH), BF16),
        compiler_params=_params(1),
        name="convgmlp",
    )(main3, main3, main3, main3, main3, w_conv, w_s, bs_rows, ln_g, ln_b)


def _pair(x, h):
    p = (h * HEAD_DIM) // LANES
    return x[:, p * LANES:(p + 1) * LANES]


def _head_in_pair_mask(rows, h):
    lane = lax.broadcasted_iota(jnp.int32, (rows, LANES), 1)
    first = lane < HEAD_DIM
    return first if (h * HEAD_DIM) % LANES == 0 else jnp.logical_not(first)


def _masked_head(x, h):
    xp = _pair(x, h)
    return jnp.where(_head_in_pair_mask(x.shape[0], h), xp, jnp.zeros_like(xp))


def _merge_heads(per_head):
    rows = per_head[0].shape[0]
    blocks = []
    for p in range(MIX_WIDTH // LANES):
        h0 = p * (LANES // HEAD_DIM)
        blocks.append(jnp.where(_head_in_pair_mask(rows, h0), per_head[h0], per_head[h0 + 1]))
    return jnp.concatenate(blocks, axis=1)


def _sb_kernel(q_ref, k_ref, v_ref, o_ref, carry_scr, acc_scr):
    i = pl.program_id(1)
    q = q_ref[pl.ds(pl.multiple_of(i * ATTN_TQ, ATTN_TQ), ATTN_TQ), :]
    qneg = [jnp.concatenate([_masked_head(q, HEADS_PER_PAIR * p + e) for e in range(HEADS_PER_PAIR)],
                            axis=0) * (-SCALE) for p in range(N_PAIRS)]
    r = lax.broadcasted_iota(jnp.int32, (ATTN_TK, ATTN_TK), 0)
    c = lax.broadcasted_iota(jnp.int32, (ATTN_TK, ATTN_TK), 1)
    suffix = jnp.where(r > c, 1.0, 0.0).astype(BF16)
    rows_all = N_HEADS * ATTN_TQ
    rq = lax.broadcasted_iota(jnp.int32, (rows_all, ATTN_TK), 0) & (ATTN_TQ - 1)
    cq = lax.broadcasted_iota(jnp.int32, (rows_all, ATTN_TK), 1)
    strict = cq < rq

    pair_rows = HEADS_PER_PAIR * ATTN_TQ

    def group(j_near, j_far, far_valid, diag_near):
        carry = carry_scr[...]
        staged = []
        for n, j in enumerate((j_near, j_far)):
            diag = diag_near and n == 0
            ks = pl.ds(pl.multiple_of(j * ATTN_TK, ATTN_TK), ATTN_TK)
            k = k_ref[ks, :]
            zn = jnp.concatenate(
                [_dot_nt(qneg[p], k[:, p * LANES:(p + 1) * LANES]) for p in range(N_PAIRS)], axis=0)
            zn2 = zn * LOG2_E
            l2 = jnp.minimum(zn2, 0.0) - jnp.log2(1.0 + jnp.exp2(-jnp.abs(zn2)))
            if diag:
                l2 = jnp.where(strict, l2, 0.0)
            staged.append((ks, diag, l2 - zn2, _dot(l2.astype(BF16), suffix),
                           jnp.sum(l2, axis=1, keepdims=True)))
        pv = None
        for n, (ks, diag, log2_beta, later_in, row_sum) in enumerate(staged):
            if n == 1:
                w = jnp.exp2(later_in + (carry + jnp.where(far_valid, 0.0, NEG_BIG)) + log2_beta)
                row_sum = jnp.where(far_valid, row_sum, 0.0)
            else:
                w = jnp.exp2(later_in + carry + log2_beta)
            if diag:
                w = jnp.where(strict, w, 0.0)
            wb = w.astype(BF16)
            v = v_ref[ks, :]
            contrib = jnp.concatenate(
                [_dot(wb[p * pair_rows:(p + 1) * pair_rows, :], v[:, p * LANES:(p + 1) * LANES])
                 for p in range(N_PAIRS)], axis=0)
            pv = contrib if pv is None else pv + contrib
            carry = carry + row_sum
        acc_scr[...] += pv
        carry_scr[...] = carry
        return jnp.max(carry) > LOG2_F32_UNDERFLOW

    carry_scr[...] = jnp.zeros_like(carry_scr)
    acc_scr[...] = jnp.zeros_like(acc_scr)
    alive = group(i, jnp.maximum(i - 1, 0), i >= 1, True)
    first_left = i - 2
    n_groups = i // 2

    def step(st):
        t, _ = st
        j_near = first_left - 2 * t
        return t + 1, group(j_near, jnp.maximum(j_near - 1, 0), j_near >= 1, False)

    lax.while_loop(lambda st: (st[0] < n_groups) & st[1], step, (jnp.int32(0), alive))
    acc = acc_scr[...]
    o_ref[...] = _merge_heads([acc[h * ATTN_TQ:(h + 1) * ATTN_TQ, :] for h in range(N_HEADS)]).astype(BF16)


def _sb_attention(main3):
    b, s, _ = main3.shape
    col = lambda k: pl.BlockSpec((None, s, MIX_WIDTH), lambda bi, i, k=k: (bi, 0, k))
    return pl.pallas_call(
        _sb_kernel,
        grid=(b, s // ATTN_TQ),
        in_specs=[col(5), col(6), col(7)],
        out_specs=pl.BlockSpec((None, ATTN_TQ, MIX_WIDTH), lambda bi, i: (bi, i, 0)),
        out_shape=jax.ShapeDtypeStruct((b, s, MIX_WIDTH), BF16),
        scratch_shapes=[pltpu.VMEM((N_HEADS * ATTN_TQ, 1), F32),
                        pltpu.VMEM((N_HEADS * ATTN_TQ, LANES), F32)],
        compiler_params=_params(2),
        name="sb",
    )(main3, main3, main3)


def _fox_kernel(q_ref, k_ref, v_ref, f_ref, bf_ref, qg_ref, kg_ref, o_ref,
                qn_scr, kn_scr, qaug_scr, kaug_scr, s_scr, mrun_scr, lrun_scr, acc_scr):
    i = pl.program_id(1)
    seq = q_ref.shape[0]

    @pl.when(i == 0)
    def _prepare():
        r = lax.broadcasted_iota(jnp.int32, (MIX_WIDTH, MIX_WIDTH), 0) // HEAD_DIM
        c = lax.broadcasted_iota(jnp.int32, (MIX_WIDTH, MIX_WIDTH), 1) // HEAD_DIM
        same_head = jnp.where(r == c, 1.0, 0.0).astype(BF16)
        tr = lax.broadcasted_iota(jnp.int32, (FOX_PREP_ROWS, FOX_PREP_ROWS), 0)
        tc = lax.broadcasted_iota(jnp.int32, (FOX_PREP_ROWS, FOX_PREP_ROWS), 1)
        prefix = jnp.where(tc <= tr, 1.0, 0.0).astype(BF16)
        lane = lax.broadcasted_iota(jnp.int32, (FOX_PREP_ROWS, LANES), 1)
        n = N_HEADS

        def tile(tix, run):
            rows = pl.ds(pl.multiple_of(tix * FOX_PREP_ROWS, FOX_PREP_ROWS), FOX_PREP_ROWS)
            for x_ref, g_ref, scale, dst in ((q_ref, qg_ref, SCALE, qn_scr), (k_ref, kg_ref, 1.0, kn_scr)):
                x = x_ref[rows, :].astype(F32)
                hi, lo = _split_hi_lo(x * x)
                ms = (_dot(hi, same_head) + _dot(lo, same_head)) * (1.0 / HEAD_DIM)
                dst[rows, :] = (x * lax.rsqrt(ms + EPS) * (g_ref[...] * scale)).astype(BF16)
            hi, lo = _split_hi_lo(jax.nn.log_sigmoid(f_ref[rows, :] + bf_ref[...]))
            cum = run + _dot(prefix, hi) + _dot(prefix, lo)
            c_hi, c_mid, c_lo = _split3(cum)
            q_aug = jnp.where(lane < n, c_hi, jnp.where(lane < 2 * n, c_mid, jnp.where(
                lane < 3 * n, c_lo, jnp.where(lane < 6 * n, 1.0, 0.0))))
            k_aug = jnp.where(lane < 3 * n, 1.0, jnp.where(lane < 4 * n, -c_hi, jnp.where(
                lane < 5 * n, -c_mid, jnp.where(lane < 6 * n, -c_lo, 0.0))))
            qaug_scr[rows, :] = q_aug.astype(BF16)
            kaug_scr[rows, :] = k_aug.astype(BF16)
            return cum[FOX_PREP_ROWS - 1:FOX_PREP_ROWS, :]

        lax.fori_loop(0, seq // FOX_PREP_ROWS, tile, jnp.zeros((1, LANES), F32))

    qrows = pl.ds(pl.multiple_of(i * ATTN_TQ, ATTN_TQ), ATTN_TQ)
    q = qn_scr[qrows, :]
    qaug = qaug_scr[qrows, :]
    lane = lax.broadcasted_iota(jnp.int32, (ATTN_TQ, LANES), 1)
    qs = []
    for p in range(N_PAIRS):
        rows = []
        for e in range(HEADS_PER_PAIR):
            h = HEADS_PER_PAIR * p + e
            aug_h = jnp.where((lane & (N_HEADS - 1)) == h, qaug, jnp.zeros_like(qaug))
            rows.append(jnp.concatenate([_masked_head(q, h), aug_h], axis=1))
        qs.append(jnp.concatenate(rows, axis=0))
    rows_all = N_HEADS * ATTN_TQ
    pair_rows = HEADS_PER_PAIR * ATTN_TQ
    r = lax.broadcasted_iota(jnp.int32, (rows_all, ATTN_TK), 0) & (ATTN_TQ - 1)
    c = lax.broadcasted_iota(jnp.int32, (rows_all, ATTN_TK), 1)
    causal = c <= r

    def key_rows(j):
        return pl.ds(pl.multiple_of(j * ATTN_TK, ATTN_TK), ATTN_TK)

    def scores(j):
        ks = key_rows(j)
        kaug = kaug_scr[ks, :]
        return jnp.concatenate(
            [_dot_nt(qs[p], jnp.concatenate([kn_scr[ks, p * LANES:(p + 1) * LANES], kaug], axis=1))
             for p in range(N_PAIRS)], axis=0) * LOG2_E

    def fold(x):
        return [x[:, n * LANES:(n + 1) * LANES] for n in range(ATTN_TK // LANES)]

    def emit():
        out = acc_scr[...] / jnp.sum(lrun_scr[...], axis=1, keepdims=True)
        o_ref[...] = _merge_heads([out[h * ATTN_TQ:(h + 1) * ATTN_TQ, :] for h in range(N_HEADS)]).astype(BF16)

    def accumulate(blocks):
        l_parts, pv = [], None
        for j, shifted in blocks:
            ks = key_rows(j)
            p = jnp.exp2(shifted)
            l_parts += fold(p)
            pb = p.astype(BF16)
            contrib = jnp.concatenate(
                [_dot(pb[pr * pair_rows:(pr + 1) * pair_rows, :], v_ref[ks, pr * LANES:(pr + 1) * LANES])
                 for pr in range(N_PAIRS)], axis=0)
            pv = contrib if pv is None else pv + contrib
        return functools.reduce(jnp.add, l_parts), pv

    s_diag = jnp.where(causal, scores(i), NEG_BIG)
    s_scr[i] = s_diag
    mrun_scr[...] = functools.reduce(jnp.maximum, fold(s_diag))

    def stage(t, _):
        parts = []
        for j in (2 * t, jnp.minimum(2 * t + 1, i - 1)):
            s = scores(j)
            s_scr[j] = s
            parts += fold(s)
        mrun_scr[...] = functools.reduce(jnp.maximum, parts, mrun_scr[...])
        return 0

    lax.fori_loop(0, (i + 1) // 2, stage, 0)
    m = jnp.max(mrun_scr[...], axis=1, keepdims=True)

    s_scr[i + 1] = jnp.full((rows_all, ATTN_TK), NEG_BIG, F32)
    lrun_scr[...] = jnp.zeros_like(lrun_scr)
    acc_scr[...] = jnp.zeros_like(acc_scr)
    last_block = seq // ATTN_TK - 1

    def weigh(t, _):
        blocks = [(jnp.minimum(j, last_block), s_scr[j] - m)
                  for j in (2 * t, 2 * t + 1)]
        l, pv = accumulate(blocks)
        lrun_scr[...] += l
        acc_scr[...] += pv
        return 0

    lax.fori_loop(0, (i + 2) // 2, weigh, 0)
    emit()


def _fox_attention(main3, f3, bf_row, qg_row, kg_row):
    b, s, _ = main3.shape
    rows_all = N_HEADS * ATTN_TQ
    col = lambda k: pl.BlockSpec((None, s, MIX_WIDTH), lambda bi, i, k=k: (bi, 0, k))
    return pl.pallas_call(
        _fox_kernel,
        grid=(b, s // ATTN_TQ),
        in_specs=[col(8), col(9), col(10),
                  pl.BlockSpec((None, s, LANES), lambda bi, i: (bi, 0, 0)),
                  _resident((1, LANES)),
                  _resident((1, MIX_WIDTH)),
                  _resident((1, MIX_WIDTH))],
        out_specs=pl.BlockSpec((None, ATTN_TQ, MIX_WIDTH), lambda bi, i: (bi, i, 0)),
        out_shape=jax.ShapeDtypeStruct((b, s, MIX_WIDTH), BF16),
        scratch_shapes=[pltpu.VMEM((s, MIX_WIDTH), BF16),
                        pltpu.VMEM((s, MIX_WIDTH), BF16),
                        pltpu.VMEM((s, LANES), BF16),
                        pltpu.VMEM((s, LANES), BF16),
                        pltpu.VMEM((s // ATTN_TK + 1, rows_all, ATTN_TK), F32),
                        pltpu.VMEM((rows_all, LANES), F32),
                        pltpu.VMEM((rows_all, LANES), F32),
                        pltpu.VMEM((rows_all, LANES), F32)],
        compiler_params=_params(2),
        name="fox",
    )(main3, main3, main3, f3, bf_row, qg_row, kg_row)


def _merge_kernel(h_ref, g_ref, yab_ref, yc_ref, yd_ref, wg_ref, wb_ref, wo_ref, o_ref, merged_scr):
    xn = _rms_norm_rows(h_ref[...], g_ref[...]).astype(BF16)
    ys = [yab_ref[:, 0:MIX_WIDTH], yab_ref[:, MIX_WIDTH:2 * MIX_WIDTH], yc_ref[...], yd_ref[...]]
    for c in range(D_MODEL // COL_CHUNK):
        cs = slice(c * COL_CHUNK, (c + 1) * COL_CHUNK)
        acc = None
        for n in range(N_BRANCH):
            gs = slice(n * D_MODEL + c * COL_CHUNK, n * D_MODEL + (c + 1) * COL_CHUNK)
            term = jax.nn.sigmoid(_dot(xn, wg_ref[:, gs])) * _dot(ys[n], wb_ref[n, :, cs])
            acc = term if acc is None else acc + term
        merged_scr[:, cs] = acc.astype(BF16)
    merged = merged_scr[...]
    for c in range(D_MODEL // COL_CHUNK):
        cs = slice(c * COL_CHUNK, (c + 1) * COL_CHUNK)
        o_ref[:, cs] = h_ref[:, cs] + _dot(merged, wo_ref[:, cs])


def _merge(h, g, yab, yc, yd, w_gate, w_branch, w_out, layer):
    t = h.shape[0]
    rows = lambda w: pl.BlockSpec((TOKEN_TILE, w), lambda i: (i, 0))
    return pl.pallas_call(
        _merge_kernel,
        grid=(t // TOKEN_TILE,),
        in_specs=[rows(D_MODEL), _resident((1, D_MODEL)),
                  rows(2 * MIX_WIDTH), rows(MIX_WIDTH), rows(MIX_WIDTH),
                  _resident_layer((D_MODEL, N_BRANCH * D_MODEL), layer),
                  _resident_layer((N_BRANCH, MIX_WIDTH, D_MODEL), layer),
                  _resident_layer((D_MODEL, D_MODEL), layer)],
        out_specs=rows(D_MODEL),
        out_shape=jax.ShapeDtypeStruct((t, D_MODEL), F32),
        scratch_shapes=[pltpu.VMEM((TOKEN_TILE, D_MODEL), BF16)],
        compiler_params=_params(1),
        name="merge",
    )(h, g, yab, yc, yd, w_gate, w_branch, w_out)


def _ffn_kernel(h_ref, g_ref, wi_ref, wo_ref, o_ref, acc_scr):
    xn = _rms_norm_rows(h_ref[...], g_ref[...]).astype(BF16)
    for c in range(FFN_HIDDEN // COL_CHUNK):
        gate = _dot(xn, wi_ref[:, c * COL_CHUNK:(c + 1) * COL_CHUNK])
        up = _dot(xn, wi_ref[:, FFN_HIDDEN + c * COL_CHUNK:FFN_HIDDEN + (c + 1) * COL_CHUNK])
        act = (jax.nn.silu(gate) * up).astype(BF16)
        part = _dot(act, wo_ref[c * COL_CHUNK:(c + 1) * COL_CHUNK, :])
        if c == 0:
            acc_scr[...] = part
        else:
            acc_scr[...] += part
    o_ref[...] = h_ref[...] + acc_scr[...]


def _ffn(h, g, w_in, w_out, layer):
    t = h.shape[0]
    rows = pl.BlockSpec((TOKEN_TILE, D_MODEL), lambda i: (i, 0))
    return pl.pallas_call(
        _ffn_kernel,
        grid=(t // TOKEN_TILE,),
        in_specs=[rows, _resident((1, D_MODEL)),
                  _resident_layer((D_MODEL, 2 * FFN_HIDDEN), layer),
                  _resident_layer((FFN_HIDDEN, D_MODEL), layer)],
        out_specs=rows,
        out_shape=jax.ShapeDtypeStruct((t, D_MODEL), F32),
        scratch_shapes=[pltpu.VMEM((TOKEN_TILE, D_MODEL), F32)],
        compiler_params=_params(1),
        name="ffn",
    )(h, g, w_in, w_out)


def kernel(x, norm_mix_g, w_in, w_conv, w_spatial, b_spatial, gmlp_ln_g, gmlp_ln_b,
           fox_q_norm_g, fox_k_norm_g, fox_forget_b, w_branch, w_out, norm_ffn_g,
           w_ffn_in, w_ffn_out):
    b, s, d = x.shape
    depth = w_in.shape[0]
    assert d == D_MODEL and s % max(ATTN_TQ, ATTN_TK, GMLP_CHUNK, TOKEN_TILE) == 0
    t = b * s
    h = x.reshape(t, d)
    w_in_b = w_in.astype(BF16)
    w_gate_b = w_in_b[:, :, N_MAIN + N_HEADS:]
    w_branch_b = w_branch.astype(BF16)
    w_out_b = w_out.astype(BF16)
    w_ffn_in_b = w_ffn_in.astype(BF16)
    w_ffn_out_b = w_ffn_out.astype(BF16)
    n_f = FORGET_COPIES * N_HEADS
    for l in range(depth):
        w_f = jnp.pad(jnp.tile(w_in_b[l, :, N_MAIN:N_MAIN + N_HEADS], (1, FORGET_COPIES)),
                      ((0, 0), (0, LANES - n_f)))
        bf_row = jnp.pad(jnp.tile(fox_forget_b[l], FORGET_COPIES), (0, LANES - n_f)).reshape(1, LANES)
        qg_row = jnp.tile(fox_q_norm_g[l], N_HEADS).reshape(1, MIX_WIDTH)
        kg_row = jnp.tile(fox_k_norm_g[l], N_HEADS).reshape(1, MIX_WIDTH)
        bs_rows = jnp.repeat(b_spatial[l].T, GROUP_WIDTH, axis=1)

        main, f_raw = _proj(h, norm_mix_g[l].reshape(1, d), w_in_b, l, w_f)
        main3 = main.reshape(b, s, N_MAIN)
        yab = _convgmlp(main3, w_conv[l], w_spatial[l], bs_rows,
                        gmlp_ln_g[l].reshape(1, MIX_WIDTH), gmlp_ln_b[l].reshape(1, MIX_WIDTH))
        yc = _sb_attention(main3)
        yd = _fox_attention(main3, f_raw.reshape(b, s, LANES), bf_row, qg_row, kg_row)
        h = _merge(h, norm_mix_g[l].reshape(1, d), yab.reshape(t, 2 * MIX_WIDTH),
                   yc.reshape(t, MIX_WIDTH), yd.reshape(t, MIX_WIDTH),
                   w_gate_b, w_branch_b, w_out_b, l)
        h = _ffn(h, norm_ffn_g[l].reshape(1, d), w_ffn_in_b, w_ffn_out_b, l)
    return h.reshape(b, s, d)
```

```python
import functools

import jax
import jax.numpy as jnp
from jax import lax
from jax.experimental import pallas as pl
from jax.experimental.pallas import tpu as pltpu

D_MODEL = 1024
MIX_WIDTH = 256
HEAD_DIM = 64
N_HEADS = MIX_WIDTH // HEAD_DIM
N_BRANCH = 4
CONV_K = 3
GMLP_GROUPS = 4
GMLP_CHUNK = 128
GROUP_WIDTH = MIX_WIDTH // GMLP_GROUPS
FFN_HIDDEN = 2816
EPS = 1e-6
N_MAIN = 11 * MIX_WIDTH
FOX_Q_CHUNK = 8
FOX_K_CHUNK = 9
LANES = 128
HEADS_PER_PAIR = LANES // HEAD_DIM
N_PAIRS = MIX_WIDTH // LANES
FORGET_COPIES = 6
FOX_PREP_ROWS = 256
SUBLANES = 8
VMEM_LIMIT_BYTES = 56 * 1024 * 1024

TOKEN_TILE = 1024
COL_CHUNK = 256
ATTN_TQ = 256
ATTN_TK = 256
SCALE = HEAD_DIM ** -0.5
LOG2_E = 1.4426950408889634
LOG2_F32_UNDERFLOW = -151.0
NEG_BIG = -1e30

F32 = jnp.float32
BF16 = jnp.bfloat16


def _dot(a, b):
    return jnp.dot(a, b, preferred_element_type=F32)


def _dot_nt(a, b):
    return lax.dot_general(a, b, (((1,), (1,)), ((), ())), preferred_element_type=F32)


def _split_hi_lo(x):
    hi = x.astype(BF16)
    lo = (x - hi.astype(F32)).astype(BF16)
    return hi, lo


def _rms_norm_rows(x, g):
    ms = jnp.mean(x * x, axis=-1, keepdims=True)
    return x * lax.rsqrt(ms + EPS) * g


def _resident(shape):
    return pl.BlockSpec(shape, lambda *_: (0,) * len(shape), pipeline_mode=pl.Buffered(1))


def _resident_layer(shape, layer):
    return pl.BlockSpec((None,) + tuple(shape), lambda *_: (layer,) + (0,) * len(shape),
                        pipeline_mode=pl.Buffered(1))


def _params(n_axes):
    return pltpu.CompilerParams(dimension_semantics=("arbitrary",) * n_axes,
                                vmem_limit_bytes=VMEM_LIMIT_BYTES)


def _split3(x):
    hi = x.astype(BF16).astype(F32)
    rem = x - hi
    mid = rem.astype(BF16).astype(F32)
    return hi, mid, rem - mid


def _proj_kernel(h_ref, g_ref, wm_ref, wf_ref, main_ref, f_ref):
    xn = _rms_norm_rows(h_ref[...], g_ref[...]).astype(BF16)
    for c in range(N_MAIN // COL_CHUNK):
        cs = slice(c * COL_CHUNK, (c + 1) * COL_CHUNK)
        main_ref[:, cs] = _dot(xn, wm_ref[:, cs]).astype(BF16)
    f_ref[...] = _dot(xn, wf_ref[...])


def _proj(h, g, w_main_all, layer, w_f):
    t = h.shape[0]
    return pl.pallas_call(
        _proj_kernel,
        grid=(t // TOKEN_TILE,),
        in_specs=[pl.BlockSpec((TOKEN_TILE, D_MODEL), lambda i: (i, 0)),
                  _resident((1, D_MODEL)),
                  _resident_layer((D_MODEL, N_MAIN), layer),
                  _resident((D_MODEL, LANES))],
        out_specs=[pl.BlockSpec((TOKEN_TILE, N_MAIN), lambda i: (i, 0)),
                   pl.BlockSpec((TOKEN_TILE, LANES), lambda i: (i, 0))],
        out_shape=[jax.ShapeDtypeStruct((t, N_MAIN), BF16),
                   jax.ShapeDtypeStruct((t, LANES), F32)],
        compiler_params=_params(1),
        name="proj",
    )(h, g, w_main_all, w_f)


def _convgmlp_kernel(cb_ref, cc_ref, cx_ref, u_ref, v_ref, wconv_ref, ws_ref, bs_ref,
                     lng_ref, lnb_ref, o_ref):
    seq = cb_ref.shape[0]
    n_chunks = seq // GMLP_CHUNK
    row = lax.broadcasted_iota(jnp.int32, (GMLP_CHUNK, GMLP_CHUNK), 0)
    col = lax.broadcasted_iota(jnp.int32, (GMLP_CHUNK, GMLP_CHUNK), 1)
    w_tril = [jnp.where(col <= row, ws_ref[gi], 0.0).astype(BF16) for gi in range(GMLP_GROUPS)]
    lane = lax.broadcasted_iota(jnp.int32, (GMLP_CHUNK, LANES), 1)
    first_group = lane < GROUP_WIDTH
    w0 = wconv_ref[0:1, :]
    w1 = wconv_ref[1:2, :]
    w2 = wconv_ref[2:3, :]

    def chunk(c, prev_tail):
        rows = pl.ds(pl.multiple_of(c * GMLP_CHUNK, GMLP_CHUNK), GMLP_CHUNK)
        xc = cc_ref[rows, :].astype(F32) * cx_ref[rows, :].astype(F32)
        win = jnp.concatenate([prev_tail, xc], axis=0)
        xc1 = pltpu.roll(win, 1, 0)[SUBLANES:, :]
        xc2 = pltpu.roll(win, 2, 0)[SUBLANES:, :]
        ya = cb_ref[rows, :].astype(F32) * (w0 * xc2 + w1 * xc1 + w2 * xc)
        o_ref[rows, 0:MIX_WIDTH] = ya.astype(BF16)

        gu = jax.nn.gelu(u_ref[rows, :].astype(F32))
        gv = jax.nn.gelu(v_ref[rows, :].astype(F32))
        mu = jnp.mean(gv, axis=-1, keepdims=True)
        cen = gv - mu
        var = jnp.mean(cen * cen, axis=-1, keepdims=True)
        vn = (cen * lax.rsqrt(var + EPS) * lng_ref[...] + lnb_ref[...]).astype(BF16)
        halves = []
        for lb in range(MIX_WIDTH // LANES):
            vb = vn[:, lb * LANES:(lb + 1) * LANES]
            m0 = _dot(w_tril[2 * lb], vb)
            m1 = _dot(w_tril[2 * lb + 1], vb)
            halves.append(jnp.where(first_group, m0, m1))
        mixed = jnp.concatenate(halves, axis=1) + bs_ref[...]
        o_ref[rows, MIX_WIDTH:2 * MIX_WIDTH] = (gu * mixed).astype(BF16)
        return xc[GMLP_CHUNK - SUBLANES:, :]

    lax.fori_loop(0, n_chunks // 2, lambda c2, tail: chunk(2 * c2 + 1, chunk(2 * c2, tail)),
                  jnp.zeros((SUBLANES, MIX_WIDTH), F32))


def _convgmlp(main3, w_conv, w_s, bs_rows, ln_g, ln_b):
    b, s, _ = main3.shape
    col = lambda k: pl.BlockSpec((None, s, MIX_WIDTH), lambda i, k=k: (i, 0, k))
    return pl.pallas_call(
        _convgmlp_kernel,
        grid=(b,),
        in_specs=[col(0), col(1), col(2), col(3), col(4),
                  _resident((CONV_K, MIX_WIDTH)),
                  _resident((GMLP_GROUPS, GMLP_CHUNK, GMLP_CHUNK)),
                  _resident((GMLP_CHUNK, MIX_WIDTH)),
                  _resident((1, MIX_WIDTH)),
                  _resident((1, MIX_WIDTH))],
        out_specs=pl.BlockSpec((None, s, 2 * MIX_WIDTH), lambda i: (i, 0, 0)),
        out_shape=jax.ShapeDtypeStruct((b, s, 2 * MIX_WIDTH), BF16),
        compiler_params=_params(1),
        name="convgmlp",
    )(main3, main3, main3, main3, main3, w_conv, w_s, bs_rows, ln_g, ln_b)


def _pair(x, h):
    p = (h * HEAD_DIM) // LANES
    return x[:, p * LANES:(p + 1) * LANES]


def _head_in_pair_mask(rows, h):
    lane = lax.broadcasted_iota(jnp.int32, (rows, LANES), 1)
    first = lane < HEAD_DIM
    return first if (h * HEAD_DIM) % LANES == 0 else jnp.logical_not(first)


def _masked_head(x, h):
    xp = _pair(x, h)
    return jnp.where(_head_in_pair_mask(x.shape[0], h), xp, jnp.zeros_like(xp))


def _merge_heads(per_head):
    rows = per_head[0].shape[0]
    blocks = []
    for p in range(MIX_WIDTH // LANES):
        h0 = p * (LANES // HEAD_DIM)
        blocks.append(jnp.where(_head_in_pair_mask(rows, h0), per_head[h0], per_head[h0 + 1]))
    return jnp.concatenate(blocks, axis=1)


def _sb_kernel(q_ref, k_ref, v_ref, o_ref, carry_scr, acc_scr):
    i = pl.program_id(1)
    q = q_ref[pl.ds(pl.multiple_of(i * ATTN_TQ, ATTN_TQ), ATTN_TQ), :]
    qneg = [jnp.concatenate([_masked_head(q, HEADS_PER_PAIR * p + e) for e in range(HEADS_PER_PAIR)],
                            axis=0) * (-SCALE) for p in range(N_PAIRS)]
    r = lax.broadcasted_iota(jnp.int32, (ATTN_TK, ATTN_TK), 0)
    c = lax.broadcasted_iota(jnp.int32, (ATTN_TK, ATTN_TK), 1)
    suffix = jnp.where(r > c, 1.0, 0.0).astype(BF16)
    rows_all = N_HEADS * ATTN_TQ
    rq = lax.broadcasted_iota(jnp.int32, (rows_all, ATTN_TK), 0) & (ATTN_TQ - 1)
    cq = lax.broadcasted_iota(jnp.int32, (rows_all, ATTN_TK), 1)
    strict = cq < rq

    pair_rows = HEADS_PER_PAIR * ATTN_TQ

    def group(j_near, j_far, far_valid, diag_near):
        carry = jnp.zeros((rows_all, 1), F32) if diag_near else carry_scr[...]
        staged = []
        for n, j in enumerate((j_near, j_far)):
            diag = diag_near and n == 0
            ks = pl.ds(pl.multiple_of(j * ATTN_TK, ATTN_TK), ATTN_TK)
            k = k_ref[ks, :]
            zn = jnp.concatenate(
                [_dot_nt(qneg[p], k[:, p * LANES:(p + 1) * LANES]) for p in range(N_PAIRS)], axis=0)
            zn2 = zn * LOG2_E
            l2 = jnp.minimum(zn2, 0.0) - jnp.log2(1.0 + jnp.exp2(-jnp.abs(zn2)))
            if diag:
                l2 = jnp.where(strict, l2, 0.0)
            staged.append((ks, diag, l2 - zn2, _dot(l2.astype(BF16), suffix),
                           jnp.sum(l2, axis=1, keepdims=True)))
        pv = None
        for n, (ks, diag, log2_beta, later_in, row_sum) in enumerate(staged):
            if n == 1:
                w = jnp.exp2(later_in + (carry + jnp.where(far_valid, 0.0, NEG_BIG)) + log2_beta)
                row_sum = jnp.where(far_valid, row_sum, 0.0)
            else:
                w = jnp.exp2(later_in + carry + log2_beta)
            if diag:
                w = jnp.where(strict, w, 0.0)
            wb = w.astype(BF16)
            v = v_ref[ks, :]
            contrib = jnp.concatenate(
                [_dot(wb[p * pair_rows:(p + 1) * pair_rows, :], v[:, p * LANES:(p + 1) * LANES])
                 for p in range(N_PAIRS)], axis=0)
            pv = contrib if pv is None else pv + contrib
            carry = carry + row_sum
        if diag_near:
            acc_scr[...] = pv
        else:
            acc_scr[...] += pv
        carry_scr[...] = carry
        return jnp.max(carry) > LOG2_F32_UNDERFLOW

    alive = group(i, jnp.maximum(i - 1, 0), i >= 1, True)
    first_left = i - 2
    n_groups = i // 2

    def step(st):
        t, _ = st
        j_near = first_left - 2 * t
        return t + 1, group(j_near, jnp.maximum(j_near - 1, 0), j_near >= 1, False)

    lax.while_loop(lambda st: (st[0] < n_groups) & st[1], step, (jnp.int32(0), alive))
    acc = acc_scr[...]
    o_ref[...] = _merge_heads([acc[h * ATTN_TQ:(h + 1) * ATTN_TQ, :] for h in range(N_HEADS)]).astype(BF16)


def _sb_attention(main3):
    b, s, _ = main3.shape
    col = lambda k: pl.BlockSpec((None, s, MIX_WIDTH), lambda bi, i, k=k: (bi, 0, k))
    return pl.pallas_call(
        _sb_kernel,
        grid=(b, s // ATTN_TQ),
        in_specs=[col(5), col(6), col(7)],
        out_specs=pl.BlockSpec((None, ATTN_TQ, MIX_WIDTH), lambda bi, i: (bi, i, 0)),
        out_shape=jax.ShapeDtypeStruct((b, s, MIX_WIDTH), BF16),
        scratch_shapes=[pltpu.VMEM((N_HEADS * ATTN_TQ, 1), F32),
                        pltpu.VMEM((N_HEADS * ATTN_TQ, LANES), F32)],
        compiler_params=_params(2),
        name="sb",
    )(main3, main3, main3)


def _fox_kernel(q_ref, k_ref, v_ref, f_ref, bf_ref, qg_ref, kg_ref, o_ref,
                qn_scr, kn_scr, qaug_scr, kaug_scr, s_scr, mrun_scr, lrun_scr, acc_scr):
    i = pl.program_id(1)
    seq = q_ref.shape[0]

    @pl.when(i == 0)
    def _prepare():
        s_scr[seq // ATTN_TK] = jnp.full((N_HEADS * ATTN_TQ, ATTN_TK), NEG_BIG, F32)
        r = lax.broadcasted_iota(jnp.int32, (MIX_WIDTH, MIX_WIDTH), 0) // HEAD_DIM
        c = lax.broadcasted_iota(jnp.int32, (MIX_WIDTH, MIX_WIDTH), 1) // HEAD_DIM
        same_head = jnp.where(r == c, 1.0, 0.0).astype(BF16)
        tr = lax.broadcasted_iota(jnp.int32, (FOX_PREP_ROWS, FOX_PREP_ROWS), 0)
        tc = lax.broadcasted_iota(jnp.int32, (FOX_PREP_ROWS, FOX_PREP_ROWS), 1)
        prefix = jnp.where(tc <= tr, 1.0, 0.0).astype(BF16)
        lane = lax.broadcasted_iota(jnp.int32, (FOX_PREP_ROWS, LANES), 1)
        n = N_HEADS

        def tile(tix, run):
            rows = pl.ds(pl.multiple_of(tix * FOX_PREP_ROWS, FOX_PREP_ROWS), FOX_PREP_ROWS)
            for x_ref, g_ref, scale, dst in ((q_ref, qg_ref, SCALE, qn_scr), (k_ref, kg_ref, 1.0, kn_scr)):
                x = x_ref[rows, :].astype(F32)
                hi, lo = _split_hi_lo(x * x)
                ms = (_dot(hi, same_head) + _dot(lo, same_head)) * (1.0 / HEAD_DIM)
                dst[rows, :] = (x * lax.rsqrt(ms + EPS) * (g_ref[...] * scale)).astype(BF16)
            hi, lo = _split_hi_lo(jax.nn.log_sigmoid(f_ref[rows, :] + bf_ref[...]))
            cum = run + _dot(prefix, hi) + _dot(prefix, lo)
            c_hi, c_mid, c_lo = _split3(cum)
            q_aug = jnp.where(lane < n, c_hi, jnp.where(lane < 2 * n, c_mid, jnp.where(
                lane < 3 * n, c_lo, jnp.where(lane < 6 * n, 1.0, 0.0))))
            k_aug = jnp.where(lane < 3 * n, 1.0, jnp.where(lane < 4 * n, -c_hi, jnp.where(
                lane < 5 * n, -c_mid, jnp.where(lane < 6 * n, -c_lo, 0.0))))
            qaug_scr[rows, :] = q_aug.astype(BF16)
            kaug_scr[rows, :] = k_aug.astype(BF16)
            return cum[FOX_PREP_ROWS - 1:FOX_PREP_ROWS, :]

        lax.fori_loop(0, seq // FOX_PREP_ROWS, tile, jnp.zeros((1, LANES), F32))

    qrows = pl.ds(pl.multiple_of(i * ATTN_TQ, ATTN_TQ), ATTN_TQ)
    q = qn_scr[qrows, :]
    qaug = qaug_scr[qrows, :]
    lane = lax.broadcasted_iota(jnp.int32, (ATTN_TQ, LANES), 1)
    qs = []
    for p in range(N_PAIRS):
        rows = []
        for e in range(HEADS_PER_PAIR):
            h = HEADS_PER_PAIR * p + e
            aug_h = jnp.where((lane & (N_HEADS - 1)) == h, qaug, jnp.zeros_like(qaug))
            rows.append(jnp.concatenate([_masked_head(q, h), aug_h], axis=1))
        qs.append(jnp.concatenate(rows, axis=0))
    rows_all = N_HEADS * ATTN_TQ
    pair_rows = HEADS_PER_PAIR * ATTN_TQ
    r = lax.broadcasted_iota(jnp.int32, (rows_all, ATTN_TK), 0) & (ATTN_TQ - 1)
    c = lax.broadcasted_iota(jnp.int32, (rows_all, ATTN_TK), 1)
    causal = c <= r

    def key_rows(j):
        return pl.ds(pl.multiple_of(j * ATTN_TK, ATTN_TK), ATTN_TK)

    def scores(j):
        ks = key_rows(j)
        kaug = kaug_scr[ks, :]
        return jnp.concatenate(
            [_dot_nt(qs[p], jnp.concatenate([kn_scr[ks, p * LANES:(p + 1) * LANES], kaug], axis=1))
             for p in range(N_PAIRS)], axis=0) * LOG2_E

    def fold(x):
        return [x[:, n * LANES:(n + 1) * LANES] for n in range(ATTN_TK // LANES)]

    def emit():
        out = acc_scr[...] / jnp.sum(lrun_scr[...], axis=1, keepdims=True)
        o_ref[...] = _merge_heads([out[h * ATTN_TQ:(h + 1) * ATTN_TQ, :] for h in range(N_HEADS)]).astype(BF16)

    def accumulate(blocks):
        l_parts, pv = [], None
        for j, shifted in blocks:
            ks = key_rows(j)
            p = jnp.exp2(shifted)
            l_parts += fold(p)
            pb = p.astype(BF16)
            contrib = jnp.concatenate(
                [_dot(pb[pr * pair_rows:(pr + 1) * pair_rows, :], v_ref[ks, pr * LANES:(pr + 1) * LANES])
                 for pr in range(N_PAIRS)], axis=0)
            pv = contrib if pv is None else pv + contrib
        return functools.reduce(jnp.add, l_parts), pv

    s_diag = jnp.where(causal, scores(i), NEG_BIG)
    s_scr[i] = s_diag
    mrun_scr[...] = functools.reduce(jnp.maximum, fold(s_diag))

    def stage(t, _):
        parts = []
        for j in (2 * t, jnp.minimum(2 * t + 1, i - 1)):
            s = scores(j)
            s_scr[j] = s
            parts += fold(s)
        mrun_scr[...] = functools.reduce(jnp.maximum, parts, mrun_scr[...])
        return 0

    lax.fori_loop(0, (i + 1) // 2, stage, 0)
    m = jnp.max(mrun_scr[...], axis=1, keepdims=True)

    n_blocks = seq // ATTN_TK
    l, pv = accumulate([(i, s_scr[i] - m)])
    lrun_scr[...] = l
    acc_scr[...] = pv

    def weigh(t, _):
        j1 = 2 * t + 1
        masked = j1 > i - 1
        blocks = [(2 * t, s_scr[2 * t] - m),
                  (jnp.where(masked, 0, j1), s_scr[jnp.where(masked, n_blocks, j1)] - m)]
        l, pv = accumulate(blocks)
        lrun_scr[...] += l
        acc_scr[...] += pv
        return 0

    lax.fori_loop(0, (i + 1) // 2, weigh, 0)
    emit()


def _fox_attention(main3, f3, bf_row, qg_row, kg_row):
    b, s, _ = main3.shape
    rows_all = N_HEADS * ATTN_TQ
    col = lambda k: pl.BlockSpec((None, s, MIX_WIDTH), lambda bi, i, k=k: (bi, 0, k))
    return pl.pallas_call(
        _fox_kernel,
        grid=(b, s // ATTN_TQ),
        in_specs=[col(8), col(9), col(10),
                  pl.BlockSpec((None, s, LANES), lambda bi, i: (bi, 0, 0)),
                  _resident((1, LANES)),
                  _resident((1, MIX_WIDTH)),
                  _resident((1, MIX_WIDTH))],
        out_specs=pl.BlockSpec((None, ATTN_TQ, MIX_WIDTH), lambda bi, i: (bi, i, 0)),
        out_shape=jax.ShapeDtypeStruct((b, s, MIX_WIDTH), BF16),
        scratch_shapes=[pltpu.VMEM((s, MIX_WIDTH), BF16),
                        pltpu.VMEM((s, MIX_WIDTH), BF16),
                        pltpu.VMEM((s, LANES), BF16),
                        pltpu.VMEM((s, LANES), BF16),
                        pltpu.VMEM((s // ATTN_TK + 1, rows_all, ATTN_TK), F32),
                        pltpu.VMEM((rows_all, LANES), F32),
                        pltpu.VMEM((rows_all, LANES), F32),
                        pltpu.VMEM((rows_all, LANES), F32)],
        compiler_params=_params(2),
        name="fox",
    )(main3, main3, main3, f3, bf_row, qg_row, kg_row)


def _merge_kernel(h_ref, g_ref, yab_ref, yc_ref, yd_ref, wg_ref, wb_ref, wo_ref, o_ref, merged_scr):
    xn = _rms_norm_rows(h_ref[...], g_ref[...]).astype(BF16)
    ys = [yab_ref[:, 0:MIX_WIDTH], yab_ref[:, MIX_WIDTH:2 * MIX_WIDTH], yc_ref[...], yd_ref[...]]
    for c in range(D_MODEL // COL_CHUNK):
        cs = slice(c * COL_CHUNK, (c + 1) * COL_CHUNK)
        acc = None
        for n in range(N_BRANCH):
            gs = slice(n * D_MODEL + c * COL_CHUNK, n * D_MODEL + (c + 1) * COL_CHUNK)
            term = jax.nn.sigmoid(_dot(xn, wg_ref[:, gs])) * _dot(ys[n], wb_ref[n, :, cs])
            acc = term if acc is None else acc + term
        merged_scr[:, cs] = acc.astype(BF16)
    merged = merged_scr[...]
    for c in range(D_MODEL // COL_CHUNK):
        cs = slice(c * COL_CHUNK, (c + 1) * COL_CHUNK)
        o_ref[:, cs] = h_ref[:, cs] + _dot(merged, wo_ref[:, cs])


def _merge(h, g, yab, yc, yd, w_gate, w_branch, w_out, layer):
    t = h.shape[0]
    rows = lambda w: pl.BlockSpec((TOKEN_TILE, w), lambda i: (i, 0))
    return pl.pallas_call(
        _merge_kernel,
        grid=(t // TOKEN_TILE,),
        in_specs=[rows(D_MODEL), _resident((1, D_MODEL)),
                  rows(2 * MIX_WIDTH), rows(MIX_WIDTH), rows(MIX_WIDTH),
                  _resident_layer((D_MODEL, N_BRANCH * D_MODEL), layer),
                  _resident_layer((N_BRANCH, MIX_WIDTH, D_MODEL), layer),
                  _resident_layer((D_MODEL, D_MODEL), layer)],
        out_specs=rows(D_MODEL),
        out_shape=jax.ShapeDtypeStruct((t, D_MODEL), F32),
        scratch_shapes=[pltpu.VMEM((TOKEN_TILE, D_MODEL), BF16)],
        compiler_params=_params(1),
        name="merge",
    )(h, g, yab, yc, yd, w_gate, w_branch, w_out)


def _ffn_kernel(h_ref, g_ref, wi_ref, wo_ref, o_ref, acc_scr):
    xn = _rms_norm_rows(h_ref[...], g_ref[...]).astype(BF16)
    for c in range(FFN_HIDDEN // COL_CHUNK):
        gate = _dot(xn, wi_ref[:, c * COL_CHUNK:(c + 1) * COL_CHUNK])
        up = _dot(xn, wi_ref[:, FFN_HIDDEN + c * COL_CHUNK:FFN_HIDDEN + (c + 1) * COL_CHUNK])
        act = (jax.nn.silu(gate) * up).astype(BF16)
        part = _dot(act, wo_ref[c * COL_CHUNK:(c + 1) * COL_CHUNK, :])
        if c == 0:
            acc_scr[...] = part
        else:
            acc_scr[...] += part
    o_ref[...] = h_ref[...] + acc_scr[...]


def _ffn(h, g, w_in, w_out, layer):
    t = h.shape[0]
    rows = pl.BlockSpec((TOKEN_TILE, D_MODEL), lambda i: (i, 0))
    return pl.pallas_call(
        _ffn_kernel,
        grid=(t // TOKEN_TILE,),
        in_specs=[rows, _resident((1, D_MODEL)),
                  _resident_layer((D_MODEL, 2 * FFN_HIDDEN), layer),
                  _resident_layer((FFN_HIDDEN, D_MODEL), layer)],
        out_specs=rows,
        out_shape=jax.ShapeDtypeStruct((t, D_MODEL), F32),
        scratch_shapes=[pltpu.VMEM((TOKEN_TILE, D_MODEL), F32)],
        compiler_params=_params(1),
        name="ffn",
    )(h, g, w_in, w_out)


def kernel(x, norm_mix_g, w_in, w_conv, w_spatial, b_spatial, gmlp_ln_g, gmlp_ln_b,
           fox_q_norm_g, fox_k_norm_g, fox_forget_b, w_branch, w_out, norm_ffn_g,
           w_ffn_in, w_ffn_out):
    b, s, d = x.shape
    depth = w_in.shape[0]
    assert d == D_MODEL and s % max(ATTN_TQ, ATTN_TK, GMLP_CHUNK, TOKEN_TILE) == 0
    t = b * s
    h = x.reshape(t, d)
    w_main_b = w_in[:, :, :N_MAIN].astype(BF16)
    w_forget_b = w_in[:, :, N_MAIN:N_MAIN + N_HEADS].astype(BF16)
    w_gate_b = w_in[:, :, N_MAIN + N_HEADS:].astype(BF16)
    w_branch_b = w_branch.astype(BF16)
    w_out_b = w_out.astype(BF16)
    w_ffn_in_b = w_ffn_in.astype(BF16)
    w_ffn_out_b = w_ffn_out.astype(BF16)
    n_f = FORGET_COPIES * N_HEADS
    for l in range(depth):
        w_f = jnp.pad(jnp.tile(w_forget_b[l], (1, FORGET_COPIES)), ((0, 0), (0, LANES - n_f)))
        bf_row = jnp.pad(jnp.tile(fox_forget_b[l], FORGET_COPIES), (0, LANES - n_f)).reshape(1, LANES)
        qg_row = jnp.tile(fox_q_norm_g[l], N_HEADS).reshape(1, MIX_WIDTH)
        kg_row = jnp.tile(fox_k_norm_g[l], N_HEADS).reshape(1, MIX_WIDTH)
        bs_rows = jnp.repeat(b_spatial[l].T, GROUP_WIDTH, axis=1)

        main, f_raw = _proj(h, norm_mix_g[l].reshape(1, d), w_main_b, l, w_f)
        main3 = main.reshape(b, s, N_MAIN)
        yab = _convgmlp(main3, w_conv[l], w_spatial[l], bs_rows,
                        gmlp_ln_g[l].reshape(1, MIX_WIDTH), gmlp_ln_b[l].reshape(1, MIX_WIDTH))
        yc = _sb_attention(main3)
        yd = _fox_attention(main3, f_raw.reshape(b, s, LANES), bf_row, qg_row, kg_row)
        h = _merge(h, norm_mix_g[l].reshape(1, d), yab.reshape(t, 2 * MIX_WIDTH),
                   yc.reshape(t, MIX_WIDTH), yd.reshape(t, MIX_WIDTH),
                   w_gate_b, w_branch_b, w_out_b, l)
        h = _ffn(h, norm_ffn_g[l].reshape(1, d), w_ffn_in_b, w_ffn_out_b, l)
    return h.reshape(b, s, d)
```

```python
import functools

import jax
import jax.numpy as jnp
from jax import lax
from jax.experimental import pallas as pl
from jax.experimental.pallas import tpu as pltpu

D_MODEL = 1024
MIX_WIDTH = 256
HEAD_DIM = 64
N_HEADS = MIX_WIDTH // HEAD_DIM
N_BRANCH = 4
CONV_K = 3
GMLP_GROUPS = 4
GMLP_CHUNK = 128
GROUP_WIDTH = MIX_WIDTH // GMLP_GROUPS
FFN_HIDDEN = 2816
EPS = 1e-6
N_MAIN = 11 * MIX_WIDTH
FOX_Q_CHUNK = 8
FOX_K_CHUNK = 9
LANES = 128
HEADS_PER_PAIR = LANES // HEAD_DIM
N_PAIRS = MIX_WIDTH // LANES
FORGET_COPIES = 6
FOX_PREP_ROWS = 256
SUBLANES = 8
VMEM_LIMIT_BYTES = 56 * 1024 * 1024

TOKEN_TILE = 1024
COL_CHUNK = 256
ATTN_TQ = 256
ATTN_TK = 256
SB_QBLOCKS = 4
FOX_QBLOCKS = 2
SCALE = HEAD_DIM ** -0.5
LOG2_E = 1.4426950408889634
LOG2_F32_UNDERFLOW = -151.0
NEG_BIG = -1e30

F32 = jnp.float32
BF16 = jnp.bfloat16


def _dot(a, b):
    return jnp.dot(a, b, preferred_element_type=F32)


def _dot_nt(a, b):
    return lax.dot_general(a, b, (((1,), (1,)), ((), ())), preferred_element_type=F32)


def _split_hi_lo(x):
    hi = x.astype(BF16)
    lo = (x - hi.astype(F32)).astype(BF16)
    return hi, lo


def _rms_norm_rows(x, g):
    ms = jnp.mean(x * x, axis=-1, keepdims=True)
    return x * lax.rsqrt(ms + EPS) * g


def _resident(shape):
    return pl.BlockSpec(shape, lambda *_: (0,) * len(shape), pipeline_mode=pl.Buffered(1))


def _resident_layer(shape, layer):
    return pl.BlockSpec((None,) + tuple(shape), lambda *_: (layer,) + (0,) * len(shape),
                        pipeline_mode=pl.Buffered(1))


def _params(n_axes):
    return pltpu.CompilerParams(dimension_semantics=("arbitrary",) * n_axes,
                                vmem_limit_bytes=VMEM_LIMIT_BYTES)


def _split3(x):
    hi = x.astype(BF16).astype(F32)
    rem = x - hi
    mid = rem.astype(BF16).astype(F32)
    return hi, mid, rem - mid


def _proj_kernel(h_ref, g_ref, wm_ref, wf_ref, main_ref, f_ref):
    xn = _rms_norm_rows(h_ref[...], g_ref[...]).astype(BF16)
    for c in range(N_MAIN // COL_CHUNK):
        cs = slice(c * COL_CHUNK, (c + 1) * COL_CHUNK)
        main_ref[:, cs] = _dot(xn, wm_ref[:, cs]).astype(BF16)
    f_ref[...] = _dot(xn, wf_ref[...])


def _proj(h, g, w_in_all, layer, w_f):
    t = h.shape[0]
    return pl.pallas_call(
        _proj_kernel,
        grid=(t // TOKEN_TILE,),
        in_specs=[pl.BlockSpec((TOKEN_TILE, D_MODEL), lambda i: (i, 0)),
                  _resident((1, D_MODEL)),
                  _resident_layer((D_MODEL, N_MAIN), layer),
                  _resident((D_MODEL, LANES))],
        out_specs=[pl.BlockSpec((TOKEN_TILE, N_MAIN), lambda i: (i, 0)),
                   pl.BlockSpec((TOKEN_TILE, LANES), lambda i: (i, 0))],
        out_shape=[jax.ShapeDtypeStruct((t, N_MAIN), BF16),
                   jax.ShapeDtypeStruct((t, LANES), F32)],
        compiler_params=_params(1),
        name="proj",
    )(h, g, w_in_all, w_f)


def _convgmlp_kernel(cb_ref, cc_ref, cx_ref, u_ref, v_ref, wconv_ref, ws_ref, bs_ref,
                     lng_ref, lnb_ref, o_ref):
    seq = cb_ref.shape[0]
    n_chunks = seq // GMLP_CHUNK
    row = lax.broadcasted_iota(jnp.int32, (GMLP_CHUNK, GMLP_CHUNK), 0)
    col = lax.broadcasted_iota(jnp.int32, (GMLP_CHUNK, GMLP_CHUNK), 1)
    w_tril = [jnp.where(col <= row, ws_ref[gi], 0.0).astype(BF16) for gi in range(GMLP_GROUPS)]
    lane = lax.broadcasted_iota(jnp.int32, (GMLP_CHUNK, LANES), 1)
    first_group = lane < GROUP_WIDTH
    w0 = wconv_ref[0:1, :]
    w1 = wconv_ref[1:2, :]
    w2 = wconv_ref[2:3, :]

    def chunk(c, prev_tail):
        rows = pl.ds(pl.multiple_of(c * GMLP_CHUNK, GMLP_CHUNK), GMLP_CHUNK)
        xc = cc_ref[rows, :].astype(F32) * cx_ref[rows, :].astype(F32)
        win = jnp.concatenate([prev_tail, xc], axis=0)
        xc1 = pltpu.roll(win, 1, 0)[SUBLANES:, :]
        xc2 = pltpu.roll(win, 2, 0)[SUBLANES:, :]
        ya = cb_ref[rows, :].astype(F32) * (w0 * xc2 + w1 * xc1 + w2 * xc)
        o_ref[rows, 0:MIX_WIDTH] = ya.astype(BF16)

        gu = jax.nn.gelu(u_ref[rows, :].astype(F32))
        gv = jax.nn.gelu(v_ref[rows, :].astype(F32))
        mu = jnp.mean(gv, axis=-1, keepdims=True)
        cen = gv - mu
        var = jnp.mean(cen * cen, axis=-1, keepdims=True)
        vn = (cen * lax.rsqrt(var + EPS) * lng_ref[...] + lnb_ref[...]).astype(BF16)
        halves = []
        for lb in range(MIX_WIDTH // LANES):
            vb = vn[:, lb * LANES:(lb + 1) * LANES]
            m0 = _dot(w_tril[2 * lb], vb)
            m1 = _dot(w_tril[2 * lb + 1], vb)
            halves.append(jnp.where(first_group, m0, m1))
        mixed = jnp.concatenate(halves, axis=1) + bs_ref[...]
        o_ref[rows, MIX_WIDTH:2 * MIX_WIDTH] = (gu * mixed).astype(BF16)
        return xc[GMLP_CHUNK - SUBLANES:, :]

    lax.fori_loop(0, n_chunks // 2, lambda c2, tail: chunk(2 * c2 + 1, chunk(2 * c2, tail)),
                  jnp.zeros((SUBLANES, MIX_WIDTH), F32))


def _convgmlp(main3, w_conv, w_s, bs_rows, ln_g, ln_b):
    b, s, _ = main3.shape
    col = lambda k: pl.BlockSpec((None, s, MIX_WIDTH), lambda i, k=k: (i, 0, k))
    return pl.pallas_call(
        _convgmlp_kernel,
        grid=(b,),
        in_specs=[col(0), col(1), col(2), col(3), col(4),
                  _resident((CONV_K, MIX_WIDTH)),
                  _resident((GMLP_GROUPS, GMLP_CHUNK, GMLP_CHUNK)),
                  _resident((GMLP_CHUNK, MIX_WIDTH)),
                  _resident((1, MIX_WIDTH)),
                  _resident((1, MIX_WIDTH))],
        out_specs=pl.BlockSpec((None, s, 2 * MIX_WIDTH), lambda i: (i, 0, 0)),
        out_shape=jax.ShapeDtypeStruct((b, s, 2 * MIX_WIDTH), BF16),
        compiler_params=_params(1),
        name="convgmlp",
    )(main3, main3, main3, main3, main3, w_conv, w_s, bs_rows, ln_g, ln_b)


def _pair(x, h):
    p = (h * HEAD_DIM) // LANES
    return x[:, p * LANES:(p + 1) * LANES]


def _head_in_pair_mask(rows, h):
    lane = lax.broadcasted_iota(jnp.int32, (rows, LANES), 1)
    first = lane < HEAD_DIM
    return first if (h * HEAD_DIM) % LANES == 0 else jnp.logical_not(first)


def _masked_head(x, h):
    xp = _pair(x, h)
    return jnp.where(_head_in_pair_mask(x.shape[0], h), xp, jnp.zeros_like(xp))


def _merge_heads(per_head):
    rows = per_head[0].shape[0]
    blocks = []
    for p in range(MIX_WIDTH // LANES):
        h0 = p * (LANES // HEAD_DIM)
        blocks.append(jnp.where(_head_in_pair_mask(rows, h0), per_head[h0], per_head[h0 + 1]))
    return jnp.concatenate(blocks, axis=1)


def _sb_kernel(q_ref, k_ref, v_ref, o_ref, carry_scr, acc_scr):
    step_id = pl.program_id(1)
    r = lax.broadcasted_iota(jnp.int32, (ATTN_TK, ATTN_TK), 0)
    c = lax.broadcasted_iota(jnp.int32, (ATTN_TK, ATTN_TK), 1)
    suffix = jnp.where(r > c, 1.0, 0.0).astype(BF16)
    rows_all = N_HEADS * ATTN_TQ
    rq = lax.broadcasted_iota(jnp.int32, (rows_all, ATTN_TK), 0) & (ATTN_TQ - 1)
    cq = lax.broadcasted_iota(jnp.int32, (rows_all, ATTN_TK), 1)
    strict = cq < rq
    pair_rows = HEADS_PER_PAIR * ATTN_TQ

    def make_group(u, i):
        q = q_ref[pl.ds(pl.multiple_of(i * ATTN_TQ, ATTN_TQ), ATTN_TQ), :]
        qneg = [jnp.concatenate([_masked_head(q, HEADS_PER_PAIR * p + e) for e in range(HEADS_PER_PAIR)],
                                axis=0) * (-SCALE) for p in range(N_PAIRS)]

        def group(blocks, first):
            carry = jnp.zeros((rows_all, 1), F32) if first else carry_scr[u]
            staged = []
            for j, valid, diag in blocks:
                ks = pl.ds(pl.multiple_of(j * ATTN_TK, ATTN_TK), ATTN_TK)
                k = k_ref[ks, :]
                zn = jnp.concatenate(
                    [_dot_nt(qneg[p], k[:, p * LANES:(p + 1) * LANES]) for p in range(N_PAIRS)], axis=0)
                zn2 = zn * LOG2_E
                l2 = jnp.minimum(zn2, 0.0) - jnp.log2(1.0 + jnp.exp2(-jnp.abs(zn2)))
                if diag:
                    l2 = jnp.where(strict, l2, 0.0)
                staged.append((ks, valid, diag, l2 - zn2, _dot(l2.astype(BF16), suffix),
                               jnp.sum(l2, axis=1, keepdims=True)))
            pv = None
            for ks, valid, diag, log2_beta, later_in, row_sum in staged:
                if valid is None:
                    w = jnp.exp2(later_in + carry + log2_beta)
                else:
                    w = jnp.exp2(later_in + (carry + jnp.where(valid, 0.0, NEG_BIG)) + log2_beta)
                    row_sum = jnp.where(valid, row_sum, 0.0)
                if diag:
                    w = jnp.where(strict, w, 0.0)
                wb = w.astype(BF16)
                v = v_ref[ks, :]
                contrib = jnp.concatenate(
                    [_dot(wb[p * pair_rows:(p + 1) * pair_rows, :], v[:, p * LANES:(p + 1) * LANES])
                     for p in range(N_PAIRS)], axis=0)
                pv = contrib if pv is None else pv + contrib
                carry = carry + row_sum
            if first:
                acc_scr[u] = pv
            else:
                acc_scr[u] += pv
            carry_scr[u] = carry

        return group

    query_blocks = [(u, SB_QBLOCKS * step_id + u) for u in range(SB_QBLOCKS)]
    groups = [make_group(u, i) for u, i in query_blocks]
    for (u, i), group in zip(query_blocks, groups):
        group([(i, None, True), (jnp.maximum(i - 1, 0), i >= 1, False)], True)

    for (u, i), group in zip(query_blocks, groups):
        def alive(u=u):
            return jnp.max(carry_scr[u]) > LOG2_F32_UNDERFLOW

        first_left = i - 2
        n_groups = i // 2

        def step(st, group=group, alive=alive, first_left=first_left):
            t, _ = st
            j_near = first_left - 2 * t
            group([(j_near, None, False), (jnp.maximum(j_near - 1, 0), j_near >= 1, False)], False)
            return t + 1, alive()

        lax.while_loop(lambda st, n_groups=n_groups: (st[0] < n_groups) & st[1], step, (jnp.int32(0), alive()))
        acc = acc_scr[u]
        o_ref[u * ATTN_TQ:(u + 1) * ATTN_TQ, :] = _merge_heads(
            [acc[h * ATTN_TQ:(h + 1) * ATTN_TQ, :] for h in range(N_HEADS)]).astype(BF16)


def _sb_attention(main3):
    b, s, _ = main3.shape
    step_rows = SB_QBLOCKS * ATTN_TQ
    col = lambda k: pl.BlockSpec((None, s, MIX_WIDTH), lambda bi, i, k=k: (bi, 0, k))
    return pl.pallas_call(
        _sb_kernel,
        grid=(b, s // step_rows),
        in_specs=[col(5), col(6), col(7)],
        out_specs=pl.BlockSpec((None, step_rows, MIX_WIDTH), lambda bi, i: (bi, i, 0)),
        out_shape=jax.ShapeDtypeStruct((b, s, MIX_WIDTH), BF16),
        scratch_shapes=[pltpu.VMEM((SB_QBLOCKS, N_HEADS * ATTN_TQ, 1), F32),
                        pltpu.VMEM((SB_QBLOCKS, N_HEADS * ATTN_TQ, LANES), F32)],
        compiler_params=_params(2),
        name="sb",
    )(main3, main3, main3)


def _fox_kernel(q_ref, k_ref, v_ref, f_ref, bf_ref, qg_ref, kg_ref, o_ref,
                qn_scr, kn_scr, qaug_scr, kaug_scr, s_scr, mrun_scr, lrun_scr, acc_scr):
    step_id = pl.program_id(1)
    seq = q_ref.shape[0]

    @pl.when(step_id == 0)
    def _prepare():
        for u in range(FOX_QBLOCKS):
            s_scr[u, seq // ATTN_TK] = jnp.full((N_HEADS * ATTN_TQ, ATTN_TK), NEG_BIG, F32)
        r = lax.broadcasted_iota(jnp.int32, (MIX_WIDTH, MIX_WIDTH), 0) // HEAD_DIM
        c = lax.broadcasted_iota(jnp.int32, (MIX_WIDTH, MIX_WIDTH), 1) // HEAD_DIM
        same_head = jnp.where(r == c, 1.0, 0.0).astype(BF16)
        tr = lax.broadcasted_iota(jnp.int32, (FOX_PREP_ROWS, FOX_PREP_ROWS), 0)
        tc = lax.broadcasted_iota(jnp.int32, (FOX_PREP_ROWS, FOX_PREP_ROWS), 1)
        prefix = jnp.where(tc <= tr, 1.0, 0.0).astype(BF16)
        lane = lax.broadcasted_iota(jnp.int32, (FOX_PREP_ROWS, LANES), 1)
        n = N_HEADS

        def tile(tix, run):
            rows = pl.ds(pl.multiple_of(tix * FOX_PREP_ROWS, FOX_PREP_ROWS), FOX_PREP_ROWS)
            for x_ref, g_ref, scale, dst in ((q_ref, qg_ref, SCALE, qn_scr), (k_ref, kg_ref, 1.0, kn_scr)):
                x = x_ref[rows, :].astype(F32)
                hi, lo = _split_hi_lo(x * x)
                ms = (_dot(hi, same_head) + _dot(lo, same_head)) * (1.0 / HEAD_DIM)
                dst[rows, :] = (x * lax.rsqrt(ms + EPS) * (g_ref[...] * scale)).astype(BF16)
            hi, lo = _split_hi_lo(jax.nn.log_sigmoid(f_ref[rows, :] + bf_ref[...]))
            cum = run + _dot(prefix, hi) + _dot(prefix, lo)
            c_hi, c_mid, c_lo = _split3(cum)
            q_aug = jnp.where(lane < n, c_hi, jnp.where(lane < 2 * n, c_mid, jnp.where(
                lane < 3 * n, c_lo, jnp.where(lane < 6 * n, 1.0, 0.0))))
            k_aug = jnp.where(lane < 3 * n, 1.0, jnp.where(lane < 4 * n, -c_hi, jnp.where(
                lane < 5 * n, -c_mid, jnp.where(lane < 6 * n, -c_lo, 0.0))))
            qaug_scr[rows, :] = q_aug.astype(BF16)
            kaug_scr[rows, :] = k_aug.astype(BF16)
            return cum[FOX_PREP_ROWS - 1:FOX_PREP_ROWS, :]

        lax.fori_loop(0, seq // FOX_PREP_ROWS, tile, jnp.zeros((1, LANES), F32))

    n_blocks = seq // ATTN_TK
    rows_all = N_HEADS * ATTN_TQ
    pair_rows = HEADS_PER_PAIR * ATTN_TQ
    lane = lax.broadcasted_iota(jnp.int32, (ATTN_TQ, LANES), 1)
    r = lax.broadcasted_iota(jnp.int32, (rows_all, ATTN_TK), 0) & (ATTN_TQ - 1)
    c = lax.broadcasted_iota(jnp.int32, (rows_all, ATTN_TK), 1)
    causal = c <= r
    query_blocks = [(u, FOX_QBLOCKS * step_id + u) for u in range(FOX_QBLOCKS)]

    def key_rows(j):
        return pl.ds(pl.multiple_of(j * ATTN_TK, ATTN_TK), ATTN_TK)

    def fold(x):
        return [x[:, n * LANES:(n + 1) * LANES] for n in range(ATTN_TK // LANES)]

    def make_scores(i):
        qrows = key_rows(i)
        q = qn_scr[qrows, :]
        qaug = qaug_scr[qrows, :]
        qs = []
        for p in range(N_PAIRS):
            rows = []
            for e in range(HEADS_PER_PAIR):
                h = HEADS_PER_PAIR * p + e
                aug_h = jnp.where((lane & (N_HEADS - 1)) == h, qaug, jnp.zeros_like(qaug))
                rows.append(jnp.concatenate([_masked_head(q, h), aug_h], axis=1))
            qs.append(jnp.concatenate(rows, axis=0))

        def scores(j):
            ks = key_rows(j)
            kaug = kaug_scr[ks, :]
            return jnp.concatenate(
                [_dot_nt(qs[p], jnp.concatenate([kn_scr[ks, p * LANES:(p + 1) * LANES], kaug], axis=1))
                 for p in range(N_PAIRS)], axis=0) * LOG2_E

        return scores

    def accumulate(blocks):
        l_parts, pv = [], None
        for j, shifted in blocks:
            ks = key_rows(j)
            p = jnp.exp2(shifted)
            l_parts += fold(p)
            pb = p.astype(BF16)
            contrib = jnp.concatenate(
                [_dot(pb[pr * pair_rows:(pr + 1) * pair_rows, :], v_ref[ks, pr * LANES:(pr + 1) * LANES])
                 for pr in range(N_PAIRS)], axis=0)
            pv = contrib if pv is None else pv + contrib
        return functools.reduce(jnp.add, l_parts), pv

    score_fns = [make_scores(i) for _, i in query_blocks]
    for (u, i), scores in zip(query_blocks, score_fns):
        s_diag = jnp.where(causal, scores(i), NEG_BIG)
        s_scr[u, i] = s_diag
        mrun_scr[u] = functools.reduce(jnp.maximum, fold(s_diag))

    for (u, i), scores in zip(query_blocks, score_fns):
        def stage(t, _, u=u, i=i, scores=scores):
            parts = []
            for j in (2 * t, jnp.minimum(2 * t + 1, i - 1)):
                s = scores(j)
                s_scr[u, j] = s
                parts += fold(s)
            mrun_scr[u] = functools.reduce(jnp.maximum, parts, mrun_scr[u])
            return 0

        lax.fori_loop(0, (i + 1) // 2, stage, 0)

    row_max = []
    for u, i in query_blocks:
        m = jnp.max(mrun_scr[u], axis=1, keepdims=True)
        l, pv = accumulate([(i, s_scr[u, i] - m)])
        lrun_scr[u] = l
        acc_scr[u] = pv
        row_max.append(m)

    for (u, i), m in zip(query_blocks, row_max):
        def weigh(t, _, u=u, i=i, m=m):
            j1 = 2 * t + 1
            masked = j1 > i - 1
            blocks = [(2 * t, s_scr[u, 2 * t] - m),
                      (jnp.where(masked, 0, j1), s_scr[u, jnp.where(masked, n_blocks, j1)] - m)]
            l, pv = accumulate(blocks)
            lrun_scr[u] += l
            acc_scr[u] += pv
            return 0

        lax.fori_loop(0, (i + 1) // 2, weigh, 0)

    for u, _ in query_blocks:
        out = acc_scr[u] / jnp.sum(lrun_scr[u], axis=1, keepdims=True)
        o_ref[u * ATTN_TQ:(u + 1) * ATTN_TQ, :] = _merge_heads(
            [out[h * ATTN_TQ:(h + 1) * ATTN_TQ, :] for h in range(N_HEADS)]).astype(BF16)


def _fox_attention(main3, f3, bf_row, qg_row, kg_row):
    b, s, _ = main3.shape
    rows_all = N_HEADS * ATTN_TQ
    step_rows = FOX_QBLOCKS * ATTN_TQ
    col = lambda k: pl.BlockSpec((None, s, MIX_WIDTH), lambda bi, i, k=k: (bi, 0, k))
    return pl.pallas_call(
        _fox_kernel,
        grid=(b, s // step_rows),
        in_specs=[col(8), col(9), col(10),
                  pl.BlockSpec((None, s, LANES), lambda bi, i: (bi, 0, 0)),
                  _resident((1, LANES)),
                  _resident((1, MIX_WIDTH)),
                  _resident((1, MIX_WIDTH))],
        out_specs=pl.BlockSpec((None, step_rows, MIX_WIDTH), lambda bi, i: (bi, i, 0)),
        out_shape=jax.ShapeDtypeStruct((b, s, MIX_WIDTH), BF16),
        scratch_shapes=[pltpu.VMEM((s, MIX_WIDTH), BF16),
                        pltpu.VMEM((s, MIX_WIDTH), BF16),
                        pltpu.VMEM((s, LANES), BF16),
                        pltpu.VMEM((s, LANES), BF16),
                        pltpu.VMEM((FOX_QBLOCKS, s // ATTN_TK + 1, rows_all, ATTN_TK), F32),
                        pltpu.VMEM((FOX_QBLOCKS, rows_all, LANES), F32),
                        pltpu.VMEM((FOX_QBLOCKS, rows_all, LANES), F32),
                        pltpu.VMEM((FOX_QBLOCKS, rows_all, LANES), F32)],
        compiler_params=_params(2),
        name="fox",
    )(main3, main3, main3, f3, bf_row, qg_row, kg_row)


def _merge_kernel(h_ref, g_ref, yab_ref, yc_ref, yd_ref, wg_ref, wb_ref, wo_ref, o_ref, merged_scr):
    xn = _rms_norm_rows(h_ref[...], g_ref[...]).astype(BF16)
    ys = [yab_ref[:, 0:MIX_WIDTH], yab_ref[:, MIX_WIDTH:2 * MIX_WIDTH], yc_ref[...], yd_ref[...]]
    for c in range(D_MODEL // COL_CHUNK):
        cs = slice(c * COL_CHUNK, (c + 1) * COL_CHUNK)
        acc = None
        for n in range(N_BRANCH):
            gs = slice(n * D_MODEL + c * COL_CHUNK, n * D_MODEL + (c + 1) * COL_CHUNK)
            term = jax.nn.sigmoid(_dot(xn, wg_ref[:, gs])) * _dot(ys[n], wb_ref[n, :, cs])
            acc = term if acc is None else acc + term
        merged_scr[:, cs] = acc.astype(BF16)
    merged = merged_scr[...]
    for c in range(D_MODEL // COL_CHUNK):
        cs = slice(c * COL_CHUNK, (c + 1) * COL_CHUNK)
        o_ref[:, cs] = h_ref[:, cs] + _dot(merged, wo_ref[:, cs])


def _merge(h, g, yab, yc, yd, w_gate, w_branch, w_out, layer):
    t = h.shape[0]
    rows = lambda w: pl.BlockSpec((TOKEN_TILE, w), lambda i: (i, 0))
    return pl.pallas_call(
        _merge_kernel,
        grid=(t // TOKEN_TILE,),
        in_specs=[rows(D_MODEL), _resident((1, D_MODEL)),
                  rows(2 * MIX_WIDTH), rows(MIX_WIDTH), rows(MIX_WIDTH),
                  _resident_layer((D_MODEL, N_BRANCH * D_MODEL), layer),
                  _resident_layer((N_BRANCH, MIX_WIDTH, D_MODEL), layer),
                  _resident_layer((D_MODEL, D_MODEL), layer)],
        out_specs=rows(D_MODEL),
        out_shape=jax.ShapeDtypeStruct((t, D_MODEL), F32),
        scratch_shapes=[pltpu.VMEM((TOKEN_TILE, D_MODEL), BF16)],
        compiler_params=_params(1),
        name="merge",
    )(h, g, yab, yc, yd, w_gate, w_branch, w_out)


def _ffn_kernel(h_ref, g_ref, wi_ref, wo_ref, o_ref, acc_scr):
    xn = _rms_norm_rows(h_ref[...], g_ref[...]).astype(BF16)
    for c in range(FFN_HIDDEN // COL_CHUNK):
        gate = _dot(xn, wi_ref[:, c * COL_CHUNK:(c + 1) * COL_CHUNK])
        up = _dot(xn, wi_ref[:, FFN_HIDDEN + c * COL_CHUNK:FFN_HIDDEN + (c + 1) * COL_CHUNK])
        act = (jax.nn.silu(gate) * up).astype(BF16)
        part = _dot(act, wo_ref[c * COL_CHUNK:(c + 1) * COL_CHUNK, :])
        if c == 0:
            acc_scr[...] = part
        else:
            acc_scr[...] += part
    o_ref[...] = h_ref[...] + acc_scr[...]


def _ffn(h, g, w_in, w_out, layer):
    t = h.shape[0]
    rows = pl.BlockSpec((TOKEN_TILE, D_MODEL), lambda i: (i, 0))
    return pl.pallas_call(
        _ffn_kernel,
        grid=(t // TOKEN_TILE,),
        in_specs=[rows, _resident((1, D_MODEL)),
                  _resident_layer((D_MODEL, 2 * FFN_HIDDEN), layer),
                  _resident_layer((FFN_HIDDEN, D_MODEL), layer)],
        out_specs=rows,
        out_shape=jax.ShapeDtypeStruct((t, D_MODEL), F32),
        scratch_shapes=[pltpu.VMEM((TOKEN_TILE, D_MODEL), F32)],
        compiler_params=_params(1),
        name="ffn",
    )(h, g, w_in, w_out)


def kernel(x, norm_mix_g, w_in, w_conv, w_spatial, b_spatial, gmlp_ln_g, gmlp_ln_b,
           fox_q_norm_g, fox_k_norm_g, fox_forget_b, w_branch, w_out, norm_ffn_g,
           w_ffn_in, w_ffn_out):
    b, s, d = x.shape
    depth = w_in.shape[0]
    assert d == D_MODEL and s % max(ATTN_TQ, ATTN_TK, GMLP_CHUNK, TOKEN_TILE) == 0
    t = b * s
    h = x.reshape(t, d)
    w_in_b = w_in.astype(BF16)
    w_gate_b = w_in_b[:, :, N_MAIN + N_HEADS:]
    w_branch_b = w_branch.astype(BF16)
    w_out_b = w_out.astype(BF16)
    w_ffn_in_b = w_ffn_in.astype(BF16)
    w_ffn_out_b = w_ffn_out.astype(BF16)
    n_f = FORGET_COPIES * N_HEADS
    for l in range(depth):
        w_f = jnp.pad(jnp.tile(w_in_b[l, :, N_MAIN:N_MAIN + N_HEADS], (1, FORGET_COPIES)),
                      ((0, 0), (0, LANES - n_f)))
        bf_row = jnp.pad(jnp.tile(fox_forget_b[l], FORGET_COPIES), (0, LANES - n_f)).reshape(1, LANES)
        qg_row = jnp.tile(fox_q_norm_g[l], N_HEADS).reshape(1, MIX_WIDTH)
        kg_row = jnp.tile(fox_k_norm_g[l], N_HEADS).reshape(1, MIX_WIDTH)
        bs_rows = jnp.repeat(b_spatial[l].T, GROUP_WIDTH, axis=1)

        main, f_raw = _proj(h, norm_mix_g[l].reshape(1, d), w_in_b, l, w_f)
        main3 = main.reshape(b, s, N_MAIN)
        yab = _convgmlp(main3, w_conv[l], w_spatial[l], bs_rows,
                        gmlp_ln_g[l].reshape(1, MIX_WIDTH), gmlp_ln_b[l].reshape(1, MIX_WIDTH))
        yc = _sb_attention(main3)
        yd = _fox_attention(main3, f_raw.reshape(b, s, LANES), bf_row, qg_row, kg_row)
        h = _merge(h, norm_mix_g[l].reshape(1, d), yab.reshape(t, 2 * MIX_WIDTH),
                   yc.reshape(t, MIX_WIDTH), yd.reshape(t, MIX_WIDTH),
                   w_gate_b, w_branch_b, w_out_b, l)
        h = _ffn(h, norm_ffn_g[l].reshape(1, d), w_ffn_in_b, w_ffn_out_b, l)
    return h.reshape(b, s, d)
```

```python
import functools

import jax
import jax.numpy as jnp
from jax import lax
from jax.experimental import pallas as pl
from jax.experimental.pallas import tpu as pltpu

D_MODEL = 1024
MIX_WIDTH = 256
HEAD_DIM = 64
N_HEADS = MIX_WIDTH // HEAD_DIM
N_BRANCH = 4
CONV_K = 3
GMLP_GROUPS = 4
GMLP_CHUNK = 128
GROUP_WIDTH = MIX_WIDTH // GMLP_GROUPS
FFN_HIDDEN = 2816
EPS = 1e-6
N_MAIN = 11 * MIX_WIDTH
FOX_Q_CHUNK = 8
FOX_K_CHUNK = 9
LANES = 128
HEADS_PER_PAIR = LANES // HEAD_DIM
N_PAIRS = MIX_WIDTH // LANES
FORGET_COPIES = 6
FOX_PREP_ROWS = 256
SUBLANES = 8
VMEM_LIMIT_BYTES = 56 * 1024 * 1024

TOKEN_TILE = 1024
COL_CHUNK = 256
ATTN_TQ = 256
ATTN_TK = 256
SB_QBLOCKS = 4
FOX_QBLOCKS = 2
SCALE = HEAD_DIM ** -0.5
LOG2_E = 1.4426950408889634
LOG2_F32_UNDERFLOW = -151.0
NEG_BIG = -1e30

F32 = jnp.float32
BF16 = jnp.bfloat16


def _dot(a, b):
    return jnp.dot(a, b, preferred_element_type=F32)


def _dot_nt(a, b):
    return lax.dot_general(a, b, (((1,), (1,)), ((), ())), preferred_element_type=F32)


def _split_hi_lo(x):
    hi = x.astype(BF16)
    lo = (x - hi.astype(F32)).astype(BF16)
    return hi, lo


def _rms_norm_rows(x, g):
    ms = jnp.mean(x * x, axis=-1, keepdims=True)
    return x * lax.rsqrt(ms + EPS) * g


def _resident(shape):
    return pl.BlockSpec(shape, lambda *_: (0,) * len(shape), pipeline_mode=pl.Buffered(1))


def _resident_layer(shape, layer):
    return pl.BlockSpec((None,) + tuple(shape), lambda *_: (layer,) + (0,) * len(shape),
                        pipeline_mode=pl.Buffered(1))


def _params(n_axes):
    return pltpu.CompilerParams(dimension_semantics=("arbitrary",) * n_axes,
                                vmem_limit_bytes=VMEM_LIMIT_BYTES)


def _split3(x):
    hi = x.astype(BF16).astype(F32)
    rem = x - hi
    mid = rem.astype(BF16).astype(F32)
    return hi, mid, rem - mid


def _proj_kernel(h_ref, g_ref, wm_ref, wf_ref, main_ref, f_ref):
    xn = _rms_norm_rows(h_ref[...], g_ref[...]).astype(BF16)
    for c in range(N_MAIN // COL_CHUNK):
        cs = slice(c * COL_CHUNK, (c + 1) * COL_CHUNK)
        main_ref[:, cs] = _dot(xn, wm_ref[:, cs]).astype(BF16)
    f_ref[...] = _dot(xn, wf_ref[...])


def _proj(h, g, w_in_all, layer, w_f):
    t = h.shape[0]
    return pl.pallas_call(
        _proj_kernel,
        grid=(t // TOKEN_TILE,),
        in_specs=[pl.BlockSpec((TOKEN_TILE, D_MODEL), lambda i: (i, 0)),
                  _resident((1, D_MODEL)),
                  _resident_layer((D_MODEL, N_MAIN), layer),
                  _resident((D_MODEL, LANES))],
        out_specs=[pl.BlockSpec((TOKEN_TILE, N_MAIN), lambda i: (i, 0)),
                   pl.BlockSpec((TOKEN_TILE, LANES), lambda i: (i, 0))],
        out_shape=[jax.ShapeDtypeStruct((t, N_MAIN), BF16),
                   jax.ShapeDtypeStruct((t, LANES), F32)],
        compiler_params=_params(1),
        name="proj",
    )(h, g, w_in_all, w_f)


def _convgmlp_kernel(cb_ref, cc_ref, cx_ref, u_ref, v_ref, wconv_ref, ws_ref, bs_ref,
                     lng_ref, lnb_ref, o_ref):
    seq = cb_ref.shape[0]
    n_chunks = seq // GMLP_CHUNK
    row = lax.broadcasted_iota(jnp.int32, (GMLP_CHUNK, GMLP_CHUNK), 0)
    col = lax.broadcasted_iota(jnp.int32, (GMLP_CHUNK, GMLP_CHUNK), 1)
    w_tril = [jnp.where(col <= row, ws_ref[gi], 0.0).astype(BF16) for gi in range(GMLP_GROUPS)]
    lane = lax.broadcasted_iota(jnp.int32, (GMLP_CHUNK, LANES), 1)
    first_group = lane < GROUP_WIDTH
    w0 = wconv_ref[0:1, :]
    w1 = wconv_ref[1:2, :]
    w2 = wconv_ref[2:3, :]

    def chunk(c, prev_tail):
        rows = pl.ds(pl.multiple_of(c * GMLP_CHUNK, GMLP_CHUNK), GMLP_CHUNK)
        xc = cc_ref[rows, :].astype(F32) * cx_ref[rows, :].astype(F32)
        win = jnp.concatenate([prev_tail, xc], axis=0)
        xc1 = pltpu.roll(win, 1, 0)[SUBLANES:, :]
        xc2 = pltpu.roll(win, 2, 0)[SUBLANES:, :]
        ya = cb_ref[rows, :].astype(F32) * (w0 * xc2 + w1 * xc1 + w2 * xc)
        o_ref[rows, 0:MIX_WIDTH] = ya.astype(BF16)

        gu = jax.nn.gelu(u_ref[rows, :].astype(F32))
        gv = jax.nn.gelu(v_ref[rows, :].astype(F32))
        mu = jnp.mean(gv, axis=-1, keepdims=True)
        cen = gv - mu
        var = jnp.mean(cen * cen, axis=-1, keepdims=True)
        vn = (cen * lax.rsqrt(var + EPS) * lng_ref[...] + lnb_ref[...]).astype(BF16)
        halves = []
        for lb in range(MIX_WIDTH // LANES):
            vb = vn[:, lb * LANES:(lb + 1) * LANES]
            m0 = _dot(w_tril[2 * lb], vb)
            m1 = _dot(w_tril[2 * lb + 1], vb)
            halves.append(jnp.where(first_group, m0, m1))
        mixed = jnp.concatenate(halves, axis=1) + bs_ref[...]
        o_ref[rows, MIX_WIDTH:2 * MIX_WIDTH] = (gu * mixed).astype(BF16)
        return xc[GMLP_CHUNK - SUBLANES:, :]

    lax.fori_loop(0, n_chunks // 2, lambda c2, tail: chunk(2 * c2 + 1, chunk(2 * c2, tail)),
                  jnp.zeros((SUBLANES, MIX_WIDTH), F32))


def _convgmlp(main3, w_conv, w_s, bs_rows, ln_g, ln_b):
    b, s, _ = main3.shape
    col = lambda k: pl.BlockSpec((None, s, MIX_WIDTH), lambda i, k=k: (i, 0, k))
    return pl.pallas_call(
        _convgmlp_kernel,
        grid=(b,),
        in_specs=[col(0), col(1), col(2), col(3), col(4),
                  _resident((CONV_K, MIX_WIDTH)),
                  _resident((GMLP_GROUPS, GMLP_CHUNK, GMLP_CHUNK)),
                  _resident((GMLP_CHUNK, MIX_WIDTH)),
                  _resident((1, MIX_WIDTH)),
                  _resident((1, MIX_WIDTH))],
        out_specs=pl.BlockSpec((None, s, 2 * MIX_WIDTH), lambda i: (i, 0, 0)),
        out_shape=jax.ShapeDtypeStruct((b, s, 2 * MIX_WIDTH), BF16),
        compiler_params=_params(1),
        name="convgmlp",
    )(main3, main3, main3, main3, main3, w_conv, w_s, bs_rows, ln_g, ln_b)


def _pair(x, h):
    p = (h * HEAD_DIM) // LANES
    return x[:, p * LANES:(p + 1) * LANES]


def _head_in_pair_mask(rows, h):
    lane = lax.broadcasted_iota(jnp.int32, (rows, LANES), 1)
    first = lane < HEAD_DIM
    return first if (h * HEAD_DIM) % LANES == 0 else jnp.logical_not(first)


def _masked_head(x, h):
    xp = _pair(x, h)
    return jnp.where(_head_in_pair_mask(x.shape[0], h), xp, jnp.zeros_like(xp))


def _merge_heads(per_head):
    rows = per_head[0].shape[0]
    blocks = []
    for p in range(MIX_WIDTH // LANES):
        h0 = p * (LANES // HEAD_DIM)
        blocks.append(jnp.where(_head_in_pair_mask(rows, h0), per_head[h0], per_head[h0 + 1]))
    return jnp.concatenate(blocks, axis=1)


def _sb_kernel(q_ref, k_ref, v_ref, o_ref, carry_scr, acc_scr):
    step_id = pl.program_id(1)
    r = lax.broadcasted_iota(jnp.int32, (ATTN_TK, ATTN_TK), 0)
    c = lax.broadcasted_iota(jnp.int32, (ATTN_TK, ATTN_TK), 1)
    suffix = jnp.where(r > c, 1.0, 0.0).astype(BF16)
    rows_all = N_HEADS * ATTN_TQ
    rq = lax.broadcasted_iota(jnp.int32, (rows_all, ATTN_TK), 0) & (ATTN_TQ - 1)
    cq = lax.broadcasted_iota(jnp.int32, (rows_all, ATTN_TK), 1)
    strict = cq < rq
    pair_rows = HEADS_PER_PAIR * ATTN_TQ

    def make_group(u, i):
        q = q_ref[pl.ds(pl.multiple_of(i * ATTN_TQ, ATTN_TQ), ATTN_TQ), :]
        qneg = [jnp.concatenate([_masked_head(q, HEADS_PER_PAIR * p + e) for e in range(HEADS_PER_PAIR)],
                                axis=0) * (-SCALE) for p in range(N_PAIRS)]

        def group(blocks, first):
            carry = jnp.zeros((rows_all, 1), F32) if first else carry_scr[u]
            staged = []
            for j, valid, diag in blocks:
                ks = pl.ds(pl.multiple_of(j * ATTN_TK, ATTN_TK), ATTN_TK)
                k = k_ref[ks, :]
                zn = jnp.concatenate(
                    [_dot_nt(qneg[p], k[:, p * LANES:(p + 1) * LANES]) for p in range(N_PAIRS)], axis=0)
                zn2 = zn * LOG2_E
                l2 = jnp.minimum(zn2, 0.0) - jnp.log2(1.0 + jnp.exp2(-jnp.abs(zn2)))
                if diag:
                    l2 = jnp.where(strict, l2, 0.0)
                staged.append((ks, valid, diag, l2 - zn2, _dot(l2.astype(BF16), suffix),
                               jnp.sum(l2, axis=1, keepdims=True)))
            pv = None
            for ks, valid, diag, log2_beta, later_in, row_sum in staged:
                if valid is None:
                    w = jnp.exp2(later_in + carry + log2_beta)
                else:
                    w = jnp.exp2(later_in + (carry + jnp.where(valid, 0.0, NEG_BIG)) + log2_beta)
                    row_sum = jnp.where(valid, row_sum, 0.0)
                if diag:
                    w = jnp.where(strict, w, 0.0)
                wb = w.astype(BF16)
                v = v_ref[ks, :]
                contrib = jnp.concatenate(
                    [_dot(wb[p * pair_rows:(p + 1) * pair_rows, :], v[:, p * LANES:(p + 1) * LANES])
                     for p in range(N_PAIRS)], axis=0)
                pv = contrib if pv is None else pv + contrib
                carry = carry + row_sum
            if first:
                acc_scr[u] = pv
            else:
                acc_scr[u] += pv
            carry_scr[u] = carry

        return group

    query_blocks = [(u, SB_QBLOCKS * step_id + u) for u in range(SB_QBLOCKS)]
    groups = [make_group(u, i) for u, i in query_blocks]
    for (u, i), group in zip(query_blocks, groups):
        group([(i, None, True), (jnp.maximum(i - 1, 0), i >= 1, False)], True)

    for (u, i), group in zip(query_blocks, groups):
        def alive(u=u):
            return jnp.max(carry_scr[u]) > LOG2_F32_UNDERFLOW

        first_left = i - 2
        n_groups = i // 2

        def step(st, group=group, alive=alive, first_left=first_left):
            t, _ = st
            j_near = first_left - 2 * t
            group([(j_near, None, False), (jnp.maximum(j_near - 1, 0), j_near >= 1, False)], False)
            return t + 1, alive()

        lax.while_loop(lambda st, n_groups=n_groups: (st[0] < n_groups) & st[1], step, (jnp.int32(0), alive()))
        acc = acc_scr[u]
        o_ref[u * ATTN_TQ:(u + 1) * ATTN_TQ, :] = _merge_heads(
            [acc[h * ATTN_TQ:(h + 1) * ATTN_TQ, :] for h in range(N_HEADS)]).astype(BF16)


def _sb_attention(main3):
    b, s, _ = main3.shape
    step_rows = SB_QBLOCKS * ATTN_TQ
    col = lambda k: pl.BlockSpec((None, s, MIX_WIDTH), lambda bi, i, k=k: (bi, 0, k))
    return pl.pallas_call(
        _sb_kernel,
        grid=(b, s // step_rows),
        in_specs=[col(5), col(6), col(7)],
        out_specs=pl.BlockSpec((None, step_rows, MIX_WIDTH), lambda bi, i: (bi, i, 0)),
        out_shape=jax.ShapeDtypeStruct((b, s, MIX_WIDTH), BF16),
        scratch_shapes=[pltpu.VMEM((SB_QBLOCKS, N_HEADS * ATTN_TQ, 1), F32),
                        pltpu.VMEM((SB_QBLOCKS, N_HEADS * ATTN_TQ, LANES), F32)],
        compiler_params=_params(2),
        name="sb",
    )(main3, main3, main3)


def _fox_kernel(q_ref, k_ref, v_ref, f_ref, bf_ref, qg_ref, kg_ref, o_ref,
                qn_scr, kn_scr, qaug_scr, kaug_scr, s_scr, mrun_scr, lrun_scr, acc_scr):
    step_id = pl.program_id(1)
    seq = q_ref.shape[0]

    @pl.when(step_id == 0)
    def _prepare():
        r = lax.broadcasted_iota(jnp.int32, (MIX_WIDTH, MIX_WIDTH), 0) // HEAD_DIM
        c = lax.broadcasted_iota(jnp.int32, (MIX_WIDTH, MIX_WIDTH), 1) // HEAD_DIM
        same_head = jnp.where(r == c, 1.0, 0.0).astype(BF16)
        tr = lax.broadcasted_iota(jnp.int32, (FOX_PREP_ROWS, FOX_PREP_ROWS), 0)
        tc = lax.broadcasted_iota(jnp.int32, (FOX_PREP_ROWS, FOX_PREP_ROWS), 1)
        prefix = jnp.where(tc <= tr, 1.0, 0.0).astype(BF16)
        lane = lax.broadcasted_iota(jnp.int32, (FOX_PREP_ROWS, LANES), 1)
        n = N_HEADS

        def tile(tix, run):
            rows = pl.ds(pl.multiple_of(tix * FOX_PREP_ROWS, FOX_PREP_ROWS), FOX_PREP_ROWS)
            for x_ref, g_ref, scale, dst in ((q_ref, qg_ref, SCALE, qn_scr), (k_ref, kg_ref, 1.0, kn_scr)):
                x = x_ref[rows, :].astype(F32)
                hi, lo = _split_hi_lo(x * x)
                ms = (_dot(hi, same_head) + _dot(lo, same_head)) * (1.0 / HEAD_DIM)
                dst[rows, :] = (x * lax.rsqrt(ms + EPS) * (g_ref[...] * scale)).astype(BF16)
            hi, lo = _split_hi_lo(jax.nn.log_sigmoid(f_ref[rows, :] + bf_ref[...]))
            cum = run + _dot(prefix, hi) + _dot(prefix, lo)
            c_hi, c_mid, c_lo = _split3(cum)
            q_aug = jnp.where(lane < n, c_hi, jnp.where(lane < 2 * n, c_mid, jnp.where(
                lane < 3 * n, c_lo, jnp.where(lane < 6 * n, 1.0, 0.0))))
            k_aug = jnp.where(lane < 3 * n, 1.0, jnp.where(lane < 4 * n, -c_hi, jnp.where(
                lane < 5 * n, -c_mid, jnp.where(lane < 6 * n, -c_lo, 0.0))))
            qaug_scr[rows, :] = q_aug.astype(BF16)
            kaug_scr[rows, :] = k_aug.astype(BF16)
            return cum[FOX_PREP_ROWS - 1:FOX_PREP_ROWS, :]

        lax.fori_loop(0, seq // FOX_PREP_ROWS, tile, jnp.zeros((1, LANES), F32))

    rows_all = N_HEADS * ATTN_TQ
    pair_rows = HEADS_PER_PAIR * ATTN_TQ
    lane = lax.broadcasted_iota(jnp.int32, (ATTN_TQ, LANES), 1)
    r = lax.broadcasted_iota(jnp.int32, (rows_all, ATTN_TK), 0) & (ATTN_TQ - 1)
    c = lax.broadcasted_iota(jnp.int32, (rows_all, ATTN_TK), 1)
    causal = c <= r
    query_blocks = [(u, FOX_QBLOCKS * step_id + u) for u in range(FOX_QBLOCKS)]

    def key_rows(j):
        return pl.ds(pl.multiple_of(j * ATTN_TK, ATTN_TK), ATTN_TK)

    def fold(x):
        return [x[:, n * LANES:(n + 1) * LANES] for n in range(ATTN_TK // LANES)]

    def make_scores(i):
        qrows = key_rows(i)
        q = qn_scr[qrows, :]
        qaug = qaug_scr[qrows, :]
        qs = []
        for p in range(N_PAIRS):
            rows = []
            for e in range(HEADS_PER_PAIR):
                h = HEADS_PER_PAIR * p + e
                aug_h = jnp.where((lane & (N_HEADS - 1)) == h, qaug, jnp.zeros_like(qaug))
                rows.append(jnp.concatenate([_masked_head(q, h), aug_h], axis=1))
            qs.append(jnp.concatenate(rows, axis=0))

        def scores(j):
            ks = key_rows(j)
            kaug = kaug_scr[ks, :]
            return jnp.concatenate(
                [_dot_nt(qs[p], jnp.concatenate([kn_scr[ks, p * LANES:(p + 1) * LANES], kaug], axis=1))
                 for p in range(N_PAIRS)], axis=0) * LOG2_E

        return scores

    def accumulate(blocks):
        l_parts, pv = [], None
        for j, shifted in blocks:
            ks = key_rows(j)
            p = jnp.exp2(shifted)
            l_parts += fold(p)
            pb = p.astype(BF16)
            contrib = jnp.concatenate(
                [_dot(pb[pr * pair_rows:(pr + 1) * pair_rows, :], v_ref[ks, pr * LANES:(pr + 1) * LANES])
                 for pr in range(N_PAIRS)], axis=0)
            pv = contrib if pv is None else pv + contrib
        return functools.reduce(jnp.add, l_parts), pv

    assert FOX_QBLOCKS == 2
    i0 = FOX_QBLOCKS * step_id
    head_blocks = [[i0], [i0 + 1, i0]]
    score_fns = [make_scores(i) for _, i in query_blocks]

    for (u, i), scores in zip(query_blocks, score_fns):
        parts = []
        for n, j in enumerate(head_blocks[u]):
            s = jnp.where(causal, scores(j), NEG_BIG) if n == 0 else scores(j)
            s_scr[u, j] = s
            parts += fold(s)
        mrun_scr[u] = functools.reduce(jnp.maximum, parts)

    def stage(t, _):
        for (u, _), scores in zip(query_blocks, score_fns):
            parts = []
            for j in (2 * t, 2 * t + 1):
                s = scores(j)
                s_scr[u, j] = s
                parts += fold(s)
            mrun_scr[u] = functools.reduce(jnp.maximum, parts, mrun_scr[u])
        return 0

    lax.fori_loop(0, step_id, stage, 0)

    row_max = []
    for u, _ in query_blocks:
        m = jnp.max(mrun_scr[u], axis=1, keepdims=True)
        l, pv = accumulate([(j, s_scr[u, j] - m) for j in head_blocks[u]])
        lrun_scr[u] = l
        acc_scr[u] = pv
        row_max.append(m)

    def weigh(t, _):
        for (u, _), m in zip(query_blocks, row_max):
            l, pv = accumulate([(j, s_scr[u, j] - m) for j in (2 * t, 2 * t + 1)])
            lrun_scr[u] += l
            acc_scr[u] += pv
        return 0

    lax.fori_loop(0, step_id, weigh, 0)

    for u, _ in query_blocks:
        out = acc_scr[u] / jnp.sum(lrun_scr[u], axis=1, keepdims=True)
        o_ref[u * ATTN_TQ:(u + 1) * ATTN_TQ, :] = _merge_heads(
            [out[h * ATTN_TQ:(h + 1) * ATTN_TQ, :] for h in range(N_HEADS)]).astype(BF16)


def _fox_attention(main3, f3, bf_row, qg_row, kg_row):
    b, s, _ = main3.shape
    rows_all = N_HEADS * ATTN_TQ
    step_rows = FOX_QBLOCKS * ATTN_TQ
    col = lambda k: pl.BlockSpec((None, s, MIX_WIDTH), lambda bi, i, k=k: (bi, 0, k))
    return pl.pallas_call(
        _fox_kernel,
        grid=(b, s // step_rows),
        in_specs=[col(8), col(9), col(10),
                  pl.BlockSpec((None, s, LANES), lambda bi, i: (bi, 0, 0)),
                  _resident((1, LANES)),
                  _resident((1, MIX_WIDTH)),
                  _resident((1, MIX_WIDTH))],
        out_specs=pl.BlockSpec((None, step_rows, MIX_WIDTH), lambda bi, i: (bi, i, 0)),
        out_shape=jax.ShapeDtypeStruct((b, s, MIX_WIDTH), BF16),
        scratch_shapes=[pltpu.VMEM((s, MIX_WIDTH), BF16),
                        pltpu.VMEM((s, MIX_WIDTH), BF16),
                        pltpu.VMEM((s, LANES), BF16),
                        pltpu.VMEM((s, LANES), BF16),
                        pltpu.VMEM((FOX_QBLOCKS, s // ATTN_TK, rows_all, ATTN_TK), F32),
                        pltpu.VMEM((FOX_QBLOCKS, rows_all, LANES), F32),
                        pltpu.VMEM((FOX_QBLOCKS, rows_all, LANES), F32),
                        pltpu.VMEM((FOX_QBLOCKS, rows_all, LANES), F32)],
        compiler_params=_params(2),
        name="fox",
    )(main3, main3, main3, f3, bf_row, qg_row, kg_row)


def _merge_kernel(h_ref, g_ref, yab_ref, yc_ref, yd_ref, wg_ref, wb_ref, wo_ref, o_ref, merged_scr):
    xn = _rms_norm_rows(h_ref[...], g_ref[...]).astype(BF16)
    ys = [yab_ref[:, 0:MIX_WIDTH], yab_ref[:, MIX_WIDTH:2 * MIX_WIDTH], yc_ref[...], yd_ref[...]]
    for c in range(D_MODEL // COL_CHUNK):
        cs = slice(c * COL_CHUNK, (c + 1) * COL_CHUNK)
        acc = None
        for n in range(N_BRANCH):
            gs = slice(n * D_MODEL + c * COL_CHUNK, n * D_MODEL + (c + 1) * COL_CHUNK)
            term = jax.nn.sigmoid(_dot(xn, wg_ref[:, gs])) * _dot(ys[n], wb_ref[n, :, cs])
            acc = term if acc is None else acc + term
        merged_scr[:, cs] = acc.astype(BF16)
    merged = merged_scr[...]
    for c in range(D_MODEL // COL_CHUNK):
        cs = slice(c * COL_CHUNK, (c + 1) * COL_CHUNK)
        o_ref[:, cs] = h_ref[:, cs] + _dot(merged, wo_ref[:, cs])


def _merge(h, g, yab, yc, yd, w_gate, w_branch, w_out, layer):
    t = h.shape[0]
    rows = lambda w: pl.BlockSpec((TOKEN_TILE, w), lambda i: (i, 0))
    return pl.pallas_call(
        _merge_kernel,
        grid=(t // TOKEN_TILE,),
        in_specs=[rows(D_MODEL), _resident((1, D_MODEL)),
                  rows(2 * MIX_WIDTH), rows(MIX_WIDTH), rows(MIX_WIDTH),
                  _resident_layer((D_MODEL, N_BRANCH * D_MODEL), layer),
                  _resident_layer((N_BRANCH, MIX_WIDTH, D_MODEL), layer),
                  _resident_layer((D_MODEL, D_MODEL), layer)],
        out_specs=rows(D_MODEL),
        out_shape=jax.ShapeDtypeStruct((t, D_MODEL), F32),
        scratch_shapes=[pltpu.VMEM((TOKEN_TILE, D_MODEL), BF16)],
        compiler_params=_params(1),
        name="merge",
    )(h, g, yab, yc, yd, w_gate, w_branch, w_out)


def _ffn_kernel(h_ref, g_ref, wi_ref, wo_ref, o_ref, acc_scr):
    xn = _rms_norm_rows(h_ref[...], g_ref[...]).astype(BF16)
    for c in range(FFN_HIDDEN // COL_CHUNK):
        gate = _dot(xn, wi_ref[:, c * COL_CHUNK:(c + 1) * COL_CHUNK])
        up = _dot(xn, wi_ref[:, FFN_HIDDEN + c * COL_CHUNK:FFN_HIDDEN + (c + 1) * COL_CHUNK])
        act = (jax.nn.silu(gate) * up).astype(BF16)
        part = _dot(act, wo_ref[c * COL_CHUNK:(c + 1) * COL_CHUNK, :])
        if c == 0:
            acc_scr[...] = part
        else:
            acc_scr[...] += part
    o_ref[...] = h_ref[...] + acc_scr[...]


def _ffn(h, g, w_in, w_out, layer):
    t = h.shape[0]
    rows = pl.BlockSpec((TOKEN_TILE, D_MODEL), lambda i: (i, 0))
    return pl.pallas_call(
        _ffn_kernel,
        grid=(t // TOKEN_TILE,),
        in_specs=[rows, _resident((1, D_MODEL)),
                  _resident_layer((D_MODEL, 2 * FFN_HIDDEN), layer),
                  _resident_layer((FFN_HIDDEN, D_MODEL), layer)],
        out_specs=rows,
        out_shape=jax.ShapeDtypeStruct((t, D_MODEL), F32),
        scratch_shapes=[pltpu.VMEM((TOKEN_TILE, D_MODEL), F32)],
        compiler_params=_params(1),
        name="ffn",
    )(h, g, w_in, w_out)


def kernel(x, norm_mix_g, w_in, w_conv, w_spatial, b_spatial, gmlp_ln_g, gmlp_ln_b,
           fox_q_norm_g, fox_k_norm_g, fox_forget_b, w_branch, w_out, norm_ffn_g,
           w_ffn_in, w_ffn_out):
    b, s, d = x.shape
    depth = w_in.shape[0]
    assert d == D_MODEL and s % max(ATTN_TQ, ATTN_TK, GMLP_CHUNK, TOKEN_TILE) == 0
    t = b * s
    h = x.reshape(t, d)
    w_in_b = w_in.astype(BF16)
    w_gate_b = w_in_b[:, :, N_MAIN + N_HEADS:]
    w_branch_b = w_branch.astype(BF16)
    w_out_b = w_out.astype(BF16)
    w_ffn_in_b = w_ffn_in.astype(BF16)
    w_ffn_out_b = w_ffn_out.astype(BF16)
    n_f = FORGET_COPIES * N_HEADS
    for l in range(depth):
        w_f = jnp.pad(jnp.tile(w_in_b[l, :, N_MAIN:N_MAIN + N_HEADS], (1, FORGET_COPIES)),
                      ((0, 0), (0, LANES - n_f)))
        bf_row = jnp.pad(jnp.tile(fox_forget_b[l], FORGET_COPIES), (0, LANES - n_f)).reshape(1, LANES)
        qg_row = jnp.tile(fox_q_norm_g[l], N_HEADS).reshape(1, MIX_WIDTH)
        kg_row = jnp.tile(fox_k_norm_g[l], N_HEADS).reshape(1, MIX_WIDTH)
        bs_rows = jnp.repeat(b_spatial[l].T, GROUP_WIDTH, axis=1)

        main, f_raw = _proj(h, norm_mix_g[l].reshape(1, d), w_in_b, l, w_f)
        main3 = main.reshape(b, s, N_MAIN)
        yab = _convgmlp(main3, w_conv[l], w_spatial[l], bs_rows,
                        gmlp_ln_g[l].reshape(1, MIX_WIDTH), gmlp_ln_b[l].reshape(1, MIX_WIDTH))
        yc = _sb_attention(main3)
        yd = _fox_attention(main3, f_raw.reshape(b, s, LANES), bf_row, qg_row, kg_row)
        h = _merge(h, norm_mix_g[l].reshape(1, d), yab.reshape(t, 2 * MIX_WIDTH),
                   yc.reshape(t, MIX_WIDTH), yd.reshape(t, MIX_WIDTH),
                   w_gate_b, w_branch_b, w_out_b, l)
        h = _ffn(h, norm_ffn_g[l].reshape(1, d), w_ffn_in_b, w_ffn_out_b, l)
    return h.reshape(b, s, d)
```

```python
import functools

import jax
import jax.numpy as jnp
from jax import lax
from jax.experimental import pallas as pl
from jax.experimental.pallas import tpu as pltpu

D_MODEL = 1024
MIX_WIDTH = 256
HEAD_DIM = 64
N_HEADS = MIX_WIDTH // HEAD_DIM
N_BRANCH = 4
CONV_K = 3
GMLP_GROUPS = 4
GMLP_CHUNK = 128
GROUP_WIDTH = MIX_WIDTH // GMLP_GROUPS
FFN_HIDDEN = 2816
EPS = 1e-6
N_MAIN = 11 * MIX_WIDTH
(COL_CONV_B, COL_CONV_C, COL_CONV_X, COL_GMLP_U, COL_GMLP_V, COL_SB_Q, COL_SB_K, COL_SB_V,
 COL_FOX_Q, COL_FOX_K, COL_FOX_V) = range(11)
LANES = 128
HEADS_PER_PAIR = LANES // HEAD_DIM
N_PAIRS = MIX_WIDTH // LANES
FORGET_COPIES = 6
FOX_PREP_ROWS = 256
SUBLANES = 8
VMEM_LIMIT_BYTES = 56 * 1024 * 1024

TOKEN_TILE = 1024
COL_CHUNK = 256
ATTN_TQ = 256
ATTN_TK = 256
SB_QBLOCKS = 8
FOX_QBLOCKS = 2
SCALE = HEAD_DIM ** -0.5
LOG2_E = 1.4426950408889634
LOG2_F32_UNDERFLOW = -151.0
NEG_BIG = -1e30

F32 = jnp.float32
BF16 = jnp.bfloat16


def _dot(a, b):
    return jnp.dot(a, b, preferred_element_type=F32)


def _dot_nt(a, b):
    return lax.dot_general(a, b, (((1,), (1,)), ((), ())), preferred_element_type=F32)


def _split_hi_lo(x):
    hi = x.astype(BF16)
    lo = (x - hi.astype(F32)).astype(BF16)
    return hi, lo


def _rms_norm_rows(x, g):
    ms = jnp.mean(x * x, axis=-1, keepdims=True)
    return x * lax.rsqrt(ms + EPS) * g


def _resident(shape):
    return pl.BlockSpec(shape, lambda *_: (0,) * len(shape), pipeline_mode=pl.Buffered(1))


def _resident_layer(shape, layer):
    return pl.BlockSpec((None,) + tuple(shape), lambda *_: (layer,) + (0,) * len(shape),
                        pipeline_mode=pl.Buffered(1))


def _params(n_axes):
    return pltpu.CompilerParams(dimension_semantics=("arbitrary",) * n_axes,
                                vmem_limit_bytes=VMEM_LIMIT_BYTES)


def _split3(x):
    hi = x.astype(BF16).astype(F32)
    rem = x - hi
    mid = rem.astype(BF16).astype(F32)
    return hi, mid, rem - mid


def _proj_kernel(h_ref, g_ref, wm_ref, wf_ref, main_ref, f_ref):
    xn = _rms_norm_rows(h_ref[...], g_ref[...]).astype(BF16)
    for c in range(N_MAIN // COL_CHUNK):
        cs = slice(c * COL_CHUNK, (c + 1) * COL_CHUNK)
        main_ref[:, cs] = _dot(xn, wm_ref[:, cs]).astype(BF16)
    f_ref[...] = _dot(xn, wf_ref[...])


def _proj(h, g, w_in_all, layer, w_f):
    t = h.shape[0]
    return pl.pallas_call(
        _proj_kernel,
        grid=(t // TOKEN_TILE,),
        in_specs=[pl.BlockSpec((TOKEN_TILE, D_MODEL), lambda i: (i, 0)),
                  _resident((1, D_MODEL)),
                  _resident_layer((D_MODEL, N_MAIN), layer),
                  _resident((D_MODEL, LANES))],
        out_specs=[pl.BlockSpec((TOKEN_TILE, N_MAIN), lambda i: (i, 0)),
                   pl.BlockSpec((TOKEN_TILE, LANES), lambda i: (i, 0))],
        out_shape=[jax.ShapeDtypeStruct((t, N_MAIN), BF16),
                   jax.ShapeDtypeStruct((t, LANES), F32)],
        compiler_params=_params(1),
        name="proj",
    )(h, g, w_in_all, w_f)


def _convgmlp_kernel(cb_ref, cc_ref, cx_ref, u_ref, v_ref, wconv_ref, ws_ref, bs_ref,
                     lng_ref, lnb_ref, o_ref):
    seq = cb_ref.shape[0]
    n_chunks = seq // GMLP_CHUNK
    row = lax.broadcasted_iota(jnp.int32, (GMLP_CHUNK, GMLP_CHUNK), 0)
    col = lax.broadcasted_iota(jnp.int32, (GMLP_CHUNK, GMLP_CHUNK), 1)
    w_tril = [jnp.where(col <= row, ws_ref[gi], 0.0).astype(BF16) for gi in range(GMLP_GROUPS)]
    lane = lax.broadcasted_iota(jnp.int32, (GMLP_CHUNK, LANES), 1)
    first_group = lane < GROUP_WIDTH
    w0 = wconv_ref[0:1, :]
    w1 = wconv_ref[1:2, :]
    w2 = wconv_ref[2:3, :]

    def chunk(c, prev_tail):
        rows = pl.ds(pl.multiple_of(c * GMLP_CHUNK, GMLP_CHUNK), GMLP_CHUNK)
        xc = cc_ref[rows, :].astype(F32) * cx_ref[rows, :].astype(F32)
        win = jnp.concatenate([prev_tail, xc], axis=0)
        xc1 = pltpu.roll(win, 1, 0)[SUBLANES:, :]
        xc2 = pltpu.roll(win, 2, 0)[SUBLANES:, :]
        ya = cb_ref[rows, :].astype(F32) * (w0 * xc2 + w1 * xc1 + w2 * xc)
        o_ref[rows, 0:MIX_WIDTH] = ya.astype(BF16)

        gu = jax.nn.gelu(u_ref[rows, :].astype(F32))
        gv = jax.nn.gelu(v_ref[rows, :].astype(F32))
        mu = jnp.mean(gv, axis=-1, keepdims=True)
        cen = gv - mu
        var = jnp.mean(cen * cen, axis=-1, keepdims=True)
        vn = (cen * lax.rsqrt(var + EPS) * lng_ref[...] + lnb_ref[...]).astype(BF16)
        halves = []
        for lb in range(MIX_WIDTH // LANES):
            vb = vn[:, lb * LANES:(lb + 1) * LANES]
            m0 = _dot(w_tril[2 * lb], vb)
            m1 = _dot(w_tril[2 * lb + 1], vb)
            halves.append(jnp.where(first_group, m0, m1))
        mixed = jnp.concatenate(halves, axis=1) + bs_ref[...]
        o_ref[rows, MIX_WIDTH:2 * MIX_WIDTH] = (gu * mixed).astype(BF16)
        return xc[GMLP_CHUNK - SUBLANES:, :]

    lax.fori_loop(0, n_chunks // 2, lambda c2, tail: chunk(2 * c2 + 1, chunk(2 * c2, tail)),
                  jnp.zeros((SUBLANES, MIX_WIDTH), F32))


def _convgmlp(main3, w_conv, w_s, bs_rows, ln_g, ln_b):
    b, s, _ = main3.shape
    col = lambda k: pl.BlockSpec((None, s, MIX_WIDTH), lambda i, k=k: (i, 0, k))
    return pl.pallas_call(
        _convgmlp_kernel,
        grid=(b,),
        in_specs=[col(COL_CONV_B), col(COL_CONV_C), col(COL_CONV_X), col(COL_GMLP_U), col(COL_GMLP_V),
                  _resident((CONV_K, MIX_WIDTH)),
                  _resident((GMLP_GROUPS, GMLP_CHUNK, GMLP_CHUNK)),
                  _resident((GMLP_CHUNK, MIX_WIDTH)),
                  _resident((1, MIX_WIDTH)),
                  _resident((1, MIX_WIDTH))],
        out_specs=pl.BlockSpec((None, s, 2 * MIX_WIDTH), lambda i: (i, 0, 0)),
        out_shape=jax.ShapeDtypeStruct((b, s, 2 * MIX_WIDTH), BF16),
        compiler_params=_params(1),
        name="convgmlp",
    )(main3, main3, main3, main3, main3, w_conv, w_s, bs_rows, ln_g, ln_b)


def _pair(x, h):
    p = (h * HEAD_DIM) // LANES
    return x[:, p * LANES:(p + 1) * LANES]


def _head_in_pair_mask(rows, h):
    lane = lax.broadcasted_iota(jnp.int32, (rows, LANES), 1)
    first = lane < HEAD_DIM
    return first if (h * HEAD_DIM) % LANES == 0 else jnp.logical_not(first)


def _masked_head(x, h):
    xp = _pair(x, h)
    return jnp.where(_head_in_pair_mask(x.shape[0], h), xp, jnp.zeros_like(xp))


def _merge_heads(per_head):
    rows = per_head[0].shape[0]
    blocks = []
    for p in range(MIX_WIDTH // LANES):
        h0 = p * (LANES // HEAD_DIM)
        blocks.append(jnp.where(_head_in_pair_mask(rows, h0), per_head[h0], per_head[h0 + 1]))
    return jnp.concatenate(blocks, axis=1)


def _sb_kernel(q_ref, k_ref, v_ref, o_ref, carry_scr, acc_scr):
    step_id = pl.program_id(1)
    r = lax.broadcasted_iota(jnp.int32, (ATTN_TK, ATTN_TK), 0)
    c = lax.broadcasted_iota(jnp.int32, (ATTN_TK, ATTN_TK), 1)
    suffix = jnp.where(r > c, 1.0, 0.0).astype(BF16)
    rows_all = N_HEADS * ATTN_TQ
    rq = lax.broadcasted_iota(jnp.int32, (rows_all, ATTN_TK), 0) & (ATTN_TQ - 1)
    cq = lax.broadcasted_iota(jnp.int32, (rows_all, ATTN_TK), 1)
    strict = cq < rq
    pair_rows = HEADS_PER_PAIR * ATTN_TQ

    def make_group(u, i):
        q = q_ref[pl.ds(pl.multiple_of(i * ATTN_TQ, ATTN_TQ), ATTN_TQ), :]
        qneg = [jnp.concatenate([_masked_head(q, HEADS_PER_PAIR * p + e) for e in range(HEADS_PER_PAIR)],
                                axis=0) * (-SCALE) for p in range(N_PAIRS)]

        def group(blocks, first):
            carry = jnp.zeros((rows_all, 1), F32) if first else carry_scr[u]
            staged = []
            for j, valid, diag in blocks:
                ks = pl.ds(pl.multiple_of(j * ATTN_TK, ATTN_TK), ATTN_TK)
                k = k_ref[ks, :]
                zn = jnp.concatenate(
                    [_dot_nt(qneg[p], k[:, p * LANES:(p + 1) * LANES]) for p in range(N_PAIRS)], axis=0)
                zn2 = zn * LOG2_E
                l2 = jnp.minimum(zn2, 0.0) - jnp.log2(1.0 + jnp.exp2(-jnp.abs(zn2)))
                if diag:
                    l2 = jnp.where(strict, l2, 0.0)
                staged.append((ks, valid, diag, l2 - zn2, _dot(l2.astype(BF16), suffix),
                               jnp.sum(l2, axis=1, keepdims=True)))
            pv = None
            for ks, valid, diag, log2_beta, later_in, row_sum in staged:
                if valid is None:
                    w = jnp.exp2(later_in + carry + log2_beta)
                else:
                    w = jnp.exp2(later_in + (carry + jnp.where(valid, 0.0, NEG_BIG)) + log2_beta)
                    row_sum = jnp.where(valid, row_sum, 0.0)
                if diag:
                    w = jnp.where(strict, w, 0.0)
                wb = w.astype(BF16)
                v = v_ref[ks, :]
                contrib = jnp.concatenate(
                    [_dot(wb[p * pair_rows:(p + 1) * pair_rows, :], v[:, p * LANES:(p + 1) * LANES])
                     for p in range(N_PAIRS)], axis=0)
                pv = contrib if pv is None else pv + contrib
                carry = carry + row_sum
            if first:
                acc_scr[u] = pv
            else:
                acc_scr[u] += pv
            carry_scr[u] = carry

        return group

    query_blocks = [(u, SB_QBLOCKS * step_id + u) for u in range(SB_QBLOCKS)]
    groups = [make_group(u, i) for u, i in query_blocks]
    for (u, i), group in zip(query_blocks, groups):
        group([(i, None, True), (jnp.maximum(i - 1, 0), i >= 1, False)], True)

    for (u, i), group in zip(query_blocks, groups):
        def alive(u=u):
            return jnp.max(carry_scr[u]) > LOG2_F32_UNDERFLOW

        first_left = i - 2
        n_groups = i // 2

        def step(st, group=group, alive=alive, first_left=first_left):
            t, _ = st
            j_near = first_left - 2 * t
            group([(j_near, None, False), (jnp.maximum(j_near - 1, 0), j_near >= 1, False)], False)
            return t + 1, alive()

        lax.while_loop(lambda st, n_groups=n_groups: (st[0] < n_groups) & st[1], step, (jnp.int32(0), alive()))
        acc = acc_scr[u]
        o_ref[u * ATTN_TQ:(u + 1) * ATTN_TQ, :] = _merge_heads(
            [acc[h * ATTN_TQ:(h + 1) * ATTN_TQ, :] for h in range(N_HEADS)]).astype(BF16)


def _sb_attention(main3):
    b, s, _ = main3.shape
    step_rows = SB_QBLOCKS * ATTN_TQ
    col = lambda k: pl.BlockSpec((None, s, MIX_WIDTH), lambda bi, i, k=k: (bi, 0, k))
    return pl.pallas_call(
        _sb_kernel,
        grid=(b, s // step_rows),
        in_specs=[col(COL_SB_Q), col(COL_SB_K), col(COL_SB_V)],
        out_specs=pl.BlockSpec((None, step_rows, MIX_WIDTH), lambda bi, i: (bi, i, 0)),
        out_shape=jax.ShapeDtypeStruct((b, s, MIX_WIDTH), BF16),
        scratch_shapes=[pltpu.VMEM((SB_QBLOCKS, N_HEADS * ATTN_TQ, 1), F32),
                        pltpu.VMEM((SB_QBLOCKS, N_HEADS * ATTN_TQ, LANES), F32)],
        compiler_params=_params(2),
        name="sb",
    )(main3, main3, main3)


def _fox_kernel(q_ref, k_ref, v_ref, f_ref, bf_ref, qg_ref, kg_ref, o_ref,
                qn_scr, kn_scr, qaug_scr, kaug_scr, s_scr, mrun_scr, lrun_scr, acc_scr):
    step_id = pl.program_id(1)
    seq = q_ref.shape[0]

    @pl.when(step_id == 0)
    def _prepare():
        r = lax.broadcasted_iota(jnp.int32, (MIX_WIDTH, MIX_WIDTH), 0) // HEAD_DIM
        c = lax.broadcasted_iota(jnp.int32, (MIX_WIDTH, MIX_WIDTH), 1) // HEAD_DIM
        same_head = jnp.where(r == c, 1.0, 0.0).astype(BF16)
        tr = lax.broadcasted_iota(jnp.int32, (FOX_PREP_ROWS, FOX_PREP_ROWS), 0)
        tc = lax.broadcasted_iota(jnp.int32, (FOX_PREP_ROWS, FOX_PREP_ROWS), 1)
        prefix = jnp.where(tc <= tr, 1.0, 0.0).astype(BF16)
        lane = lax.broadcasted_iota(jnp.int32, (FOX_PREP_ROWS, LANES), 1)
        n = N_HEADS

        def tile(tix, run):
            rows = pl.ds(pl.multiple_of(tix * FOX_PREP_ROWS, FOX_PREP_ROWS), FOX_PREP_ROWS)
            for x_ref, g_ref, scale, dst in ((q_ref, qg_ref, SCALE, qn_scr), (k_ref, kg_ref, 1.0, kn_scr)):
                x = x_ref[rows, :].astype(F32)
                hi, lo = _split_hi_lo(x * x)
                ms = (_dot(hi, same_head) + _dot(lo, same_head)) * (1.0 / HEAD_DIM)
                dst[rows, :] = (x * lax.rsqrt(ms + EPS) * (g_ref[...] * scale)).astype(BF16)
            hi, lo = _split_hi_lo(jax.nn.log_sigmoid(f_ref[rows, :] + bf_ref[...]))
            cum = run + _dot(prefix, hi) + _dot(prefix, lo)
            c_hi, c_mid, c_lo = _split3(cum)
            q_aug = jnp.where(lane < n, c_hi, jnp.where(lane < 2 * n, c_mid, jnp.where(
                lane < 3 * n, c_lo, jnp.where(lane < 6 * n, 1.0, 0.0))))
            k_aug = jnp.where(lane < 3 * n, 1.0, jnp.where(lane < 4 * n, -c_hi, jnp.where(
                lane < 5 * n, -c_mid, jnp.where(lane < 6 * n, -c_lo, 0.0))))
            qaug_scr[rows, :] = q_aug.astype(BF16)
            kaug_scr[rows, :] = k_aug.astype(BF16)
            return cum[FOX_PREP_ROWS - 1:FOX_PREP_ROWS, :]

        lax.fori_loop(0, seq // FOX_PREP_ROWS, tile, jnp.zeros((1, LANES), F32))

    rows_all = N_HEADS * ATTN_TQ
    pair_rows = HEADS_PER_PAIR * ATTN_TQ
    lane = lax.broadcasted_iota(jnp.int32, (ATTN_TQ, LANES), 1)
    r = lax.broadcasted_iota(jnp.int32, (rows_all, ATTN_TK), 0) & (ATTN_TQ - 1)
    c = lax.broadcasted_iota(jnp.int32, (rows_all, ATTN_TK), 1)
    causal = c <= r
    query_blocks = [(u, FOX_QBLOCKS * step_id + u) for u in range(FOX_QBLOCKS)]

    def key_rows(j):
        return pl.ds(pl.multiple_of(j * ATTN_TK, ATTN_TK), ATTN_TK)

    def fold(x):
        return [x[:, n * LANES:(n + 1) * LANES] for n in range(ATTN_TK // LANES)]

    def make_scores(i):
        qrows = key_rows(i)
        q = qn_scr[qrows, :]
        qaug = qaug_scr[qrows, :]
        qs = []
        for p in range(N_PAIRS):
            rows = []
            for e in range(HEADS_PER_PAIR):
                h = HEADS_PER_PAIR * p + e
                aug_h = jnp.where((lane & (N_HEADS - 1)) == h, qaug, jnp.zeros_like(qaug))
                rows.append(jnp.concatenate([_masked_head(q, h), aug_h], axis=1))
            qs.append(jnp.concatenate(rows, axis=0))

        def scores(j):
            ks = key_rows(j)
            kaug = kaug_scr[ks, :]
            return jnp.concatenate(
                [_dot_nt(qs[p], jnp.concatenate([kn_scr[ks, p * LANES:(p + 1) * LANES], kaug], axis=1))
                 for p in range(N_PAIRS)], axis=0) * LOG2_E

        return scores

    def accumulate(blocks):
        l_parts, pv = [], None
        for j, shifted in blocks:
            ks = key_rows(j)
            p = jnp.exp2(shifted)
            l_parts += fold(p)
            pb = p.astype(BF16)
            contrib = jnp.concatenate(
                [_dot(pb[pr * pair_rows:(pr + 1) * pair_rows, :], v_ref[ks, pr * LANES:(pr + 1) * LANES])
                 for pr in range(N_PAIRS)], axis=0)
            pv = contrib if pv is None else pv + contrib
        return functools.reduce(jnp.add, l_parts), pv

    assert FOX_QBLOCKS == 2
    i0 = FOX_QBLOCKS * step_id
    head_blocks = [[i0], [i0 + 1, i0]]
    score_fns = [make_scores(i) for _, i in query_blocks]

    for (u, i), scores in zip(query_blocks, score_fns):
        parts = []
        for n, j in enumerate(head_blocks[u]):
            s = jnp.where(causal, scores(j), NEG_BIG) if n == 0 else scores(j)
            s_scr[u, j] = s
            parts += fold(s)
        mrun_scr[u] = functools.reduce(jnp.maximum, parts)

    def stage(t, _):
        for (u, _), scores in zip(query_blocks, score_fns):
            parts = []
            for j in (2 * t, 2 * t + 1):
                s = scores(j)
                s_scr[u, j] = s
                parts += fold(s)
            mrun_scr[u] = functools.reduce(jnp.maximum, parts, mrun_scr[u])
        return 0

    lax.fori_loop(0, step_id, stage, 0)

    row_max = []
    for u, _ in query_blocks:
        m = jnp.max(mrun_scr[u], axis=1, keepdims=True)
        l, pv = accumulate([(j, s_scr[u, j] - m) for j in head_blocks[u]])
        lrun_scr[u] = l
        acc_scr[u] = pv
        row_max.append(m)

    def weigh(t, _):
        for (u, _), m in zip(query_blocks, row_max):
            l, pv = accumulate([(j, s_scr[u, j] - m) for j in (2 * t, 2 * t + 1)])
            lrun_scr[u] += l
            acc_scr[u] += pv
        return 0

    lax.fori_loop(0, step_id, weigh, 0)

    for u, _ in query_blocks:
        out = acc_scr[u] / jnp.sum(lrun_scr[u], axis=1, keepdims=True)
        o_ref[u * ATTN_TQ:(u + 1) * ATTN_TQ, :] = _merge_heads(
            [out[h * ATTN_TQ:(h + 1) * ATTN_TQ, :] for h in range(N_HEADS)]).astype(BF16)


def _fox_attention(main3, f3, bf_row, qg_row, kg_row):
    b, s, _ = main3.shape
    rows_all = N_HEADS * ATTN_TQ
    step_rows = FOX_QBLOCKS * ATTN_TQ
    col = lambda k: pl.BlockSpec((None, s, MIX_WIDTH), lambda bi, i, k=k: (bi, 0, k))
    return pl.pallas_call(
        _fox_kernel,
        grid=(b, s // step_rows),
        in_specs=[col(COL_FOX_Q), col(COL_FOX_K), col(COL_FOX_V),
                  pl.BlockSpec((None, s, LANES), lambda bi, i: (bi, 0, 0)),
                  _resident((1, LANES)),
                  _resident((1, MIX_WIDTH)),
                  _resident((1, MIX_WIDTH))],
        out_specs=pl.BlockSpec((None, step_rows, MIX_WIDTH), lambda bi, i: (bi, i, 0)),
        out_shape=jax.ShapeDtypeStruct((b, s, MIX_WIDTH), BF16),
        scratch_shapes=[pltpu.VMEM((s, MIX_WIDTH), BF16),
                        pltpu.VMEM((s, MIX_WIDTH), BF16),
                        pltpu.VMEM((s, LANES), BF16),
                        pltpu.VMEM((s, LANES), BF16),
                        pltpu.VMEM((FOX_QBLOCKS, s // ATTN_TK, rows_all, ATTN_TK), F32),
                        pltpu.VMEM((FOX_QBLOCKS, rows_all, LANES), F32),
                        pltpu.VMEM((FOX_QBLOCKS, rows_all, LANES), F32),
                        pltpu.VMEM((FOX_QBLOCKS, rows_all, LANES), F32)],
        compiler_params=_params(2),
        name="fox",
    )(main3, main3, main3, f3, bf_row, qg_row, kg_row)


def _merge_kernel(h_ref, g_ref, yab_ref, yc_ref, yd_ref, wg_ref, wb_ref, wo_ref, o_ref, merged_scr):
    xn = _rms_norm_rows(h_ref[...], g_ref[...]).astype(BF16)
    ys = [yab_ref[:, 0:MIX_WIDTH], yab_ref[:, MIX_WIDTH:2 * MIX_WIDTH], yc_ref[...], yd_ref[...]]
    for c in range(D_MODEL // COL_CHUNK):
        cs = slice(c * COL_CHUNK, (c + 1) * COL_CHUNK)
        acc = None
        for n in range(N_BRANCH):
            gs = slice(n * D_MODEL + c * COL_CHUNK, n * D_MODEL + (c + 1) * COL_CHUNK)
            term = jax.nn.sigmoid(_dot(xn, wg_ref[:, gs])) * _dot(ys[n], wb_ref[n, :, cs])
            acc = term if acc is None else acc + term
        merged_scr[:, cs] = acc.astype(BF16)
    merged = merged_scr[...]
    for c in range(D_MODEL // COL_CHUNK):
        cs = slice(c * COL_CHUNK, (c + 1) * COL_CHUNK)
        o_ref[:, cs] = h_ref[:, cs] + _dot(merged, wo_ref[:, cs])


def _merge(h, g, yab, yc, yd, w_gate, w_branch, w_out, layer):
    t = h.shape[0]
    rows = lambda w: pl.BlockSpec((TOKEN_TILE, w), lambda i: (i, 0))
    return pl.pallas_call(
        _merge_kernel,
        grid=(t // TOKEN_TILE,),
        in_specs=[rows(D_MODEL), _resident((1, D_MODEL)),
                  rows(2 * MIX_WIDTH), rows(MIX_WIDTH), rows(MIX_WIDTH),
                  _resident_layer((D_MODEL, N_BRANCH * D_MODEL), layer),
                  _resident_layer((N_BRANCH, MIX_WIDTH, D_MODEL), layer),
                  _resident_layer((D_MODEL, D_MODEL), layer)],
        out_specs=rows(D_MODEL),
        out_shape=jax.ShapeDtypeStruct((t, D_MODEL), F32),
        scratch_shapes=[pltpu.VMEM((TOKEN_TILE, D_MODEL), BF16)],
        compiler_params=_params(1),
        name="merge",
    )(h, g, yab, yc, yd, w_gate, w_branch, w_out)


def _ffn_kernel(h_ref, g_ref, wi_ref, wo_ref, o_ref, acc_scr):
    xn = _rms_norm_rows(h_ref[...], g_ref[...]).astype(BF16)
    for c in range(FFN_HIDDEN // COL_CHUNK):
        gate = _dot(xn, wi_ref[:, c * COL_CHUNK:(c + 1) * COL_CHUNK])
        up = _dot(xn, wi_ref[:, FFN_HIDDEN + c * COL_CHUNK:FFN_HIDDEN + (c + 1) * COL_CHUNK])
        act = (jax.nn.silu(gate) * up).astype(BF16)
        part = _dot(act, wo_ref[c * COL_CHUNK:(c + 1) * COL_CHUNK, :])
        if c == 0:
            acc_scr[...] = part
        else:
            acc_scr[...] += part
    o_ref[...] = h_ref[...] + acc_scr[...]


def _ffn(h, g, w_in, w_out, layer):
    t = h.shape[0]
    rows = pl.BlockSpec((TOKEN_TILE, D_MODEL), lambda i: (i, 0))
    return pl.pallas_call(
        _ffn_kernel,
        grid=(t // TOKEN_TILE,),
        in_specs=[rows, _resident((1, D_MODEL)),
                  _resident_layer((D_MODEL, 2 * FFN_HIDDEN), layer),
                  _resident_layer((FFN_HIDDEN, D_MODEL), layer)],
        out_specs=rows,
        out_shape=jax.ShapeDtypeStruct((t, D_MODEL), F32),
        scratch_shapes=[pltpu.VMEM((TOKEN_TILE, D_MODEL), F32)],
        compiler_params=_params(1),
        name="ffn",
    )(h, g, w_in, w_out)


def kernel(x, norm_mix_g, w_in, w_conv, w_spatial, b_spatial, gmlp_ln_g, gmlp_ln_b,
           fox_q_norm_g, fox_k_norm_g, fox_forget_b, w_branch, w_out, norm_ffn_g,
           w_ffn_in, w_ffn_out):
    b, s, d = x.shape
    depth = w_in.shape[0]
    assert d == D_MODEL and ATTN_TQ == ATTN_TK
    for rows in (SB_QBLOCKS * ATTN_TQ, FOX_QBLOCKS * ATTN_TQ, 2 * GMLP_CHUNK, FOX_PREP_ROWS, TOKEN_TILE):
        assert s % rows == 0, (s, rows)
    t = b * s
    h = x.reshape(t, d)
    w_in_b = w_in.astype(BF16)
    w_gate_b = w_in_b[:, :, N_MAIN + N_HEADS:]
    w_branch_b = w_branch.astype(BF16)
    w_out_b = w_out.astype(BF16)
    w_ffn_in_b = w_ffn_in.astype(BF16)
    w_ffn_out_b = w_ffn_out.astype(BF16)
    n_f = FORGET_COPIES * N_HEADS
    for l in range(depth):
        w_f = jnp.pad(jnp.tile(w_in_b[l, :, N_MAIN:N_MAIN + N_HEADS], (1, FORGET_COPIES)),
                      ((0, 0), (0, LANES - n_f)))
        bf_row = jnp.pad(jnp.tile(fox_forget_b[l], FORGET_COPIES), (0, LANES - n_f)).reshape(1, LANES)
        qg_row = jnp.tile(fox_q_norm_g[l], N_HEADS).reshape(1, MIX_WIDTH)
        kg_row = jnp.tile(fox_k_norm_g[l], N_HEADS).reshape(1, MIX_WIDTH)
        bs_rows = jnp.repeat(b_spatial[l].T, GROUP_WIDTH, axis=1)

        main, f_raw = _proj(h, norm_mix_g[l].reshape(1, d), w_in_b, l, w_f)
        main3 = main.reshape(b, s, N_MAIN)
        yab = _convgmlp(main3, w_conv[l], w_spatial[l], bs_rows,
                        gmlp_ln_g[l].reshape(1, MIX_WIDTH), gmlp_ln_b[l].reshape(1, MIX_WIDTH))
        yc = _sb_attention(main3)
        yd = _fox_attention(main3, f_raw.reshape(b, s, LANES), bf_row, qg_row, kg_row)
        h = _merge(h, norm_mix_g[l].reshape(1, d), yab.reshape(t, 2 * MIX_WIDTH),
                   yc.reshape(t, MIX_WIDTH), yd.reshape(t, MIX_WIDTH),
                   w_gate_b, w_branch_b, w_out_b, l)
        h = _ffn(h, norm_ffn_g[l].reshape(1, d), w_ffn_in_b, w_ffn_out_b, l)
    return h.reshape(b, s, d)
```

```python
import functools

import jax
import jax.numpy as jnp
from jax import lax
from jax.experimental import pallas as pl
from jax.experimental.pallas import tpu as pltpu

D_MODEL = 1024
MIX_WIDTH = 256
HEAD_DIM = 64
N_HEADS = MIX_WIDTH // HEAD_DIM
N_BRANCH = 4
CONV_K = 3
GMLP_GROUPS = 4
GMLP_CHUNK = 128
GROUP_WIDTH = MIX_WIDTH // GMLP_GROUPS
FFN_HIDDEN = 2816
EPS = 1e-6
N_MAIN = 11 * MIX_WIDTH
(COL_CONV_B, COL_CONV_C, COL_CONV_X, COL_GMLP_U, COL_GMLP_V, COL_SB_Q, COL_SB_K, COL_SB_V,
 COL_FOX_Q, COL_FOX_K, COL_FOX_V) = range(11)
LANES = 128
HEADS_PER_PAIR = LANES // HEAD_DIM
N_PAIRS = MIX_WIDTH // LANES
FORGET_COPIES = 6
FOX_PREP_ROWS = 256
SUBLANES = 8
VMEM_LIMIT_BYTES = 56 * 1024 * 1024

TOKEN_TILE = 1024
COL_CHUNK = 256
ATTN_TQ = 256
ATTN_TK = 256
SB_QBLOCKS = 8
FOX_QBLOCKS = 2
SCALE = HEAD_DIM ** -0.5
LOG2_E = 1.4426950408889634
SB_Q_SCALE = -SCALE * LOG2_E
LOG2_F32_UNDERFLOW = -151.0
NEG_BIG = -1e30

F32 = jnp.float32
BF16 = jnp.bfloat16


def _dot(a, b):
    return jnp.dot(a, b, preferred_element_type=F32)


def _dot_nt(a, b):
    return lax.dot_general(a, b, (((1,), (1,)), ((), ())), preferred_element_type=F32)


def _split_hi_lo(x):
    hi = x.astype(BF16)
    lo = (x - hi.astype(F32)).astype(BF16)
    return hi, lo


def _rms_norm_rows(x, g):
    ms = jnp.mean(x * x, axis=-1, keepdims=True)
    return x * lax.rsqrt(ms + EPS) * g


def _resident(shape):
    return pl.BlockSpec(shape, lambda *_: (0,) * len(shape), pipeline_mode=pl.Buffered(1))


def _resident_layer(shape, layer):
    return pl.BlockSpec((None,) + tuple(shape), lambda *_: (layer,) + (0,) * len(shape),
                        pipeline_mode=pl.Buffered(1))


def _params(n_axes):
    return pltpu.CompilerParams(dimension_semantics=("arbitrary",) * n_axes,
                                vmem_limit_bytes=VMEM_LIMIT_BYTES)


def _split3(x):
    hi = x.astype(BF16).astype(F32)
    rem = x - hi
    mid = rem.astype(BF16).astype(F32)
    return hi, mid, rem - mid


def _proj_kernel(h_ref, g_ref, wm_ref, wf_ref, main_ref, f_ref):
    xn = _rms_norm_rows(h_ref[...], g_ref[...]).astype(BF16)
    for c in range(N_MAIN // COL_CHUNK):
        cs = slice(c * COL_CHUNK, (c + 1) * COL_CHUNK)
        y = _dot(xn, wm_ref[:, cs])
        if c == COL_SB_Q:
            y = y * SB_Q_SCALE
        main_ref[:, cs] = y.astype(BF16)
    f_ref[...] = _dot(xn, wf_ref[...])


def _proj(h, g, w_in_all, layer, w_f):
    t = h.shape[0]
    return pl.pallas_call(
        _proj_kernel,
        grid=(t // TOKEN_TILE,),
        in_specs=[pl.BlockSpec((TOKEN_TILE, D_MODEL), lambda i: (i, 0)),
                  _resident((1, D_MODEL)),
                  _resident_layer((D_MODEL, N_MAIN), layer),
                  _resident((D_MODEL, LANES))],
        out_specs=[pl.BlockSpec((TOKEN_TILE, N_MAIN), lambda i: (i, 0)),
                   pl.BlockSpec((TOKEN_TILE, LANES), lambda i: (i, 0))],
        out_shape=[jax.ShapeDtypeStruct((t, N_MAIN), BF16),
                   jax.ShapeDtypeStruct((t, LANES), F32)],
        compiler_params=_params(1),
        name="proj",
    )(h, g, w_in_all, w_f)


def _convgmlp_kernel(cb_ref, cc_ref, cx_ref, u_ref, v_ref, wconv_ref, ws_ref, bs_ref,
                     lng_ref, lnb_ref, o_ref):
    seq = cb_ref.shape[0]
    n_chunks = seq // GMLP_CHUNK
    row = lax.broadcasted_iota(jnp.int32, (GMLP_CHUNK, GMLP_CHUNK), 0)
    col = lax.broadcasted_iota(jnp.int32, (GMLP_CHUNK, GMLP_CHUNK), 1)
    w_tril = [jnp.where(col <= row, ws_ref[gi], 0.0).astype(BF16) for gi in range(GMLP_GROUPS)]
    lane = lax.broadcasted_iota(jnp.int32, (GMLP_CHUNK, LANES), 1)
    first_group = lane < GROUP_WIDTH
    w0 = wconv_ref[0:1, :]
    w1 = wconv_ref[1:2, :]
    w2 = wconv_ref[2:3, :]

    def chunk(c, prev_tail):
        rows = pl.ds(pl.multiple_of(c * GMLP_CHUNK, GMLP_CHUNK), GMLP_CHUNK)
        xc = cc_ref[rows, :].astype(F32) * cx_ref[rows, :].astype(F32)
        win = jnp.concatenate([prev_tail, xc], axis=0)
        xc1 = pltpu.roll(win, 1, 0)[SUBLANES:, :]
        xc2 = pltpu.roll(win, 2, 0)[SUBLANES:, :]
        ya = cb_ref[rows, :].astype(F32) * (w0 * xc2 + w1 * xc1 + w2 * xc)
        o_ref[rows, 0:MIX_WIDTH] = ya.astype(BF16)

        gu = jax.nn.gelu(u_ref[rows, :].astype(F32))
        gv = jax.nn.gelu(v_ref[rows, :].astype(F32))
        mu = jnp.mean(gv, axis=-1, keepdims=True)
        cen = gv - mu
        var = jnp.mean(cen * cen, axis=-1, keepdims=True)
        vn = (cen * lax.rsqrt(var + EPS) * lng_ref[...] + lnb_ref[...]).astype(BF16)
        halves = []
        for lb in range(MIX_WIDTH // LANES):
            vb = vn[:, lb * LANES:(lb + 1) * LANES]
            m0 = _dot(w_tril[2 * lb], vb)
            m1 = _dot(w_tril[2 * lb + 1], vb)
            halves.append(jnp.where(first_group, m0, m1))
        mixed = jnp.concatenate(halves, axis=1) + bs_ref[...]
        o_ref[rows, MIX_WIDTH:2 * MIX_WIDTH] = (gu * mixed).astype(BF16)
        return xc[GMLP_CHUNK - SUBLANES:, :]

    lax.fori_loop(0, n_chunks // 2, lambda c2, tail: chunk(2 * c2 + 1, chunk(2 * c2, tail)),
                  jnp.zeros((SUBLANES, MIX_WIDTH), F32))


def _convgmlp(main3, w_conv, w_s, bs_rows, ln_g, ln_b):
    b, s, _ = main3.shape
    col = lambda k: pl.BlockSpec((None, s, MIX_WIDTH), lambda i, k=k: (i, 0, k))
    return pl.pallas_call(
        _convgmlp_kernel,
        grid=(b,),
        in_specs=[col(COL_CONV_B), col(COL_CONV_C), col(COL_CONV_X), col(COL_GMLP_U), col(COL_GMLP_V),
                  _resident((CONV_K, MIX_WIDTH)),
                  _resident((GMLP_GROUPS, GMLP_CHUNK, GMLP_CHUNK)),
                  _resident((GMLP_CHUNK, MIX_WIDTH)),
                  _resident((1, MIX_WIDTH)),
                  _resident((1, MIX_WIDTH))],
        out_specs=pl.BlockSpec((None, s, 2 * MIX_WIDTH), lambda i: (i, 0, 0)),
        out_shape=jax.ShapeDtypeStruct((b, s, 2 * MIX_WIDTH), BF16),
        compiler_params=_params(1),
        name="convgmlp",
    )(main3, main3, main3, main3, main3, w_conv, w_s, bs_rows, ln_g, ln_b)


def _pair(x, h):
    p = (h * HEAD_DIM) // LANES
    return x[:, p * LANES:(p + 1) * LANES]


def _head_in_pair_mask(rows, h):
    lane = lax.broadcasted_iota(jnp.int32, (rows, LANES), 1)
    first = lane < HEAD_DIM
    return first if (h * HEAD_DIM) % LANES == 0 else jnp.logical_not(first)


def _masked_head(x, h):
    xp = _pair(x, h)
    return jnp.where(_head_in_pair_mask(x.shape[0], h), xp, jnp.zeros_like(xp))


def _merge_heads(per_head):
    rows = per_head[0].shape[0]
    blocks = []
    for p in range(MIX_WIDTH // LANES):
        h0 = p * (LANES // HEAD_DIM)
        blocks.append(jnp.where(_head_in_pair_mask(rows, h0), per_head[h0], per_head[h0 + 1]))
    return jnp.concatenate(blocks, axis=1)


def _sb_kernel(q_ref, k_ref, v_ref, o_ref, carry_scr, acc_scr):
    step_id = pl.program_id(1)
    r = lax.broadcasted_iota(jnp.int32, (ATTN_TK, ATTN_TK), 0)
    c = lax.broadcasted_iota(jnp.int32, (ATTN_TK, ATTN_TK), 1)
    suffix = jnp.where(r > c, 1.0, 0.0).astype(BF16)
    rows_all = N_HEADS * ATTN_TQ
    rq = lax.broadcasted_iota(jnp.int32, (rows_all, ATTN_TK), 0) & (ATTN_TQ - 1)
    cq = lax.broadcasted_iota(jnp.int32, (rows_all, ATTN_TK), 1)
    strict = cq < rq
    pair_rows = HEADS_PER_PAIR * ATTN_TQ

    def make_group(u, i):
        q = q_ref[pl.ds(pl.multiple_of(i * ATTN_TQ, ATTN_TQ), ATTN_TQ), :]
        qneg = [jnp.concatenate([_masked_head(q, HEADS_PER_PAIR * p + e) for e in range(HEADS_PER_PAIR)],
                                axis=0) for p in range(N_PAIRS)]

        def group(blocks, first):
            carry = jnp.zeros((rows_all, 1), F32) if first else carry_scr[u]
            staged = []
            for j, valid, diag in blocks:
                ks = pl.ds(pl.multiple_of(j * ATTN_TK, ATTN_TK), ATTN_TK)
                k = k_ref[ks, :]
                zn2 = jnp.concatenate(
                    [_dot_nt(qneg[p], k[:, p * LANES:(p + 1) * LANES]) for p in range(N_PAIRS)], axis=0)
                l2 = jnp.minimum(zn2, 0.0) - jnp.log2(1.0 + jnp.exp2(-jnp.abs(zn2)))
                if diag:
                    l2 = jnp.where(strict, l2, 0.0)
                staged.append((ks, valid, diag, l2 - zn2, _dot(l2.astype(BF16), suffix),
                               jnp.sum(l2, axis=1, keepdims=True)))
            pv = None
            for ks, valid, diag, log2_beta, later_in, row_sum in staged:
                if valid is None:
                    w = jnp.exp2(later_in + carry + log2_beta)
                else:
                    w = jnp.exp2(later_in + (carry + jnp.where(valid, 0.0, NEG_BIG)) + log2_beta)
                    row_sum = jnp.where(valid, row_sum, 0.0)
                if diag:
                    w = jnp.where(strict, w, 0.0)
                wb = w.astype(BF16)
                v = v_ref[ks, :]
                contrib = jnp.concatenate(
                    [_dot(wb[p * pair_rows:(p + 1) * pair_rows, :], v[:, p * LANES:(p + 1) * LANES])
                     for p in range(N_PAIRS)], axis=0)
                pv = contrib if pv is None else pv + contrib
                carry = carry + row_sum
            if first:
                acc_scr[u] = pv
            else:
                acc_scr[u] += pv
            carry_scr[u] = carry

        return group

    query_blocks = [(u, SB_QBLOCKS * step_id + u) for u in range(SB_QBLOCKS)]
    groups = [make_group(u, i) for u, i in query_blocks]
    for (u, i), group in zip(query_blocks, groups):
        group([(i, None, True), (jnp.maximum(i - 1, 0), i >= 1, False)], True)

    for (u, i), group in zip(query_blocks, groups):
        def alive(u=u):
            return jnp.max(carry_scr[u]) > LOG2_F32_UNDERFLOW

        first_left = i - 2
        n_groups = i // 2

        def step(st, group=group, alive=alive, first_left=first_left):
            t, _ = st
            j_near = first_left - 2 * t
            group([(j_near, None, False), (jnp.maximum(j_near - 1, 0), j_near >= 1, False)], False)
            return t + 1, alive()

        lax.while_loop(lambda st, n_groups=n_groups: (st[0] < n_groups) & st[1], step, (jnp.int32(0), alive()))
        acc = acc_scr[u]
        o_ref[u * ATTN_TQ:(u + 1) * ATTN_TQ, :] = _merge_heads(
            [acc[h * ATTN_TQ:(h + 1) * ATTN_TQ, :] for h in range(N_HEADS)]).astype(BF16)


def _sb_attention(main3):
    b, s, _ = main3.shape
    step_rows = SB_QBLOCKS * ATTN_TQ
    col = lambda k: pl.BlockSpec((None, s, MIX_WIDTH), lambda bi, i, k=k: (bi, 0, k))
    return pl.pallas_call(
        _sb_kernel,
        grid=(b, s // step_rows),
        in_specs=[col(COL_SB_Q), col(COL_SB_K), col(COL_SB_V)],
        out_specs=pl.BlockSpec((None, step_rows, MIX_WIDTH), lambda bi, i: (bi, i, 0)),
        out_shape=jax.ShapeDtypeStruct((b, s, MIX_WIDTH), BF16),
        scratch_shapes=[pltpu.VMEM((SB_QBLOCKS, N_HEADS * ATTN_TQ, 1), F32),
                        pltpu.VMEM((SB_QBLOCKS, N_HEADS * ATTN_TQ, LANES), F32)],
        compiler_params=_params(2),
        name="sb",
    )(main3, main3, main3)


def _fox_kernel(q_ref, k_ref, v_ref, f_ref, bf_ref, qg_ref, kg_ref, o_ref,
                qn_scr, kn_scr, qaug_scr, kaug_scr, s_scr, mrun_scr, lrun_scr, acc_scr):
    step_id = pl.program_id(1)
    seq = q_ref.shape[0]

    @pl.when(step_id == 0)
    def _prepare():
        r = lax.broadcasted_iota(jnp.int32, (MIX_WIDTH, MIX_WIDTH), 0) // HEAD_DIM
        c = lax.broadcasted_iota(jnp.int32, (MIX_WIDTH, MIX_WIDTH), 1) // HEAD_DIM
        same_head = jnp.where(r == c, 1.0, 0.0).astype(BF16)
        tr = lax.broadcasted_iota(jnp.int32, (FOX_PREP_ROWS, FOX_PREP_ROWS), 0)
        tc = lax.broadcasted_iota(jnp.int32, (FOX_PREP_ROWS, FOX_PREP_ROWS), 1)
        prefix = jnp.where(tc <= tr, 1.0, 0.0).astype(BF16)
        lane = lax.broadcasted_iota(jnp.int32, (FOX_PREP_ROWS, LANES), 1)
        n = N_HEADS

        def tile(tix, run):
            rows = pl.ds(pl.multiple_of(tix * FOX_PREP_ROWS, FOX_PREP_ROWS), FOX_PREP_ROWS)
            for x_ref, g_ref, scale, dst in ((q_ref, qg_ref, SCALE, qn_scr), (k_ref, kg_ref, 1.0, kn_scr)):
                x = x_ref[rows, :].astype(F32)
                hi, lo = _split_hi_lo(x * x)
                ms = (_dot(hi, same_head) + _dot(lo, same_head)) * (1.0 / HEAD_DIM)
                dst[rows, :] = (x * lax.rsqrt(ms + EPS) * (g_ref[...] * scale)).astype(BF16)
            hi, lo = _split_hi_lo(jax.nn.log_sigmoid(f_ref[rows, :] + bf_ref[...]))
            cum = run + _dot(prefix, hi) + _dot(prefix, lo)
            c_hi, c_mid, c_lo = _split3(cum)
            q_aug = jnp.where(lane < n, c_hi, jnp.where(lane < 2 * n, c_mid, jnp.where(
                lane < 3 * n, c_lo, jnp.where(lane < 6 * n, 1.0, 0.0))))
            k_aug = jnp.where(lane < 3 * n, 1.0, jnp.where(lane < 4 * n, -c_hi, jnp.where(
                lane < 5 * n, -c_mid, jnp.where(lane < 6 * n, -c_lo, 0.0))))
            qaug_scr[rows, :] = q_aug.astype(BF16)
            kaug_scr[rows, :] = k_aug.astype(BF16)
            return cum[FOX_PREP_ROWS - 1:FOX_PREP_ROWS, :]

        lax.fori_loop(0, seq // FOX_PREP_ROWS, tile, jnp.zeros((1, LANES), F32))

    rows_all = N_HEADS * ATTN_TQ
    pair_rows = HEADS_PER_PAIR * ATTN_TQ
    lane = lax.broadcasted_iota(jnp.int32, (ATTN_TQ, LANES), 1)
    r = lax.broadcasted_iota(jnp.int32, (rows_all, ATTN_TK), 0) & (ATTN_TQ - 1)
    c = lax.broadcasted_iota(jnp.int32, (rows_all, ATTN_TK), 1)
    causal = c <= r
    query_blocks = [(u, FOX_QBLOCKS * step_id + u) for u in range(FOX_QBLOCKS)]

    def key_rows(j):
        return pl.ds(pl.multiple_of(j * ATTN_TK, ATTN_TK), ATTN_TK)

    def fold(x):
        return [x[:, n * LANES:(n + 1) * LANES] for n in range(ATTN_TK // LANES)]

    def make_scores(i):
        qrows = key_rows(i)
        q = qn_scr[qrows, :]
        qaug = qaug_scr[qrows, :]
        qs = []
        for p in range(N_PAIRS):
            rows = []
            for e in range(HEADS_PER_PAIR):
                h = HEADS_PER_PAIR * p + e
                aug_h = jnp.where((lane & (N_HEADS - 1)) == h, qaug, jnp.zeros_like(qaug))
                rows.append(jnp.concatenate([_masked_head(q, h), aug_h], axis=1))
            qs.append(jnp.concatenate(rows, axis=0))

        def scores(j):
            ks = key_rows(j)
            kaug = kaug_scr[ks, :]
            return jnp.concatenate(
                [_dot_nt(qs[p], jnp.concatenate([kn_scr[ks, p * LANES:(p + 1) * LANES], kaug], axis=1))
                 for p in range(N_PAIRS)], axis=0) * LOG2_E

        return scores

    def accumulate(blocks):
        l_parts, pv = [], None
        for j, shifted in blocks:
            ks = key_rows(j)
            p = jnp.exp2(shifted)
            l_parts += fold(p)
            pb = p.astype(BF16)
            contrib = jnp.concatenate(
                [_dot(pb[pr * pair_rows:(pr + 1) * pair_rows, :], v_ref[ks, pr * LANES:(pr + 1) * LANES])
                 for pr in range(N_PAIRS)], axis=0)
            pv = contrib if pv is None else pv + contrib
        return functools.reduce(jnp.add, l_parts), pv

    assert FOX_QBLOCKS == 2
    i0 = FOX_QBLOCKS * step_id
    head_blocks = [[i0], [i0 + 1, i0]]
    score_fns = [make_scores(i) for _, i in query_blocks]

    for (u, i), scores in zip(query_blocks, score_fns):
        parts = []
        for n, j in enumerate(head_blocks[u]):
            s = jnp.where(causal, scores(j), NEG_BIG) if n == 0 else scores(j)
            s_scr[u, j] = s
            parts += fold(s)
        mrun_scr[u] = functools.reduce(jnp.maximum, parts)

    def stage(t, _):
        for (u, _), scores in zip(query_blocks, score_fns):
            parts = []
            for j in (2 * t, 2 * t + 1):
                s = scores(j)
                s_scr[u, j] = s
                parts += fold(s)
            mrun_scr[u] = functools.reduce(jnp.maximum, parts, mrun_scr[u])
        return 0

    lax.fori_loop(0, step_id, stage, 0)

    row_max = []
    for u, _ in query_blocks:
        m = jnp.max(mrun_scr[u], axis=1, keepdims=True)
        l, pv = accumulate([(j, s_scr[u, j] - m) for j in head_blocks[u]])
        lrun_scr[u] = l
        acc_scr[u] = pv
        row_max.append(m)

    def weigh(t, _):
        for (u, _), m in zip(query_blocks, row_max):
            l, pv = accumulate([(j, s_scr[u, j] - m) for j in (2 * t, 2 * t + 1)])
            lrun_scr[u] += l
            acc_scr[u] += pv
        return 0

    lax.fori_loop(0, step_id, weigh, 0)

    for u, _ in query_blocks:
        out = acc_scr[u] / jnp.sum(lrun_scr[u], axis=1, keepdims=True)
        o_ref[u * ATTN_TQ:(u + 1) * ATTN_TQ, :] = _merge_heads(
            [out[h * ATTN_TQ:(h + 1) * ATTN_TQ, :] for h in range(N_HEADS)]).astype(BF16)


def _fox_attention(main3, f3, bf_row, qg_row, kg_row):
    b, s, _ = main3.shape
    rows_all = N_HEADS * ATTN_TQ
    step_rows = FOX_QBLOCKS * ATTN_TQ
    col = lambda k: pl.BlockSpec((None, s, MIX_WIDTH), lambda bi, i, k=k: (bi, 0, k))
    return pl.pallas_call(
        _fox_kernel,
        grid=(b, s // step_rows),
        in_specs=[col(COL_FOX_Q), col(COL_FOX_K), col(COL_FOX_V),
                  pl.BlockSpec((None, s, LANES), lambda bi, i: (bi, 0, 0)),
                  _resident((1, LANES)),
                  _resident((1, MIX_WIDTH)),
                  _resident((1, MIX_WIDTH))],
        out_specs=pl.BlockSpec((None, step_rows, MIX_WIDTH), lambda bi, i: (bi, i, 0)),
        out_shape=jax.ShapeDtypeStruct((b, s, MIX_WIDTH), BF16),
        scratch_shapes=[pltpu.VMEM((s, MIX_WIDTH), BF16),
                        pltpu.VMEM((s, MIX_WIDTH), BF16),
                        pltpu.VMEM((s, LANES), BF16),
                        pltpu.VMEM((s, LANES), BF16),
                        pltpu.VMEM((FOX_QBLOCKS, s // ATTN_TK, rows_all, ATTN_TK), F32),
                        pltpu.VMEM((FOX_QBLOCKS, rows_all, LANES), F32),
                        pltpu.VMEM((FOX_QBLOCKS, rows_all, LANES), F32),
                        pltpu.VMEM((FOX_QBLOCKS, rows_all, LANES), F32)],
        compiler_params=_params(2),
        name="fox",
    )(main3, main3, main3, f3, bf_row, qg_row, kg_row)


def _merge_kernel(h_ref, g_ref, yab_ref, yc_ref, yd_ref, wg_ref, wb_ref, wo_ref, o_ref, merged_scr):
    xn = _rms_norm_rows(h_ref[...], g_ref[...]).astype(BF16)
    ys = [yab_ref[:, 0:MIX_WIDTH], yab_ref[:, MIX_WIDTH:2 * MIX_WIDTH], yc_ref[...], yd_ref[...]]
    for c in range(D_MODEL // COL_CHUNK):
        cs = slice(c * COL_CHUNK, (c + 1) * COL_CHUNK)
        acc = None
        for n in range(N_BRANCH):
            gs = slice(n * D_MODEL + c * COL_CHUNK, n * D_MODEL + (c + 1) * COL_CHUNK)
            term = jax.nn.sigmoid(_dot(xn, wg_ref[:, gs])) * _dot(ys[n], wb_ref[n, :, cs])
            acc = term if acc is None else acc + term
        merged_scr[:, cs] = acc.astype(BF16)
    merged = merged_scr[...]
    for c in range(D_MODEL // COL_CHUNK):
        cs = slice(c * COL_CHUNK, (c + 1) * COL_CHUNK)
        o_ref[:, cs] = h_ref[:, cs] + _dot(merged, wo_ref[:, cs])


def _merge(h, g, yab, yc, yd, w_gate, w_branch, w_out, layer):
    t = h.shape[0]
    rows = lambda w: pl.BlockSpec((TOKEN_TILE, w), lambda i: (i, 0))
    return pl.pallas_call(
        _merge_kernel,
        grid=(t // TOKEN_TILE,),
        in_specs=[rows(D_MODEL), _resident((1, D_MODEL)),
                  rows(2 * MIX_WIDTH), rows(MIX_WIDTH), rows(MIX_WIDTH),
                  _resident_layer((D_MODEL, N_BRANCH * D_MODEL), layer),
                  _resident_layer((N_BRANCH, MIX_WIDTH, D_MODEL), layer),
                  _resident_layer((D_MODEL, D_MODEL), layer)],
        out_specs=rows(D_MODEL),
        out_shape=jax.ShapeDtypeStruct((t, D_MODEL), F32),
        scratch_shapes=[pltpu.VMEM((TOKEN_TILE, D_MODEL), BF16)],
        compiler_params=_params(1),
        name="merge",
    )(h, g, yab, yc, yd, w_gate, w_branch, w_out)


def _ffn_kernel(h_ref, g_ref, wi_ref, wo_ref, o_ref, acc_scr):
    xn = _rms_norm_rows(h_ref[...], g_ref[...]).astype(BF16)
    for c in range(FFN_HIDDEN // COL_CHUNK):
        gate = _dot(xn, wi_ref[:, c * COL_CHUNK:(c + 1) * COL_CHUNK])
        up = _dot(xn, wi_ref[:, FFN_HIDDEN + c * COL_CHUNK:FFN_HIDDEN + (c + 1) * COL_CHUNK])
        act = (jax.nn.silu(gate) * up).astype(BF16)
        part = _dot(act, wo_ref[c * COL_CHUNK:(c + 1) * COL_CHUNK, :])
        if c == 0:
            acc_scr[...] = part
        else:
            acc_scr[...] += part
    o_ref[...] = h_ref[...] + acc_scr[...]


def _ffn(h, g, w_in, w_out, layer):
    t = h.shape[0]
    rows = pl.BlockSpec((TOKEN_TILE, D_MODEL), lambda i: (i, 0))
    return pl.pallas_call(
        _ffn_kernel,
        grid=(t // TOKEN_TILE,),
        in_specs=[rows, _resident((1, D_MODEL)),
                  _resident_layer((D_MODEL, 2 * FFN_HIDDEN), layer),
                  _resident_layer((FFN_HIDDEN, D_MODEL), layer)],
        out_specs=rows,
        out_shape=jax.ShapeDtypeStruct((t, D_MODEL), F32),
        scratch_shapes=[pltpu.VMEM((TOKEN_TILE, D_MODEL), F32)],
        compiler_params=_params(1),
        name="ffn",
    )(h, g, w_in, w_out)


def kernel(x, norm_mix_g, w_in, w_conv, w_spatial, b_spatial, gmlp_ln_g, gmlp_ln_b,
           fox_q_norm_g, fox_k_norm_g, fox_forget_b, w_branch, w_out, norm_ffn_g,
           w_ffn_in, w_ffn_out):
    b, s, d = x.shape
    depth = w_in.shape[0]
    assert d == D_MODEL and ATTN_TQ == ATTN_TK
    for rows in (SB_QBLOCKS * ATTN_TQ, FOX_QBLOCKS * ATTN_TQ, 2 * GMLP_CHUNK, FOX_PREP_ROWS, TOKEN_TILE):
        assert s % rows == 0, (s, rows)
    t = b * s
    h = x.reshape(t, d)
    w_in_b = jnp.transpose(lax.optimization_barrier(jnp.transpose(w_in, (2, 0, 1)).astype(BF16)), (1, 2, 0))
    w_gate_b = w_in_b[:, :, N_MAIN + N_HEADS:]
    w_branch_b = w_branch.astype(BF16)
    w_out_b = w_out.astype(BF16)
    w_ffn_in_b = w_ffn_in.astype(BF16)
    w_ffn_out_b = w_ffn_out.astype(BF16)
    n_f = FORGET_COPIES * N_HEADS
    for l in range(depth):
        w_f = jnp.pad(jnp.tile(w_in_b[l, :, N_MAIN:N_MAIN + N_HEADS], (1, FORGET_COPIES)),
                      ((0, 0), (0, LANES - n_f)))
        bf_row = jnp.pad(jnp.tile(fox_forget_b[l], FORGET_COPIES), (0, LANES - n_f)).reshape(1, LANES)
        qg_row = jnp.tile(fox_q_norm_g[l], N_HEADS).reshape(1, MIX_WIDTH)
        kg_row = jnp.tile(fox_k_norm_g[l], N_HEADS).reshape(1, MIX_WIDTH)
        bs_rows = jnp.repeat(b_spatial[l].T, GROUP_WIDTH, axis=1)

        main, f_raw = _proj(h, norm_mix_g[l].reshape(1, d), w_in_b, l, w_f)
        main3 = main.reshape(b, s, N_MAIN)
        yab = _convgmlp(main3, w_conv[l], w_spatial[l], bs_rows,
                        gmlp_ln_g[l].reshape(1, MIX_WIDTH), gmlp_ln_b[l].reshape(1, MIX_WIDTH))
        yc = _sb_attention(main3)
        yd = _fox_attention(main3, f_raw.reshape(b, s, LANES), bf_row, qg_row, kg_row)
        h = _merge(h, norm_mix_g[l].reshape(1, d), yab.reshape(t, 2 * MIX_WIDTH),
                   yc.reshape(t, MIX_WIDTH), yd.reshape(t, MIX_WIDTH),
                   w_gate_b, w_branch_b, w_out_b, l)
        h = _ffn(h, norm_ffn_g[l].reshape(1, d), w_ffn_in_b, w_ffn_out_b, l)
    return h.reshape(b, s, d)
```

```python
import functools

import jax
import jax.numpy as jnp
from jax import lax
from jax.experimental import pallas as pl
from jax.experimental.pallas import tpu as pltpu

D_MODEL = 1024
MIX_WIDTH = 256
HEAD_DIM = 64
N_HEADS = MIX_WIDTH // HEAD_DIM
N_BRANCH = 4
CONV_K = 3
GMLP_GROUPS = 4
GMLP_CHUNK = 128
GROUP_WIDTH = MIX_WIDTH // GMLP_GROUPS
FFN_HIDDEN = 2816
EPS = 1e-6
N_MAIN = 11 * MIX_WIDTH
(COL_CONV_B, COL_CONV_C, COL_CONV_X, COL_GMLP_U, COL_GMLP_V, COL_SB_Q, COL_SB_K, COL_SB_V,
 COL_FOX_Q, COL_FOX_K, COL_FOX_V) = range(11)
LANES = 128
HEADS_PER_PAIR = LANES // HEAD_DIM
N_PAIRS = MIX_WIDTH // LANES
FORGET_COPIES = 6
FOX_PREP_ROWS = 256
SUBLANES = 8
VMEM_LIMIT_BYTES = 56 * 1024 * 1024

TOKEN_TILE = 1024
COL_CHUNK = 256
ATTN_TQ = 256
ATTN_TK = 256
SB_QBLOCKS = 8
FOX_QBLOCKS = 4
SCALE = HEAD_DIM ** -0.5
LOG2_E = 1.4426950408889634
SB_Q_SCALE = -SCALE * LOG2_E
LOG2_F32_UNDERFLOW = -151.0
NEG_BIG = -1e30

F32 = jnp.float32
BF16 = jnp.bfloat16


def _dot(a, b):
    return jnp.dot(a, b, preferred_element_type=F32)


def _dot_nt(a, b):
    return lax.dot_general(a, b, (((1,), (1,)), ((), ())), preferred_element_type=F32)


def _split_hi_lo(x):
    hi = x.astype(BF16)
    lo = (x - hi.astype(F32)).astype(BF16)
    return hi, lo


def _rms_norm_rows(x, g):
    ms = jnp.mean(x * x, axis=-1, keepdims=True)
    return x * lax.rsqrt(ms + EPS) * g


def _resident(shape):
    return pl.BlockSpec(shape, lambda *_: (0,) * len(shape), pipeline_mode=pl.Buffered(1))


def _resident_layer(shape, layer):
    return pl.BlockSpec((None,) + tuple(shape), lambda *_: (layer,) + (0,) * len(shape),
                        pipeline_mode=pl.Buffered(1))


def _params(n_axes):
    return pltpu.CompilerParams(dimension_semantics=("arbitrary",) * n_axes,
                                vmem_limit_bytes=VMEM_LIMIT_BYTES)


def _split3(x):
    hi = x.astype(BF16).astype(F32)
    rem = x - hi
    mid = rem.astype(BF16).astype(F32)
    return hi, mid, rem - mid


def _proj_kernel(h_ref, g_ref, wm_ref, wf_ref, main_ref, f_ref):
    xn = _rms_norm_rows(h_ref[...], g_ref[...]).astype(BF16)
    for c in range(N_MAIN // COL_CHUNK):
        cs = slice(c * COL_CHUNK, (c + 1) * COL_CHUNK)
        y = _dot(xn, wm_ref[:, cs])
        if c == COL_SB_Q:
            y = y * SB_Q_SCALE
        main_ref[:, cs] = y.astype(BF16)
    f_ref[...] = _dot(xn, wf_ref[...])


def _proj(h, g, w_main_all, layer, w_f):
    t = h.shape[0]
    return pl.pallas_call(
        _proj_kernel,
        grid=(t // TOKEN_TILE,),
        in_specs=[pl.BlockSpec((TOKEN_TILE, D_MODEL), lambda i: (i, 0)),
                  _resident((1, D_MODEL)),
                  _resident_layer((D_MODEL, N_MAIN), layer),
                  _resident((D_MODEL, LANES))],
        out_specs=[pl.BlockSpec((TOKEN_TILE, N_MAIN), lambda i: (i, 0)),
                   pl.BlockSpec((TOKEN_TILE, LANES), lambda i: (i, 0))],
        out_shape=[jax.ShapeDtypeStruct((t, N_MAIN), BF16),
                   jax.ShapeDtypeStruct((t, LANES), F32)],
        compiler_params=_params(1),
        name="proj",
    )(h, g, w_main_all, w_f)


def _convgmlp_kernel(cb_ref, cc_ref, cx_ref, u_ref, v_ref, wconv_ref, ws_ref, bs_ref,
                     lng_ref, lnb_ref, o_ref):
    seq = cb_ref.shape[0]
    n_chunks = seq // GMLP_CHUNK
    row = lax.broadcasted_iota(jnp.int32, (GMLP_CHUNK, GMLP_CHUNK), 0)
    col = lax.broadcasted_iota(jnp.int32, (GMLP_CHUNK, GMLP_CHUNK), 1)
    w_tril = [jnp.where(col <= row, ws_ref[gi], 0.0).astype(BF16) for gi in range(GMLP_GROUPS)]
    lane = lax.broadcasted_iota(jnp.int32, (GMLP_CHUNK, LANES), 1)
    first_group = lane < GROUP_WIDTH
    w0 = wconv_ref[0:1, :]
    w1 = wconv_ref[1:2, :]
    w2 = wconv_ref[2:3, :]

    def chunk(c, prev_tail):
        rows = pl.ds(pl.multiple_of(c * GMLP_CHUNK, GMLP_CHUNK), GMLP_CHUNK)
        xc = cc_ref[rows, :].astype(F32) * cx_ref[rows, :].astype(F32)
        win = jnp.concatenate([prev_tail, xc], axis=0)
        xc1 = pltpu.roll(win, 1, 0)[SUBLANES:, :]
        xc2 = pltpu.roll(win, 2, 0)[SUBLANES:, :]
        ya = cb_ref[rows, :].astype(F32) * (w0 * xc2 + w1 * xc1 + w2 * xc)
        o_ref[rows, 0:MIX_WIDTH] = ya.astype(BF16)

        gu = jax.nn.gelu(u_ref[rows, :].astype(F32))
        gv = jax.nn.gelu(v_ref[rows, :].astype(F32))
        mu = jnp.mean(gv, axis=-1, keepdims=True)
        cen = gv - mu
        var = jnp.mean(cen * cen, axis=-1, keepdims=True)
        vn = (cen * lax.rsqrt(var + EPS) * lng_ref[...] + lnb_ref[...]).astype(BF16)
        halves = []
        for lb in range(MIX_WIDTH // LANES):
            vb = vn[:, lb * LANES:(lb + 1) * LANES]
            m0 = _dot(w_tril[2 * lb], vb)
            m1 = _dot(w_tril[2 * lb + 1], vb)
            halves.append(jnp.where(first_group, m0, m1))
        mixed = jnp.concatenate(halves, axis=1) + bs_ref[...]
        o_ref[rows, MIX_WIDTH:2 * MIX_WIDTH] = (gu * mixed).astype(BF16)
        return xc[GMLP_CHUNK - SUBLANES:, :]

    lax.fori_loop(0, n_chunks // 2, lambda c2, tail: chunk(2 * c2 + 1, chunk(2 * c2, tail)),
                  jnp.zeros((SUBLANES, MIX_WIDTH), F32))


def _convgmlp(main3, w_conv, w_s, bs_rows, ln_g, ln_b):
    b, s, _ = main3.shape
    col = lambda k: pl.BlockSpec((None, s, MIX_WIDTH), lambda i, k=k: (i, 0, k))
    return pl.pallas_call(
        _convgmlp_kernel,
        grid=(b,),
        in_specs=[col(COL_CONV_B), col(COL_CONV_C), col(COL_CONV_X), col(COL_GMLP_U), col(COL_GMLP_V),
                  _resident((CONV_K, MIX_WIDTH)),
                  _resident((GMLP_GROUPS, GMLP_CHUNK, GMLP_CHUNK)),
                  _resident((GMLP_CHUNK, MIX_WIDTH)),
                  _resident((1, MIX_WIDTH)),
                  _resident((1, MIX_WIDTH))],
        out_specs=pl.BlockSpec((None, s, 2 * MIX_WIDTH), lambda i: (i, 0, 0)),
        out_shape=jax.ShapeDtypeStruct((b, s, 2 * MIX_WIDTH), BF16),
        compiler_params=_params(1),
        name="convgmlp",
    )(main3, main3, main3, main3, main3, w_conv, w_s, bs_rows, ln_g, ln_b)


def _pair(x, h):
    p = (h * HEAD_DIM) // LANES
    return x[:, p * LANES:(p + 1) * LANES]


def _head_in_pair_mask(rows, h):
    lane = lax.broadcasted_iota(jnp.int32, (rows, LANES), 1)
    first = lane < HEAD_DIM
    return first if (h * HEAD_DIM) % LANES == 0 else jnp.logical_not(first)


def _masked_head(x, h):
    xp = _pair(x, h)
    return jnp.where(_head_in_pair_mask(x.shape[0], h), xp, jnp.zeros_like(xp))


def _merge_heads(per_head):
    rows = per_head[0].shape[0]
    blocks = []
    for p in range(MIX_WIDTH // LANES):
        h0 = p * (LANES // HEAD_DIM)
        blocks.append(jnp.where(_head_in_pair_mask(rows, h0), per_head[h0], per_head[h0 + 1]))
    return jnp.concatenate(blocks, axis=1)


def _sb_kernel(q_ref, k_ref, v_ref, o_ref, carry_scr, acc_scr):
    step_id = pl.program_id(1)
    r = lax.broadcasted_iota(jnp.int32, (ATTN_TK, ATTN_TK), 0)
    c = lax.broadcasted_iota(jnp.int32, (ATTN_TK, ATTN_TK), 1)
    suffix = jnp.where(r > c, 1.0, 0.0).astype(BF16)
    rows_all = N_HEADS * ATTN_TQ
    rq = lax.broadcasted_iota(jnp.int32, (rows_all, ATTN_TK), 0) & (ATTN_TQ - 1)
    cq = lax.broadcasted_iota(jnp.int32, (rows_all, ATTN_TK), 1)
    strict = cq < rq
    pair_rows = HEADS_PER_PAIR * ATTN_TQ

    def make_group(u, i):
        q = q_ref[pl.ds(pl.multiple_of(i * ATTN_TQ, ATTN_TQ), ATTN_TQ), :]
        qneg = [jnp.concatenate([_masked_head(q, HEADS_PER_PAIR * p + e) for e in range(HEADS_PER_PAIR)],
                                axis=0) for p in range(N_PAIRS)]

        def group(blocks, first):
            carry = jnp.zeros((rows_all, 1), F32) if first else carry_scr[u]
            staged = []
            for j, valid, diag in blocks:
                ks = pl.ds(pl.multiple_of(j * ATTN_TK, ATTN_TK), ATTN_TK)
                k = k_ref[ks, :]
                zn2 = jnp.concatenate(
                    [_dot_nt(qneg[p], k[:, p * LANES:(p + 1) * LANES]) for p in range(N_PAIRS)], axis=0)
                l2 = jnp.minimum(zn2, 0.0) - jnp.log2(1.0 + jnp.exp2(-jnp.abs(zn2)))
                if diag:
                    l2 = jnp.where(strict, l2, 0.0)
                staged.append((ks, valid, diag, l2 - zn2, _dot(l2.astype(BF16), suffix),
                               jnp.sum(l2, axis=1, keepdims=True)))
            pv = None
            for ks, valid, diag, log2_beta, later_in, row_sum in staged:
                if valid is None:
                    w = jnp.exp2(later_in + carry + log2_beta)
                else:
                    w = jnp.exp2(later_in + (carry + jnp.where(valid, 0.0, NEG_BIG)) + log2_beta)
                    row_sum = jnp.where(valid, row_sum, 0.0)
                if diag:
                    w = jnp.where(strict, w, 0.0)
                wb = w.astype(BF16)
                v = v_ref[ks, :]
                contrib = jnp.concatenate(
                    [_dot(wb[p * pair_rows:(p + 1) * pair_rows, :], v[:, p * LANES:(p + 1) * LANES])
                     for p in range(N_PAIRS)], axis=0)
                pv = contrib if pv is None else pv + contrib
                carry = carry + row_sum
            if first:
                acc_scr[u] = pv
            else:
                acc_scr[u] += pv
            carry_scr[u] = carry

        return group

    query_blocks = [(u, SB_QBLOCKS * step_id + u) for u in range(SB_QBLOCKS)]
    groups = [make_group(u, i) for u, i in query_blocks]
    for (u, i), group in zip(query_blocks, groups):
        group([(i, None, True), (jnp.maximum(i - 1, 0), i >= 1, False)], True)

    for (u, i), group in zip(query_blocks, groups):
        def alive(u=u):
            return jnp.max(carry_scr[u]) > LOG2_F32_UNDERFLOW

        first_left = i - 2
        n_groups = i // 2

        def step(st, group=group, alive=alive, first_left=first_left):
            t, _ = st
            j_near = first_left - 2 * t
            group([(j_near, None, False), (jnp.maximum(j_near - 1, 0), j_near >= 1, False)], False)
            return t + 1, alive()

        lax.while_loop(lambda st, n_groups=n_groups: (st[0] < n_groups) & st[1], step, (jnp.int32(0), alive()))
        acc = acc_scr[u]
        o_ref[u * ATTN_TQ:(u + 1) * ATTN_TQ, :] = _merge_heads(
            [acc[h * ATTN_TQ:(h + 1) * ATTN_TQ, :] for h in range(N_HEADS)]).astype(BF16)


def _sb_attention(main3):
    b, s, _ = main3.shape
    step_rows = SB_QBLOCKS * ATTN_TQ
    col = lambda k: pl.BlockSpec((None, s, MIX_WIDTH), lambda bi, i, k=k: (bi, 0, k))
    return pl.pallas_call(
        _sb_kernel,
        grid=(b, s // step_rows),
        in_specs=[col(COL_SB_Q), col(COL_SB_K), col(COL_SB_V)],
        out_specs=pl.BlockSpec((None, step_rows, MIX_WIDTH), lambda bi, i: (bi, i, 0)),
        out_shape=jax.ShapeDtypeStruct((b, s, MIX_WIDTH), BF16),
        scratch_shapes=[pltpu.VMEM((SB_QBLOCKS, N_HEADS * ATTN_TQ, 1), F32),
                        pltpu.VMEM((SB_QBLOCKS, N_HEADS * ATTN_TQ, LANES), F32)],
        compiler_params=_params(2),
        name="sb",
    )(main3, main3, main3)


def _fox_kernel(q_ref, k_ref, v_ref, f_ref, bf_ref, qg_ref, kg_ref, o_ref,
                qn_scr, kn_scr, qaug_scr, kaug_scr, s_scr, mrun_scr, lrun_scr, acc_scr):
    step_id = pl.program_id(1)
    seq = q_ref.shape[0]

    @pl.when(step_id == 0)
    def _prepare():
        r = lax.broadcasted_iota(jnp.int32, (MIX_WIDTH, MIX_WIDTH), 0) // HEAD_DIM
        c = lax.broadcasted_iota(jnp.int32, (MIX_WIDTH, MIX_WIDTH), 1) // HEAD_DIM
        same_head = jnp.where(r == c, 1.0, 0.0).astype(BF16)
        tr = lax.broadcasted_iota(jnp.int32, (FOX_PREP_ROWS, FOX_PREP_ROWS), 0)
        tc = lax.broadcasted_iota(jnp.int32, (FOX_PREP_ROWS, FOX_PREP_ROWS), 1)
        prefix = jnp.where(tc <= tr, 1.0, 0.0).astype(BF16)
        lane = lax.broadcasted_iota(jnp.int32, (FOX_PREP_ROWS, LANES), 1)
        n = N_HEADS

        def tile(tix, run):
            rows = pl.ds(pl.multiple_of(tix * FOX_PREP_ROWS, FOX_PREP_ROWS), FOX_PREP_ROWS)
            for x_ref, g_ref, scale, dst in ((q_ref, qg_ref, SCALE, qn_scr), (k_ref, kg_ref, 1.0, kn_scr)):
                x = x_ref[rows, :].astype(F32)
                hi, lo = _split_hi_lo(x * x)
                ms = (_dot(hi, same_head) + _dot(lo, same_head)) * (1.0 / HEAD_DIM)
                dst[rows, :] = (x * lax.rsqrt(ms + EPS) * (g_ref[...] * scale)).astype(BF16)
            hi, lo = _split_hi_lo(jax.nn.log_sigmoid(f_ref[rows, :] + bf_ref[...]))
            cum = run + _dot(prefix, hi) + _dot(prefix, lo)
            c_hi, c_mid, c_lo = _split3(cum)
            q_aug = jnp.where(lane < n, c_hi, jnp.where(lane < 2 * n, c_mid, jnp.where(
                lane < 3 * n, c_lo, jnp.where(lane < 6 * n, 1.0, 0.0))))
            k_aug = jnp.where(lane < 3 * n, 1.0, jnp.where(lane < 4 * n, -c_hi, jnp.where(
                lane < 5 * n, -c_mid, jnp.where(lane < 6 * n, -c_lo, 0.0))))
            qaug_scr[rows, :] = q_aug.astype(BF16)
            kaug_scr[rows, :] = k_aug.astype(BF16)
            return cum[FOX_PREP_ROWS - 1:FOX_PREP_ROWS, :]

        lax.fori_loop(0, seq // FOX_PREP_ROWS, tile, jnp.zeros((1, LANES), F32))

    rows_all = N_HEADS * ATTN_TQ
    pair_rows = HEADS_PER_PAIR * ATTN_TQ
    lane = lax.broadcasted_iota(jnp.int32, (ATTN_TQ, LANES), 1)
    r = lax.broadcasted_iota(jnp.int32, (rows_all, ATTN_TK), 0) & (ATTN_TQ - 1)
    c = lax.broadcasted_iota(jnp.int32, (rows_all, ATTN_TK), 1)
    causal = c <= r
    query_blocks = [(u, FOX_QBLOCKS * step_id + u) for u in range(FOX_QBLOCKS)]

    def key_rows(j):
        return pl.ds(pl.multiple_of(j * ATTN_TK, ATTN_TK), ATTN_TK)

    def fold(x):
        return [x[:, n * LANES:(n + 1) * LANES] for n in range(ATTN_TK // LANES)]

    def make_scores(i):
        qrows = key_rows(i)
        q = qn_scr[qrows, :]
        qaug = qaug_scr[qrows, :]
        qs = []
        for p in range(N_PAIRS):
            rows = []
            for e in range(HEADS_PER_PAIR):
                h = HEADS_PER_PAIR * p + e
                aug_h = jnp.where((lane & (N_HEADS - 1)) == h, qaug, jnp.zeros_like(qaug))
                rows.append(jnp.concatenate([_masked_head(q, h), aug_h], axis=1))
            qs.append(jnp.concatenate(rows, axis=0))

        def scores(j):
            ks = key_rows(j)
            kaug = kaug_scr[ks, :]
            return jnp.concatenate(
                [_dot_nt(qs[p], jnp.concatenate([kn_scr[ks, p * LANES:(p + 1) * LANES], kaug], axis=1))
                 for p in range(N_PAIRS)], axis=0) * LOG2_E

        return scores

    def accumulate(blocks):
        l_parts, pv = [], None
        for j, shifted in blocks:
            ks = key_rows(j)
            p = jnp.exp2(shifted)
            l_parts += fold(p)
            pb = p.astype(BF16)
            contrib = jnp.concatenate(
                [_dot(pb[pr * pair_rows:(pr + 1) * pair_rows, :], v_ref[ks, pr * LANES:(pr + 1) * LANES])
                 for pr in range(N_PAIRS)], axis=0)
            pv = contrib if pv is None else pv + contrib
        return functools.reduce(jnp.add, l_parts), pv

    assert FOX_QBLOCKS % 2 == 0
    i0 = FOX_QBLOCKS * step_id
    head_blocks = [[i0 + v for v in range(u, -1, -1)] for u in range(FOX_QBLOCKS)]
    n_pair_steps = (FOX_QBLOCKS // 2) * step_id
    score_fns = [make_scores(i) for _, i in query_blocks]

    for (u, i), scores in zip(query_blocks, score_fns):
        parts = []
        for n, j in enumerate(head_blocks[u]):
            s = jnp.where(causal, scores(j), NEG_BIG) if n == 0 else scores(j)
            s_scr[u, j] = s
            parts += fold(s)
        mrun_scr[u] = functools.reduce(jnp.maximum, parts)

    def stage(t, _):
        for (u, _), scores in zip(query_blocks, score_fns):
            parts = []
            for j in (2 * t, 2 * t + 1):
                s = scores(j)
                s_scr[u, j] = s
                parts += fold(s)
            mrun_scr[u] = functools.reduce(jnp.maximum, parts, mrun_scr[u])
        return 0

    lax.fori_loop(0, n_pair_steps, stage, 0)

    row_max = []
    for u, _ in query_blocks:
        m = jnp.max(mrun_scr[u], axis=1, keepdims=True)
        l, pv = accumulate([(j, s_scr[u, j] - m) for j in head_blocks[u]])
        lrun_scr[u] = l
        acc_scr[u] = pv
        row_max.append(m)

    def weigh(t, _):
        for (u, _), m in zip(query_blocks, row_max):
            l, pv = accumulate([(j, s_scr[u, j] - m) for j in (2 * t, 2 * t + 1)])
            lrun_scr[u] += l
            acc_scr[u] += pv
        return 0

    lax.fori_loop(0, n_pair_steps, weigh, 0)

    for u, _ in query_blocks:
        out = acc_scr[u] / jnp.sum(lrun_scr[u], axis=1, keepdims=True)
        o_ref[u * ATTN_TQ:(u + 1) * ATTN_TQ, :] = _merge_heads(
            [out[h * ATTN_TQ:(h + 1) * ATTN_TQ, :] for h in range(N_HEADS)]).astype(BF16)


def _fox_attention(main3, f3, bf_row, qg_row, kg_row):
    b, s, _ = main3.shape
    rows_all = N_HEADS * ATTN_TQ
    step_rows = FOX_QBLOCKS * ATTN_TQ
    col = lambda k: pl.BlockSpec((None, s, MIX_WIDTH), lambda bi, i, k=k: (bi, 0, k))
    return pl.pallas_call(
        _fox_kernel,
        grid=(b, s // step_rows),
        in_specs=[col(COL_FOX_Q), col(COL_FOX_K), col(COL_FOX_V),
                  pl.BlockSpec((None, s, LANES), lambda bi, i: (bi, 0, 0)),
                  _resident((1, LANES)),
                  _resident((1, MIX_WIDTH)),
                  _resident((1, MIX_WIDTH))],
        out_specs=pl.BlockSpec((None, step_rows, MIX_WIDTH), lambda bi, i: (bi, i, 0)),
        out_shape=jax.ShapeDtypeStruct((b, s, MIX_WIDTH), BF16),
        scratch_shapes=[pltpu.VMEM((s, MIX_WIDTH), BF16),
                        pltpu.VMEM((s, MIX_WIDTH), BF16),
                        pltpu.VMEM((s, LANES), BF16),
                        pltpu.VMEM((s, LANES), BF16),
                        pltpu.VMEM((FOX_QBLOCKS, s // ATTN_TK, rows_all, ATTN_TK), F32),
                        pltpu.VMEM((FOX_QBLOCKS, rows_all, LANES), F32),
                        pltpu.VMEM((FOX_QBLOCKS, rows_all, LANES), F32),
                        pltpu.VMEM((FOX_QBLOCKS, rows_all, LANES), F32)],
        compiler_params=_params(2),
        name="fox",
    )(main3, main3, main3, f3, bf_row, qg_row, kg_row)


def _merge_kernel(h_ref, g_ref, yab_ref, yc_ref, yd_ref, wg_ref, wb_ref, wo_ref, o_ref, merged_scr):
    xn = _rms_norm_rows(h_ref[...], g_ref[...]).astype(BF16)
    ys = [yab_ref[:, 0:MIX_WIDTH], yab_ref[:, MIX_WIDTH:2 * MIX_WIDTH], yc_ref[...], yd_ref[...]]
    for c in range(D_MODEL // COL_CHUNK):
        cs = slice(c * COL_CHUNK, (c + 1) * COL_CHUNK)
        acc = None
        for n in range(N_BRANCH):
            gs = slice(n * D_MODEL + c * COL_CHUNK, n * D_MODEL + (c + 1) * COL_CHUNK)
            term = jax.nn.sigmoid(_dot(xn, wg_ref[:, gs])) * _dot(ys[n], wb_ref[n, :, cs])
            acc = term if acc is None else acc + term
        merged_scr[:, cs] = acc.astype(BF16)
    merged = merged_scr[...]
    for c in range(D_MODEL // COL_CHUNK):
        cs = slice(c * COL_CHUNK, (c + 1) * COL_CHUNK)
        o_ref[:, cs] = h_ref[:, cs] + _dot(merged, wo_ref[:, cs])


def _merge(h, g, yab, yc, yd, w_gate, w_branch, w_out, layer):
    t = h.shape[0]
    rows = lambda w: pl.BlockSpec((TOKEN_TILE, w), lambda i: (i, 0))
    return pl.pallas_call(
        _merge_kernel,
        grid=(t // TOKEN_TILE,),
        in_specs=[rows(D_MODEL), _resident((1, D_MODEL)),
                  rows(2 * MIX_WIDTH), rows(MIX_WIDTH), rows(MIX_WIDTH),
                  _resident_layer((D_MODEL, N_BRANCH * D_MODEL), layer),
                  _resident_layer((N_BRANCH, MIX_WIDTH, D_MODEL), layer),
                  _resident_layer((D_MODEL, D_MODEL), layer)],
        out_specs=rows(D_MODEL),
        out_shape=jax.ShapeDtypeStruct((t, D_MODEL), F32),
        scratch_shapes=[pltpu.VMEM((TOKEN_TILE, D_MODEL), BF16)],
        compiler_params=_params(1),
        name="merge",
    )(h, g, yab, yc, yd, w_gate, w_branch, w_out)


def _ffn_kernel(h_ref, g_ref, wi_ref, wo_ref, o_ref, acc_scr):
    xn = _rms_norm_rows(h_ref[...], g_ref[...]).astype(BF16)
    for c in range(FFN_HIDDEN // COL_CHUNK):
        gate = _dot(xn, wi_ref[:, c * COL_CHUNK:(c + 1) * COL_CHUNK])
        up = _dot(xn, wi_ref[:, FFN_HIDDEN + c * COL_CHUNK:FFN_HIDDEN + (c + 1) * COL_CHUNK])
        act = (jax.nn.silu(gate) * up).astype(BF16)
        part = _dot(act, wo_ref[c * COL_CHUNK:(c + 1) * COL_CHUNK, :])
        if c == 0:
            acc_scr[...] = part
        else:
            acc_scr[...] += part
    o_ref[...] = h_ref[...] + acc_scr[...]


def _ffn(h, g, w_in, w_out, layer):
    t = h.shape[0]
    rows = pl.BlockSpec((TOKEN_TILE, D_MODEL), lambda i: (i, 0))
    return pl.pallas_call(
        _ffn_kernel,
        grid=(t // TOKEN_TILE,),
        in_specs=[rows, _resident((1, D_MODEL)),
                  _resident_layer((D_MODEL, 2 * FFN_HIDDEN), layer),
                  _resident_layer((FFN_HIDDEN, D_MODEL), layer)],
        out_specs=rows,
        out_shape=jax.ShapeDtypeStruct((t, D_MODEL), F32),
        scratch_shapes=[pltpu.VMEM((TOKEN_TILE, D_MODEL), F32)],
        compiler_params=_params(1),
        name="ffn",
    )(h, g, w_in, w_out)


def kernel(x, norm_mix_g, w_in, w_conv, w_spatial, b_spatial, gmlp_ln_g, gmlp_ln_b,
           fox_q_norm_g, fox_k_norm_g, fox_forget_b, w_branch, w_out, norm_ffn_g,
           w_ffn_in, w_ffn_out):
    b, s, d = x.shape
    depth = w_in.shape[0]
    assert d == D_MODEL and ATTN_TQ == ATTN_TK
    for rows in (SB_QBLOCKS * ATTN_TQ, FOX_QBLOCKS * ATTN_TQ, 2 * GMLP_CHUNK, FOX_PREP_ROWS, TOKEN_TILE):
        assert s % rows == 0, (s, rows)
    t = b * s
    h = x.reshape(t, d)
    w_in_cols = lax.optimization_barrier(jnp.transpose(w_in, (2, 0, 1)).astype(BF16))
    w_main_b = jnp.transpose(w_in_cols[:N_MAIN], (1, 2, 0))
    w_forget_b = jnp.transpose(w_in_cols[N_MAIN:N_MAIN + N_HEADS], (1, 2, 0))
    w_gate_b = jnp.transpose(w_in_cols[N_MAIN + N_HEADS:], (1, 2, 0))
    w_branch_b = w_branch.astype(BF16)
    w_out_b = w_out.astype(BF16)
    w_ffn_in_b = w_ffn_in.astype(BF16)
    w_ffn_out_b = w_ffn_out.astype(BF16)
    n_f = FORGET_COPIES * N_HEADS
    for l in range(depth):
        w_f = jnp.pad(jnp.tile(w_forget_b[l], (1, FORGET_COPIES)), ((0, 0), (0, LANES - n_f)))
        bf_row = jnp.pad(jnp.tile(fox_forget_b[l], FORGET_COPIES), (0, LANES - n_f)).reshape(1, LANES)
        qg_row = jnp.tile(fox_q_norm_g[l], N_HEADS).reshape(1, MIX_WIDTH)
        kg_row = jnp.tile(fox_k_norm_g[l], N_HEADS).reshape(1, MIX_WIDTH)
        bs_rows = jnp.repeat(b_spatial[l].T, GROUP_WIDTH, axis=1)

        main, f_raw = _proj(h, norm_mix_g[l].reshape(1, d), w_main_b, l, w_f)
        main3 = main.reshape(b, s, N_MAIN)
        yab = _convgmlp(main3, w_conv[l], w_spatial[l], bs_rows,
                        gmlp_ln_g[l].reshape(1, MIX_WIDTH), gmlp_ln_b[l].reshape(1, MIX_WIDTH))
        yc = _sb_attention(main3)
        yd = _fox_attention(main3, f_raw.reshape(b, s, LANES), bf_row, qg_row, kg_row)
        h = _merge(h, norm_mix_g[l].reshape(1, d), yab.reshape(t, 2 * MIX_WIDTH),
                   yc.reshape(t, MIX_WIDTH), yd.reshape(t, MIX_WIDTH),
                   w_gate_b, w_branch_b, w_out_b, l)
        h = _ffn(h, norm_ffn_g[l].reshape(1, d), w_ffn_in_b, w_ffn_out_b, l)
    return h.reshape(b, s, d)
```

```python
import functools

import jax
import jax.numpy as jnp
from jax import lax
from jax.experimental import pallas as pl
from jax.experimental.pallas import tpu as pltpu

D_MODEL = 1024
MIX_WIDTH = 256
HEAD_DIM = 64
N_HEADS = MIX_WIDTH // HEAD_DIM
N_BRANCH = 4
CONV_K = 3
GMLP_GROUPS = 4
GMLP_CHUNK = 128
GROUP_WIDTH = MIX_WIDTH // GMLP_GROUPS
FFN_HIDDEN = 2816
EPS = 1e-6
N_MAIN = 11 * MIX_WIDTH
(COL_CONV_B, COL_CONV_C, COL_CONV_X, COL_GMLP_U, COL_GMLP_V, COL_SB_Q, COL_SB_K, COL_SB_V,
 COL_FOX_Q, COL_FOX_K, COL_FOX_V) = range(11)
LANES = 128
HEADS_PER_PAIR = LANES // HEAD_DIM
N_PAIRS = MIX_WIDTH // LANES
FORGET_COPIES = 6
FOX_PREP_ROWS = 256
SUBLANES = 8
VMEM_LIMIT_BYTES = 56 * 1024 * 1024

TOKEN_TILE = 1024
COL_CHUNK = 256
ATTN_TQ = 256
ATTN_TK = 256
SB_QBLOCKS = 8
FOX_QBLOCKS = 4
SCALE = HEAD_DIM ** -0.5
LOG2_E = 1.4426950408889634
SB_Q_SCALE = -SCALE * LOG2_E
LOG2_F32_UNDERFLOW = -151.0
NEG_BIG = -1e30

F32 = jnp.float32
BF16 = jnp.bfloat16


def _dot(a, b):
    return jnp.dot(a, b, preferred_element_type=F32)


def _dot_nt(a, b):
    return lax.dot_general(a, b, (((1,), (1,)), ((), ())), preferred_element_type=F32)


def _split_hi_lo(x):
    hi = x.astype(BF16)
    lo = (x - hi.astype(F32)).astype(BF16)
    return hi, lo


def _rms_norm_rows(x, g):
    ms = jnp.mean(x * x, axis=-1, keepdims=True)
    return x * lax.rsqrt(ms + EPS) * g


def _resident(shape):
    return pl.BlockSpec(shape, lambda *_: (0,) * len(shape), pipeline_mode=pl.Buffered(1))


def _resident_layer(shape, layer):
    return pl.BlockSpec((None,) + tuple(shape), lambda *_: (layer,) + (0,) * len(shape),
                        pipeline_mode=pl.Buffered(1))


def _params(n_axes):
    return pltpu.CompilerParams(dimension_semantics=("arbitrary",) * n_axes,
                                vmem_limit_bytes=VMEM_LIMIT_BYTES)


def _split3(x):
    hi = x.astype(BF16).astype(F32)
    rem = x - hi
    mid = rem.astype(BF16).astype(F32)
    return hi, mid, rem - mid


def _proj_kernel(h_ref, g_ref, wm_ref, wf_ref, main_ref, f_ref):
    xn = _rms_norm_rows(h_ref[...], g_ref[...]).astype(BF16)
    for c in range(N_MAIN // COL_CHUNK):
        cs = slice(c * COL_CHUNK, (c + 1) * COL_CHUNK)
        y = _dot(xn, wm_ref[:, cs])
        if c == COL_SB_Q:
            y = y * SB_Q_SCALE
        main_ref[:, cs] = y.astype(BF16)
    f_ref[...] = _dot(xn, wf_ref[...])


def _proj(h, g, w_in_all, layer, w_f):
    t = h.shape[0]
    return pl.pallas_call(
        _proj_kernel,
        grid=(t // TOKEN_TILE,),
        in_specs=[pl.BlockSpec((TOKEN_TILE, D_MODEL), lambda i: (i, 0)),
                  _resident((1, D_MODEL)),
                  _resident_layer((D_MODEL, N_MAIN), layer),
                  _resident((D_MODEL, LANES))],
        out_specs=[pl.BlockSpec((TOKEN_TILE, N_MAIN), lambda i: (i, 0)),
                   pl.BlockSpec((TOKEN_TILE, LANES), lambda i: (i, 0))],
        out_shape=[jax.ShapeDtypeStruct((t, N_MAIN), BF16),
                   jax.ShapeDtypeStruct((t, LANES), F32)],
        compiler_params=_params(1),
        name="proj",
    )(h, g, w_in_all, w_f)


def _convgmlp_kernel(cb_ref, cc_ref, cx_ref, u_ref, v_ref, wconv_ref, ws_ref, bs_ref,
                     lng_ref, lnb_ref, o_ref):
    seq = cb_ref.shape[0]
    n_chunks = seq // GMLP_CHUNK
    row = lax.broadcasted_iota(jnp.int32, (GMLP_CHUNK, GMLP_CHUNK), 0)
    col = lax.broadcasted_iota(jnp.int32, (GMLP_CHUNK, GMLP_CHUNK), 1)
    w_tril = [jnp.where(col <= row, ws_ref[gi], 0.0).astype(BF16) for gi in range(GMLP_GROUPS)]
    lane = lax.broadcasted_iota(jnp.int32, (GMLP_CHUNK, LANES), 1)
    first_group = lane < GROUP_WIDTH
    w0 = wconv_ref[0:1, :]
    w1 = wconv_ref[1:2, :]
    w2 = wconv_ref[2:3, :]

    def chunk(c, prev_tail):
        rows = pl.ds(pl.multiple_of(c * GMLP_CHUNK, GMLP_CHUNK), GMLP_CHUNK)
        xc = cc_ref[rows, :].astype(F32) * cx_ref[rows, :].astype(F32)
        win = jnp.concatenate([prev_tail, xc], axis=0)
        xc1 = pltpu.roll(win, 1, 0)[SUBLANES:, :]
        xc2 = pltpu.roll(win, 2, 0)[SUBLANES:, :]
        ya = cb_ref[rows, :].astype(F32) * (w0 * xc2 + w1 * xc1 + w2 * xc)
        o_ref[rows, 0:MIX_WIDTH] = ya.astype(BF16)

        gu = jax.nn.gelu(u_ref[rows, :].astype(F32))
        gv = jax.nn.gelu(v_ref[rows, :].astype(F32))
        mu = jnp.mean(gv, axis=-1, keepdims=True)
        cen = gv - mu
        var = jnp.mean(cen * cen, axis=-1, keepdims=True)
        vn = (cen * lax.rsqrt(var + EPS) * lng_ref[...] + lnb_ref[...]).astype(BF16)
        halves = []
        for lb in range(MIX_WIDTH // LANES):
            vb = vn[:, lb * LANES:(lb + 1) * LANES]
            m0 = _dot(w_tril[2 * lb], vb)
            m1 = _dot(w_tril[2 * lb + 1], vb)
            halves.append(jnp.where(first_group, m0, m1))
        mixed = jnp.concatenate(halves, axis=1) + bs_ref[...]
        o_ref[rows, MIX_WIDTH:2 * MIX_WIDTH] = (gu * mixed).astype(BF16)
        return xc[GMLP_CHUNK - SUBLANES:, :]

    lax.fori_loop(0, n_chunks // 2, lambda c2, tail: chunk(2 * c2 + 1, chunk(2 * c2, tail)),
                  jnp.zeros((SUBLANES, MIX_WIDTH), F32))


def _convgmlp(main3, w_conv, w_s, bs_rows, ln_g, ln_b):
    b, s, _ = main3.shape
    col = lambda k: pl.BlockSpec((None, s, MIX_WIDTH), lambda i, k=k: (i, 0, k))
    return pl.pallas_call(
        _convgmlp_kernel,
        grid=(b,),
        in_specs=[col(COL_CONV_B), col(COL_CONV_C), col(COL_CONV_X), col(COL_GMLP_U), col(COL_GMLP_V),
                  _resident((CONV_K, MIX_WIDTH)),
                  _resident((GMLP_GROUPS, GMLP_CHUNK, GMLP_CHUNK)),
                  _resident((GMLP_CHUNK, MIX_WIDTH)),
                  _resident((1, MIX_WIDTH)),
                  _resident((1, MIX_WIDTH))],
        out_specs=pl.BlockSpec((None, s, 2 * MIX_WIDTH), lambda i: (i, 0, 0)),
        out_shape=jax.ShapeDtypeStruct((b, s, 2 * MIX_WIDTH), BF16),
        compiler_params=_params(1),
        name="convgmlp",
    )(main3, main3, main3, main3, main3, w_conv, w_s, bs_rows, ln_g, ln_b)


def _pair(x, h):
    p = (h * HEAD_DIM) // LANES
    return x[:, p * LANES:(p + 1) * LANES]


def _head_in_pair_mask(rows, h):
    lane = lax.broadcasted_iota(jnp.int32, (rows, LANES), 1)
    first = lane < HEAD_DIM
    return first if (h * HEAD_DIM) % LANES == 0 else jnp.logical_not(first)


def _masked_head(x, h):
    xp = _pair(x, h)
    return jnp.where(_head_in_pair_mask(x.shape[0], h), xp, jnp.zeros_like(xp))


def _merge_heads(per_head):
    rows = per_head[0].shape[0]
    blocks = []
    for p in range(MIX_WIDTH // LANES):
        h0 = p * (LANES // HEAD_DIM)
        blocks.append(jnp.where(_head_in_pair_mask(rows, h0), per_head[h0], per_head[h0 + 1]))
    return jnp.concatenate(blocks, axis=1)


def _sb_kernel(q_ref, k_ref, v_ref, o_ref, carry_scr, acc_scr):
    step_id = pl.program_id(1)
    r = lax.broadcasted_iota(jnp.int32, (ATTN_TK, ATTN_TK), 0)
    c = lax.broadcasted_iota(jnp.int32, (ATTN_TK, ATTN_TK), 1)
    suffix = jnp.where(r > c, 1.0, 0.0).astype(BF16)
    rows_all = N_HEADS * ATTN_TQ
    rq = lax.broadcasted_iota(jnp.int32, (rows_all, ATTN_TK), 0) & (ATTN_TQ - 1)
    cq = lax.broadcasted_iota(jnp.int32, (rows_all, ATTN_TK), 1)
    strict = cq < rq
    pair_rows = HEADS_PER_PAIR * ATTN_TQ

    def make_group(u, i):
        q = q_ref[pl.ds(pl.multiple_of(i * ATTN_TQ, ATTN_TQ), ATTN_TQ), :]
        qneg = [jnp.concatenate([_masked_head(q, HEADS_PER_PAIR * p + e) for e in range(HEADS_PER_PAIR)],
                                axis=0) for p in range(N_PAIRS)]

        def group(blocks, first):
            carry = jnp.zeros((rows_all, 1), F32) if first else carry_scr[u]
            staged = []
            for j, valid, diag in blocks:
                ks = pl.ds(pl.multiple_of(j * ATTN_TK, ATTN_TK), ATTN_TK)
                k = k_ref[ks, :]
                zn2 = jnp.concatenate(
                    [_dot_nt(qneg[p], k[:, p * LANES:(p + 1) * LANES]) for p in range(N_PAIRS)], axis=0)
                l2 = jnp.minimum(zn2, 0.0) - jnp.log2(1.0 + jnp.exp2(-jnp.abs(zn2)))
                if diag:
                    l2 = jnp.where(strict, l2, 0.0)
                staged.append((ks, valid, diag, l2 - zn2, _dot(l2.astype(BF16), suffix),
                               jnp.sum(l2, axis=1, keepdims=True)))
            pv = None
            for ks, valid, diag, log2_beta, later_in, row_sum in staged:
                if valid is None:
                    w = jnp.exp2(later_in + carry + log2_beta)
                else:
                    w = jnp.exp2(later_in + (carry + jnp.where(valid, 0.0, NEG_BIG)) + log2_beta)
                    row_sum = jnp.where(valid, row_sum, 0.0)
                if diag:
                    w = jnp.where(strict, w, 0.0)
                wb = w.astype(BF16)
                v = v_ref[ks, :]
                contrib = jnp.concatenate(
                    [_dot(wb[p * pair_rows:(p + 1) * pair_rows, :], v[:, p * LANES:(p + 1) * LANES])
                     for p in range(N_PAIRS)], axis=0)
                pv = contrib if pv is None else pv + contrib
                carry = carry + row_sum
            if first:
                acc_scr[u] = pv
            else:
                acc_scr[u] += pv
            carry_scr[u] = carry

        return group

    query_blocks = [(u, SB_QBLOCKS * step_id + u) for u in range(SB_QBLOCKS)]
    groups = [make_group(u, i) for u, i in query_blocks]
    for (u, i), group in zip(query_blocks, groups):
        group([(i, None, True), (jnp.maximum(i - 1, 0), i >= 1, False)], True)

    for (u, i), group in zip(query_blocks, groups):
        def alive(u=u):
            return jnp.max(carry_scr[u]) > LOG2_F32_UNDERFLOW

        first_left = i - 2
        n_groups = i // 2

        def step(st, group=group, alive=alive, first_left=first_left):
            t, _ = st
            j_near = first_left - 2 * t
            group([(j_near, None, False), (jnp.maximum(j_near - 1, 0), j_near >= 1, False)], False)
            return t + 1, alive()

        lax.while_loop(lambda st, n_groups=n_groups: (st[0] < n_groups) & st[1], step, (jnp.int32(0), alive()))
        acc = acc_scr[u]
        o_ref[u * ATTN_TQ:(u + 1) * ATTN_TQ, :] = _merge_heads(
            [acc[h * ATTN_TQ:(h + 1) * ATTN_TQ, :] for h in range(N_HEADS)]).astype(BF16)


def _sb_attention(main3):
    b, s, _ = main3.shape
    step_rows = SB_QBLOCKS * ATTN_TQ
    col = lambda k: pl.BlockSpec((None, s, MIX_WIDTH), lambda bi, i, k=k: (bi, 0, k))
    return pl.pallas_call(
        _sb_kernel,
        grid=(b, s // step_rows),
        in_specs=[col(COL_SB_Q), col(COL_SB_K), col(COL_SB_V)],
        out_specs=pl.BlockSpec((None, step_rows, MIX_WIDTH), lambda bi, i: (bi, i, 0)),
        out_shape=jax.ShapeDtypeStruct((b, s, MIX_WIDTH), BF16),
        scratch_shapes=[pltpu.VMEM((SB_QBLOCKS, N_HEADS * ATTN_TQ, 1), F32),
                        pltpu.VMEM((SB_QBLOCKS, N_HEADS * ATTN_TQ, LANES), F32)],
        compiler_params=_params(2),
        name="sb",
    )(main3, main3, main3)


def _fox_kernel(q_ref, k_ref, v_ref, f_ref, bf_ref, qg_ref, kg_ref, o_ref,
                qn_scr, kn_scr, qaug_scr, kaug_scr, s_scr, mrun_scr, lrun_scr, acc_scr):
    step_id = pl.program_id(1)
    seq = q_ref.shape[0]

    @pl.when(step_id == 0)
    def _prepare():
        r = lax.broadcasted_iota(jnp.int32, (MIX_WIDTH, MIX_WIDTH), 0) // HEAD_DIM
        c = lax.broadcasted_iota(jnp.int32, (MIX_WIDTH, MIX_WIDTH), 1) // HEAD_DIM
        same_head = jnp.where(r == c, 1.0, 0.0).astype(BF16)
        tr = lax.broadcasted_iota(jnp.int32, (FOX_PREP_ROWS, FOX_PREP_ROWS), 0)
        tc = lax.broadcasted_iota(jnp.int32, (FOX_PREP_ROWS, FOX_PREP_ROWS), 1)
        prefix = jnp.where(tc <= tr, 1.0, 0.0).astype(BF16)
        lane = lax.broadcasted_iota(jnp.int32, (FOX_PREP_ROWS, LANES), 1)
        n = N_HEADS

        def tile(tix, run):
            rows = pl.ds(pl.multiple_of(tix * FOX_PREP_ROWS, FOX_PREP_ROWS), FOX_PREP_ROWS)
            for x_ref, g_ref, scale, dst in ((q_ref, qg_ref, SCALE, qn_scr), (k_ref, kg_ref, 1.0, kn_scr)):
                x = x_ref[rows, :].astype(F32)
                hi, lo = _split_hi_lo(x * x)
                ms = (_dot(hi, same_head) + _dot(lo, same_head)) * (1.0 / HEAD_DIM)
                dst[rows, :] = (x * lax.rsqrt(ms + EPS) * (g_ref[...] * scale)).astype(BF16)
            hi, lo = _split_hi_lo(jax.nn.log_sigmoid(f_ref[rows, :] + bf_ref[...]))
            cum = run + _dot(prefix, hi) + _dot(prefix, lo)
            c_hi, c_mid, c_lo = _split3(cum)
            q_aug = jnp.where(lane < n, c_hi, jnp.where(lane < 2 * n, c_mid, jnp.where(
                lane < 3 * n, c_lo, jnp.where(lane < 6 * n, 1.0, 0.0))))
            k_aug = jnp.where(lane < 3 * n, 1.0, jnp.where(lane < 4 * n, -c_hi, jnp.where(
                lane < 5 * n, -c_mid, jnp.where(lane < 6 * n, -c_lo, 0.0))))
            qaug_scr[rows, :] = q_aug.astype(BF16)
            kaug_scr[rows, :] = k_aug.astype(BF16)
            return cum[FOX_PREP_ROWS - 1:FOX_PREP_ROWS, :]

        lax.fori_loop(0, seq // FOX_PREP_ROWS, tile, jnp.zeros((1, LANES), F32))

    rows_all = N_HEADS * ATTN_TQ
    pair_rows = HEADS_PER_PAIR * ATTN_TQ
    lane = lax.broadcasted_iota(jnp.int32, (ATTN_TQ, LANES), 1)
    r = lax.broadcasted_iota(jnp.int32, (rows_all, ATTN_TK), 0) & (ATTN_TQ - 1)
    c = lax.broadcasted_iota(jnp.int32, (rows_all, ATTN_TK), 1)
    causal = c <= r
    query_blocks = [(u, FOX_QBLOCKS * step_id + u) for u in range(FOX_QBLOCKS)]

    def key_rows(j):
        return pl.ds(pl.multiple_of(j * ATTN_TK, ATTN_TK), ATTN_TK)

    def fold(x):
        return [x[:, n * LANES:(n + 1) * LANES] for n in range(ATTN_TK // LANES)]

    def make_scores(i):
        qrows = key_rows(i)
        q = qn_scr[qrows, :]
        qaug = qaug_scr[qrows, :]
        qs = []
        for p in range(N_PAIRS):
            rows = []
            for e in range(HEADS_PER_PAIR):
                h = HEADS_PER_PAIR * p + e
                aug_h = jnp.where((lane & (N_HEADS - 1)) == h, qaug, jnp.zeros_like(qaug))
                rows.append(jnp.concatenate([_masked_head(q, h), aug_h], axis=1))
            qs.append(jnp.concatenate(rows, axis=0))

        def scores(j):
            ks = key_rows(j)
            kaug = kaug_scr[ks, :]
            return jnp.concatenate(
                [_dot_nt(qs[p], jnp.concatenate([kn_scr[ks, p * LANES:(p + 1) * LANES], kaug], axis=1))
                 for p in range(N_PAIRS)], axis=0) * LOG2_E

        return scores

    def accumulate(blocks):
        l_parts, pv = [], None
        for j, shifted in blocks:
            ks = key_rows(j)
            p = jnp.exp2(shifted)
            l_parts += fold(p)
            pb = p.astype(BF16)
            contrib = jnp.concatenate(
                [_dot(pb[pr * pair_rows:(pr + 1) * pair_rows, :], v_ref[ks, pr * LANES:(pr + 1) * LANES])
                 for pr in range(N_PAIRS)], axis=0)
            pv = contrib if pv is None else pv + contrib
        return functools.reduce(jnp.add, l_parts), pv

    assert FOX_QBLOCKS % 2 == 0
    i0 = FOX_QBLOCKS * step_id
    head_blocks = [[i0 + v for v in range(u, -1, -1)] for u in range(FOX_QBLOCKS)]
    n_pair_steps = (FOX_QBLOCKS // 2) * step_id
    score_fns = [make_scores(i) for _, i in query_blocks]

    for (u, i), scores in zip(query_blocks, score_fns):
        parts = []
        for n, j in enumerate(head_blocks[u]):
            s = jnp.where(causal, scores(j), NEG_BIG) if n == 0 else scores(j)
            s_scr[u, j] = s
            parts += fold(s)
        mrun_scr[u] = functools.reduce(jnp.maximum, parts)

    def stage(t, _):
        for (u, _), scores in zip(query_blocks, score_fns):
            parts = []
            for j in (2 * t, 2 * t + 1):
                s = scores(j)
                s_scr[u, j] = s
                parts += fold(s)
            mrun_scr[u] = functools.reduce(jnp.maximum, parts, mrun_scr[u])
        return 0

    lax.fori_loop(0, n_pair_steps, stage, 0)

    row_max = []
    for u, _ in query_blocks:
        m = jnp.max(mrun_scr[u], axis=1, keepdims=True)
        l, pv = accumulate([(j, s_scr[u, j] - m) for j in head_blocks[u]])
        lrun_scr[u] = l
        acc_scr[u] = pv
        row_max.append(m)

    def weigh(t, _):
        for (u, _), m in zip(query_blocks, row_max):
            l, pv = accumulate([(j, s_scr[u, j] - m) for j in (2 * t, 2 * t + 1)])
            lrun_scr[u] += l
            acc_scr[u] += pv
        return 0

    lax.fori_loop(0, n_pair_steps, weigh, 0)

    for u, _ in query_blocks:
        out = acc_scr[u] / jnp.sum(lrun_scr[u], axis=1, keepdims=True)
        o_ref[u * ATTN_TQ:(u + 1) * ATTN_TQ, :] = _merge_heads(
            [out[h * ATTN_TQ:(h + 1) * ATTN_TQ, :] for h in range(N_HEADS)]).astype(BF16)


def _fox_attention(main3, f3, bf_row, qg_row, kg_row):
    b, s, _ = main3.shape
    rows_all = N_HEADS * ATTN_TQ
    step_rows = FOX_QBLOCKS * ATTN_TQ
    col = lambda k: pl.BlockSpec((None, s, MIX_WIDTH), lambda bi, i, k=k: (bi, 0, k))
    return pl.pallas_call(
        _fox_kernel,
        grid=(b, s // step_rows),
        in_specs=[col(COL_FOX_Q), col(COL_FOX_K), col(COL_FOX_V),
                  pl.BlockSpec((None, s, LANES), lambda bi, i: (bi, 0, 0)),
                  _resident((1, LANES)),
                  _resident((1, MIX_WIDTH)),
                  _resident((1, MIX_WIDTH))],
        out_specs=pl.BlockSpec((None, step_rows, MIX_WIDTH), lambda bi, i: (bi, i, 0)),
        out_shape=jax.ShapeDtypeStruct((b, s, MIX_WIDTH), BF16),
        scratch_shapes=[pltpu.VMEM((s, MIX_WIDTH), BF16),
                        pltpu.VMEM((s, MIX_WIDTH), BF16),
                        pltpu.VMEM((s, LANES), BF16),
                        pltpu.VMEM((s, LANES), BF16),
                        pltpu.VMEM((FOX_QBLOCKS, s // ATTN_TK, rows_all, ATTN_TK), F32),
                        pltpu.VMEM((FOX_QBLOCKS, rows_all, LANES), F32),
                        pltpu.VMEM((FOX_QBLOCKS, rows_all, LANES), F32),
                        pltpu.VMEM((FOX_QBLOCKS, rows_all, LANES), F32)],
        compiler_params=_params(2),
        name="fox",
    )(main3, main3, main3, f3, bf_row, qg_row, kg_row)


def _merge_kernel(h_ref, g_ref, yab_ref, yc_ref, yd_ref, wg_ref, wb_ref, wo_ref, o_ref, merged_scr):
    xn = _rms_norm_rows(h_ref[...], g_ref[...]).astype(BF16)
    ys = [yab_ref[:, 0:MIX_WIDTH], yab_ref[:, MIX_WIDTH:2 * MIX_WIDTH], yc_ref[...], yd_ref[...]]
    for c in range(D_MODEL // COL_CHUNK):
        cs = slice(c * COL_CHUNK, (c + 1) * COL_CHUNK)
        acc = None
        for n in range(N_BRANCH):
            gs = slice(n * D_MODEL + c * COL_CHUNK, n * D_MODEL + (c + 1) * COL_CHUNK)
            term = jax.nn.sigmoid(_dot(xn, wg_ref[:, gs])) * _dot(ys[n], wb_ref[n, :, cs])
            acc = term if acc is None else acc + term
        merged_scr[:, cs] = acc.astype(BF16)
    merged = merged_scr[...]
    for c in range(D_MODEL // COL_CHUNK):
        cs = slice(c * COL_CHUNK, (c + 1) * COL_CHUNK)
        o_ref[:, cs] = h_ref[:, cs] + _dot(merged, wo_ref[:, cs])


def _merge(h, g, yab, yc, yd, w_gate, w_branch, w_out, layer):
    t = h.shape[0]
    rows = lambda w: pl.BlockSpec((TOKEN_TILE, w), lambda i: (i, 0))
    return pl.pallas_call(
        _merge_kernel,
        grid=(t // TOKEN_TILE,),
        in_specs=[rows(D_MODEL), _resident((1, D_MODEL)),
                  rows(2 * MIX_WIDTH), rows(MIX_WIDTH), rows(MIX_WIDTH),
                  _resident_layer((D_MODEL, N_BRANCH * D_MODEL), layer),
                  _resident_layer((N_BRANCH, MIX_WIDTH, D_MODEL), layer),
                  _resident_layer((D_MODEL, D_MODEL), layer)],
        out_specs=rows(D_MODEL),
        out_shape=jax.ShapeDtypeStruct((t, D_MODEL), F32),
        scratch_shapes=[pltpu.VMEM((TOKEN_TILE, D_MODEL), BF16)],
        compiler_params=_params(1),
        name="merge",
    )(h, g, yab, yc, yd, w_gate, w_branch, w_out)


def _ffn_kernel(h_ref, g_ref, wi_ref, wo_ref, o_ref, acc_scr):
    xn = _rms_norm_rows(h_ref[...], g_ref[...]).astype(BF16)
    for c in range(FFN_HIDDEN // COL_CHUNK):
        gate = _dot(xn, wi_ref[:, c * COL_CHUNK:(c + 1) * COL_CHUNK])
        up = _dot(xn, wi_ref[:, FFN_HIDDEN + c * COL_CHUNK:FFN_HIDDEN + (c + 1) * COL_CHUNK])
        act = (jax.nn.silu(gate) * up).astype(BF16)
        part = _dot(act, wo_ref[c * COL_CHUNK:(c + 1) * COL_CHUNK, :])
        if c == 0:
            acc_scr[...] = part
        else:
            acc_scr[...] += part
    o_ref[...] = h_ref[...] + acc_scr[...]


def _ffn(h, g, w_in, w_out, layer):
    t = h.shape[0]
    rows = pl.BlockSpec((TOKEN_TILE, D_MODEL), lambda i: (i, 0))
    return pl.pallas_call(
        _ffn_kernel,
        grid=(t // TOKEN_TILE,),
        in_specs=[rows, _resident((1, D_MODEL)),
                  _resident_layer((D_MODEL, 2 * FFN_HIDDEN), layer),
                  _resident_layer((FFN_HIDDEN, D_MODEL), layer)],
        out_specs=rows,
        out_shape=jax.ShapeDtypeStruct((t, D_MODEL), F32),
        scratch_shapes=[pltpu.VMEM((TOKEN_TILE, D_MODEL), F32)],
        compiler_params=_params(1),
        name="ffn",
    )(h, g, w_in, w_out)


def kernel(x, norm_mix_g, w_in, w_conv, w_spatial, b_spatial, gmlp_ln_g, gmlp_ln_b,
           fox_q_norm_g, fox_k_norm_g, fox_forget_b, w_branch, w_out, norm_ffn_g,
           w_ffn_in, w_ffn_out):
    b, s, d = x.shape
    depth = w_in.shape[0]
    assert d == D_MODEL and ATTN_TQ == ATTN_TK
    for rows in (SB_QBLOCKS * ATTN_TQ, FOX_QBLOCKS * ATTN_TQ, 2 * GMLP_CHUNK, FOX_PREP_ROWS, TOKEN_TILE):
        assert s % rows == 0, (s, rows)
    t = b * s
    h = x.reshape(t, d)
    w_in_b = jnp.transpose(lax.optimization_barrier(jnp.transpose(w_in, (2, 0, 1)).astype(BF16)), (1, 2, 0))
    w_gate_b = w_in_b[:, :, N_MAIN + N_HEADS:]
    w_branch_b = w_branch.astype(BF16)
    w_out_b = w_out.astype(BF16)
    w_ffn_in_b = w_ffn_in.astype(BF16)
    w_ffn_out_b = w_ffn_out.astype(BF16)
    n_f = FORGET_COPIES * N_HEADS
    for l in range(depth):
        w_f = jnp.pad(jnp.tile(w_in_b[l, :, N_MAIN:N_MAIN + N_HEADS], (1, FORGET_COPIES)),
                      ((0, 0), (0, LANES - n_f)))
        bf_row = jnp.pad(jnp.tile(fox_forget_b[l], FORGET_COPIES), (0, LANES - n_f)).reshape(1, LANES)
        qg_row = jnp.tile(fox_q_norm_g[l], N_HEADS).reshape(1, MIX_WIDTH)
        kg_row = jnp.tile(fox_k_norm_g[l], N_HEADS).reshape(1, MIX_WIDTH)
        bs_rows = jnp.repeat(b_spatial[l].T, GROUP_WIDTH, axis=1)

        main, f_raw = _proj(h, norm_mix_g[l].reshape(1, d), w_in_b, l, w_f)
        main3 = main.reshape(b, s, N_MAIN)
        yab = _convgmlp(main3, w_conv[l], w_spatial[l], bs_rows,
                        gmlp_ln_g[l].reshape(1, MIX_WIDTH), gmlp_ln_b[l].reshape(1, MIX_WIDTH))
        yc = _sb_attention(main3)
        yd = _fox_attention(main3, f_raw.reshape(b, s, LANES), bf_row, qg_row, kg_row)
        h = _merge(h, norm_mix_g[l].reshape(1, d), yab.reshape(t, 2 * MIX_WIDTH),
                   yc.reshape(t, MIX_WIDTH), yd.reshape(t, MIX_WIDTH),
                   w_gate_b, w_branch_b, w_out_b, l)
        h = _ffn(h, norm_ffn_g[l].reshape(1, d), w_ffn_in_b, w_ffn_out_b, l)
    return h.reshape(b, s, d)
```

```python
import functools

import jax
import jax.numpy as jnp
from jax import lax
from jax.experimental import pallas as pl
from jax.experimental.pallas import tpu as pltpu

D_MODEL = 1024
MIX_WIDTH = 256
HEAD_DIM = 64
N_HEADS = MIX_WIDTH // HEAD_DIM
N_BRANCH = 4
CONV_K = 3
GMLP_GROUPS = 4
GMLP_CHUNK = 128
GROUP_WIDTH = MIX_WIDTH // GMLP_GROUPS
FFN_HIDDEN = 2816
EPS = 1e-6
N_MAIN = 11 * MIX_WIDTH
(COL_CONV_B, COL_CONV_C, COL_CONV_X, COL_GMLP_U, COL_GMLP_V, COL_SB_Q, COL_SB_K, COL_SB_V,
 COL_FOX_Q, COL_FOX_K, COL_FOX_V) = range(11)
LANES = 128
HEADS_PER_PAIR = LANES // HEAD_DIM
N_PAIRS = MIX_WIDTH // LANES
FORGET_COPIES = 6
FOX_PREP_ROWS = 256
GMLP_CHUNKS_PER_STEP = 4
SUBLANES = 8
VMEM_LIMIT_BYTES = 56 * 1024 * 1024

TOKEN_TILE = 1024
COL_CHUNK = 256
ATTN_TQ = 256
ATTN_TK = 256
SB_QBLOCKS = 8
FOX_QBLOCKS = 4
SCALE = HEAD_DIM ** -0.5
LOG2_E = 1.4426950408889634
SB_Q_SCALE = -SCALE * LOG2_E
LOG2_F32_UNDERFLOW = -151.0
NEG_BIG = -1e30

F32 = jnp.float32
BF16 = jnp.bfloat16


def _dot(a, b):
    return jnp.dot(a, b, preferred_element_type=F32)


def _dot_nt(a, b):
    return lax.dot_general(a, b, (((1,), (1,)), ((), ())), preferred_element_type=F32)


def _split_hi_lo(x):
    hi = x.astype(BF16)
    lo = (x - hi.astype(F32)).astype(BF16)
    return hi, lo


def _rms_norm_rows(x, g):
    ms = jnp.mean(x * x, axis=-1, keepdims=True)
    return x * lax.rsqrt(ms + EPS) * g


def _resident(shape):
    return pl.BlockSpec(shape, lambda *_: (0,) * len(shape), pipeline_mode=pl.Buffered(1))


def _resident_layer(shape, layer):
    return pl.BlockSpec((None,) + tuple(shape), lambda *_: (layer,) + (0,) * len(shape),
                        pipeline_mode=pl.Buffered(1))


def _params(n_axes):
    return pltpu.CompilerParams(dimension_semantics=("arbitrary",) * n_axes,
                                vmem_limit_bytes=VMEM_LIMIT_BYTES)


def _split3(x):
    hi = x.astype(BF16).astype(F32)
    rem = x - hi
    mid = rem.astype(BF16).astype(F32)
    return hi, mid, rem - mid


def _proj_kernel(h_ref, g_ref, wm_ref, wf_ref, main_ref, f_ref):
    xn = _rms_norm_rows(h_ref[...], g_ref[...]).astype(BF16)
    for c in range(N_MAIN // COL_CHUNK):
        cs = slice(c * COL_CHUNK, (c + 1) * COL_CHUNK)
        y = _dot(xn, wm_ref[:, cs])
        if c == COL_SB_Q:
            y = y * SB_Q_SCALE
        main_ref[:, cs] = y.astype(BF16)
    f_ref[...] = _dot(xn, wf_ref[...])


def _proj(h, g, w_in_all, layer, w_f):
    t = h.shape[0]
    return pl.pallas_call(
        _proj_kernel,
        grid=(t // TOKEN_TILE,),
        in_specs=[pl.BlockSpec((TOKEN_TILE, D_MODEL), lambda i: (i, 0)),
                  _resident((1, D_MODEL)),
                  _resident_layer((D_MODEL, N_MAIN), layer),
                  _resident((D_MODEL, LANES))],
        out_specs=[pl.BlockSpec((TOKEN_TILE, N_MAIN), lambda i: (i, 0)),
                   pl.BlockSpec((TOKEN_TILE, LANES), lambda i: (i, 0))],
        out_shape=[jax.ShapeDtypeStruct((t, N_MAIN), BF16),
                   jax.ShapeDtypeStruct((t, LANES), F32)],
        compiler_params=_params(1),
        name="proj",
    )(h, g, w_in_all, w_f)


def _convgmlp_kernel(cb_ref, cc_ref, cx_ref, u_ref, v_ref, wconv_ref, ws_ref, bs_ref,
                     lng_ref, lnb_ref, o_ref):
    seq = cb_ref.shape[0]
    n_chunks = seq // GMLP_CHUNK
    row = lax.broadcasted_iota(jnp.int32, (GMLP_CHUNK, GMLP_CHUNK), 0)
    col = lax.broadcasted_iota(jnp.int32, (GMLP_CHUNK, GMLP_CHUNK), 1)
    w_tril = [jnp.where(col <= row, ws_ref[gi], 0.0).astype(BF16) for gi in range(GMLP_GROUPS)]
    lane = lax.broadcasted_iota(jnp.int32, (GMLP_CHUNK, LANES), 1)
    first_group = lane < GROUP_WIDTH
    w0 = wconv_ref[0:1, :]
    w1 = wconv_ref[1:2, :]
    w2 = wconv_ref[2:3, :]

    def chunk(c, prev_tail):
        rows = pl.ds(pl.multiple_of(c * GMLP_CHUNK, GMLP_CHUNK), GMLP_CHUNK)
        xc = cc_ref[rows, :].astype(F32) * cx_ref[rows, :].astype(F32)
        win = jnp.concatenate([prev_tail, xc], axis=0)
        xc1 = pltpu.roll(win, 1, 0)[SUBLANES:, :]
        xc2 = pltpu.roll(win, 2, 0)[SUBLANES:, :]
        ya = cb_ref[rows, :].astype(F32) * (w0 * xc2 + w1 * xc1 + w2 * xc)
        o_ref[rows, 0:MIX_WIDTH] = ya.astype(BF16)

        gu = jax.nn.gelu(u_ref[rows, :].astype(F32))
        gv = jax.nn.gelu(v_ref[rows, :].astype(F32))
        mu = jnp.mean(gv, axis=-1, keepdims=True)
        cen = gv - mu
        var = jnp.mean(cen * cen, axis=-1, keepdims=True)
        vn = (cen * lax.rsqrt(var + EPS) * lng_ref[...] + lnb_ref[...]).astype(BF16)
        halves = []
        for lb in range(MIX_WIDTH // LANES):
            vb = vn[:, lb * LANES:(lb + 1) * LANES]
            m0 = _dot(w_tril[2 * lb], vb)
            m1 = _dot(w_tril[2 * lb + 1], vb)
            halves.append(jnp.where(first_group, m0, m1))
        mixed = jnp.concatenate(halves, axis=1) + bs_ref[...]
        o_ref[rows, MIX_WIDTH:2 * MIX_WIDTH] = (gu * mixed).astype(BF16)
        return xc[GMLP_CHUNK - SUBLANES:, :]

    def step(c0, tail):
        for n in range(GMLP_CHUNKS_PER_STEP):
            tail = chunk(GMLP_CHUNKS_PER_STEP * c0 + n, tail)
        return tail

    lax.fori_loop(0, n_chunks // GMLP_CHUNKS_PER_STEP, step, jnp.zeros((SUBLANES, MIX_WIDTH), F32))


def _convgmlp(main3, w_conv, w_s, bs_rows, ln_g, ln_b):
    b, s, _ = main3.shape
    col = lambda k: pl.BlockSpec((None, s, MIX_WIDTH), lambda i, k=k: (i, 0, k))
    return pl.pallas_call(
        _convgmlp_kernel,
        grid=(b,),
        in_specs=[col(COL_CONV_B), col(COL_CONV_C), col(COL_CONV_X), col(COL_GMLP_U), col(COL_GMLP_V),
                  _resident((CONV_K, MIX_WIDTH)),
                  _resident((GMLP_GROUPS, GMLP_CHUNK, GMLP_CHUNK)),
                  _resident((GMLP_CHUNK, MIX_WIDTH)),
                  _resident((1, MIX_WIDTH)),
                  _resident((1, MIX_WIDTH))],
        out_specs=pl.BlockSpec((None, s, 2 * MIX_WIDTH), lambda i: (i, 0, 0)),
        out_shape=jax.ShapeDtypeStruct((b, s, 2 * MIX_WIDTH), BF16),
        compiler_params=_params(1),
        name="convgmlp",
    )(main3, main3, main3, main3, main3, w_conv, w_s, bs_rows, ln_g, ln_b)


def _pair(x, h):
    p = (h * HEAD_DIM) // LANES
    return x[:, p * LANES:(p + 1) * LANES]


def _head_in_pair_mask(rows, h):
    lane = lax.broadcasted_iota(jnp.int32, (rows, LANES), 1)
    first = lane < HEAD_DIM
    return first if (h * HEAD_DIM) % LANES == 0 else jnp.logical_not(first)


def _masked_head(x, h):
    xp = _pair(x, h)
    return jnp.where(_head_in_pair_mask(x.shape[0], h), xp, jnp.zeros_like(xp))


def _merge_heads(per_head):
    rows = per_head[0].shape[0]
    blocks = []
    for p in range(MIX_WIDTH // LANES):
        h0 = p * (LANES // HEAD_DIM)
        blocks.append(jnp.where(_head_in_pair_mask(rows, h0), per_head[h0], per_head[h0 + 1]))
    return jnp.concatenate(blocks, axis=1)


def _sb_kernel(q_ref, k_ref, v_ref, o_ref, carry_scr, acc_scr):
    step_id = pl.program_id(1)
    r = lax.broadcasted_iota(jnp.int32, (ATTN_TK, ATTN_TK), 0)
    c = lax.broadcasted_iota(jnp.int32, (ATTN_TK, ATTN_TK), 1)
    suffix = jnp.where(r > c, 1.0, 0.0).astype(BF16)
    rows_all = N_HEADS * ATTN_TQ
    rq = lax.broadcasted_iota(jnp.int32, (rows_all, ATTN_TK), 0) & (ATTN_TQ - 1)
    cq = lax.broadcasted_iota(jnp.int32, (rows_all, ATTN_TK), 1)
    strict = cq < rq
    pair_rows = HEADS_PER_PAIR * ATTN_TQ

    def make_group(u, i):
        q = q_ref[pl.ds(pl.multiple_of(i * ATTN_TQ, ATTN_TQ), ATTN_TQ), :]
        qneg = [jnp.concatenate([_masked_head(q, HEADS_PER_PAIR * p + e) for e in range(HEADS_PER_PAIR)],
                                axis=0) for p in range(N_PAIRS)]

        def group(blocks, first):
            carry = jnp.zeros((rows_all, 1), F32) if first else carry_scr[u]
            staged = []
            for j, valid, diag in blocks:
                ks = pl.ds(pl.multiple_of(j * ATTN_TK, ATTN_TK), ATTN_TK)
                k = k_ref[ks, :]
                zn2 = jnp.concatenate(
                    [_dot_nt(qneg[p], k[:, p * LANES:(p + 1) * LANES]) for p in range(N_PAIRS)], axis=0)
                l2 = jnp.minimum(zn2, 0.0) - jnp.log2(1.0 + jnp.exp2(-jnp.abs(zn2)))
                if diag:
                    l2 = jnp.where(strict, l2, 0.0)
                staged.append((ks, valid, diag, l2 - zn2, _dot(l2.astype(BF16), suffix),
                               jnp.sum(l2, axis=1, keepdims=True)))
            pv = None
            for ks, valid, diag, log2_beta, later_in, row_sum in staged:
                if valid is None:
                    w = jnp.exp2(later_in + carry + log2_beta)
                else:
                    w = jnp.exp2(later_in + (carry + jnp.where(valid, 0.0, NEG_BIG)) + log2_beta)
                    row_sum = jnp.where(valid, row_sum, 0.0)
                if diag:
                    w = jnp.where(strict, w, 0.0)
                wb = w.astype(BF16)
                v = v_ref[ks, :]
                contrib = jnp.concatenate(
                    [_dot(wb[p * pair_rows:(p + 1) * pair_rows, :], v[:, p * LANES:(p + 1) * LANES])
                     for p in range(N_PAIRS)], axis=0)
                pv = contrib if pv is None else pv + contrib
                carry = carry + row_sum
            if first:
                acc_scr[u] = pv
            else:
                acc_scr[u] += pv
            carry_scr[u] = carry

        return group

    query_blocks = [(u, SB_QBLOCKS * step_id + u) for u in range(SB_QBLOCKS)]
    groups = [make_group(u, i) for u, i in query_blocks]
    for (u, i), group in zip(query_blocks, groups):
        group([(i, None, True), (jnp.maximum(i - 1, 0), i >= 1, False)], True)

    for (u, i), group in zip(query_blocks, groups):
        def alive(u=u):
            return jnp.max(carry_scr[u]) > LOG2_F32_UNDERFLOW

        first_left = i - 2
        n_groups = i // 2

        def step(st, group=group, alive=alive, first_left=first_left):
            t, _ = st
            j_near = first_left - 2 * t
            group([(j_near, None, False), (jnp.maximum(j_near - 1, 0), j_near >= 1, False)], False)
            return t + 1, alive()

        lax.while_loop(lambda st, n_groups=n_groups: (st[0] < n_groups) & st[1], step, (jnp.int32(0), alive()))
        acc = acc_scr[u]
        o_ref[u * ATTN_TQ:(u + 1) * ATTN_TQ, :] = _merge_heads(
            [acc[h * ATTN_TQ:(h + 1) * ATTN_TQ, :] for h in range(N_HEADS)]).astype(BF16)


def _sb_attention(main3):
    b, s, _ = main3.shape
    step_rows = SB_QBLOCKS * ATTN_TQ
    col = lambda k: pl.BlockSpec((None, s, MIX_WIDTH), lambda bi, i, k=k: (bi, 0, k))
    return pl.pallas_call(
        _sb_kernel,
        grid=(b, s // step_rows),
        in_specs=[col(COL_SB_Q), col(COL_SB_K), col(COL_SB_V)],
        out_specs=pl.BlockSpec((None, step_rows, MIX_WIDTH), lambda bi, i: (bi, i, 0)),
        out_shape=jax.ShapeDtypeStruct((b, s, MIX_WIDTH), BF16),
        scratch_shapes=[pltpu.VMEM((SB_QBLOCKS, N_HEADS * ATTN_TQ, 1), F32),
                        pltpu.VMEM((SB_QBLOCKS, N_HEADS * ATTN_TQ, LANES), F32)],
        compiler_params=_params(2),
        name="sb",
    )(main3, main3, main3)


def _fox_kernel(q_ref, k_ref, v_ref, f_ref, bf_ref, qg_ref, kg_ref, o_ref,
                qn_scr, kn_scr, qaug_scr, kaug_scr, s_scr, mrun_scr, lrun_scr, acc_scr):
    step_id = pl.program_id(1)
    seq = q_ref.shape[0]

    @pl.when(step_id == 0)
    def _prepare():
        r = lax.broadcasted_iota(jnp.int32, (MIX_WIDTH, MIX_WIDTH), 0) // HEAD_DIM
        c = lax.broadcasted_iota(jnp.int32, (MIX_WIDTH, MIX_WIDTH), 1) // HEAD_DIM
        same_head = jnp.where(r == c, 1.0, 0.0).astype(BF16)
        tr = lax.broadcasted_iota(jnp.int32, (FOX_PREP_ROWS, FOX_PREP_ROWS), 0)
        tc = lax.broadcasted_iota(jnp.int32, (FOX_PREP_ROWS, FOX_PREP_ROWS), 1)
        prefix = jnp.where(tc <= tr, 1.0, 0.0).astype(BF16)
        lane = lax.broadcasted_iota(jnp.int32, (FOX_PREP_ROWS, LANES), 1)
        n = N_HEADS

        def tile(tix, run):
            rows = pl.ds(pl.multiple_of(tix * FOX_PREP_ROWS, FOX_PREP_ROWS), FOX_PREP_ROWS)
            for x_ref, g_ref, scale, dst in ((q_ref, qg_ref, SCALE, qn_scr), (k_ref, kg_ref, 1.0, kn_scr)):
                x = x_ref[rows, :].astype(F32)
                hi, lo = _split_hi_lo(x * x)
                ms = (_dot(hi, same_head) + _dot(lo, same_head)) * (1.0 / HEAD_DIM)
                dst[rows, :] = (x * lax.rsqrt(ms + EPS) * (g_ref[...] * scale)).astype(BF16)
            hi, lo = _split_hi_lo(jax.nn.log_sigmoid(f_ref[rows, :] + bf_ref[...]))
            cum = run + _dot(prefix, hi) + _dot(prefix, lo)
            c_hi, c_mid, c_lo = _split3(cum)
            q_aug = jnp.where(lane < n, c_hi, jnp.where(lane < 2 * n, c_mid, jnp.where(
                lane < 3 * n, c_lo, jnp.where(lane < 6 * n, 1.0, 0.0))))
            k_aug = jnp.where(lane < 3 * n, 1.0, jnp.where(lane < 4 * n, -c_hi, jnp.where(
                lane < 5 * n, -c_mid, jnp.where(lane < 6 * n, -c_lo, 0.0))))
            qaug_scr[rows, :] = q_aug.astype(BF16)
            kaug_scr[rows, :] = k_aug.astype(BF16)
            return cum[FOX_PREP_ROWS - 1:FOX_PREP_ROWS, :]

        lax.fori_loop(0, seq // FOX_PREP_ROWS, tile, jnp.zeros((1, LANES), F32))

    rows_all = N_HEADS * ATTN_TQ
    pair_rows = HEADS_PER_PAIR * ATTN_TQ
    lane = lax.broadcasted_iota(jnp.int32, (ATTN_TQ, LANES), 1)
    r = lax.broadcasted_iota(jnp.int32, (rows_all, ATTN_TK), 0) & (ATTN_TQ - 1)
    c = lax.broadcasted_iota(jnp.int32, (rows_all, ATTN_TK), 1)
    causal = c <= r
    query_blocks = [(u, FOX_QBLOCKS * step_id + u) for u in range(FOX_QBLOCKS)]

    def key_rows(j):
        return pl.ds(pl.multiple_of(j * ATTN_TK, ATTN_TK), ATTN_TK)

    def fold(x):
        return [x[:, n * LANES:(n + 1) * LANES] for n in range(ATTN_TK // LANES)]

    def make_scores(i):
        qrows = key_rows(i)
        q = qn_scr[qrows, :]
        qaug = qaug_scr[qrows, :]
        qs = []
        for p in range(N_PAIRS):
            rows = []
            for e in range(HEADS_PER_PAIR):
                h = HEADS_PER_PAIR * p + e
                aug_h = jnp.where((lane & (N_HEADS - 1)) == h, qaug, jnp.zeros_like(qaug))
                rows.append(jnp.concatenate([_masked_head(q, h), aug_h], axis=1))
            qs.append(jnp.concatenate(rows, axis=0))

        def scores(j):
            ks = key_rows(j)
            kaug = kaug_scr[ks, :]
            return jnp.concatenate(
                [_dot_nt(qs[p], jnp.concatenate([kn_scr[ks, p * LANES:(p + 1) * LANES], kaug], axis=1))
                 for p in range(N_PAIRS)], axis=0) * LOG2_E

        return scores

    def accumulate(blocks):
        l_parts, pv = [], None
        for j, shifted in blocks:
            ks = key_rows(j)
            p = jnp.exp2(shifted)
            l_parts += fold(p)
            pb = p.astype(BF16)
            contrib = jnp.concatenate(
                [_dot(pb[pr * pair_rows:(pr + 1) * pair_rows, :], v_ref[ks, pr * LANES:(pr + 1) * LANES])
                 for pr in range(N_PAIRS)], axis=0)
            pv = contrib if pv is None else pv + contrib
        return functools.reduce(jnp.add, l_parts), pv

    assert FOX_QBLOCKS % 2 == 0
    i0 = FOX_QBLOCKS * step_id
    head_blocks = [[i0 + v for v in range(u, -1, -1)] for u in range(FOX_QBLOCKS)]
    n_pair_steps = (FOX_QBLOCKS // 2) * step_id
    score_fns = [make_scores(i) for _, i in query_blocks]

    for (u, i), scores in zip(query_blocks, score_fns):
        parts = []
        for n, j in enumerate(head_blocks[u]):
            s = jnp.where(causal, scores(j), NEG_BIG) if n == 0 else scores(j)
            s_scr[u, j] = s
            parts += fold(s)
        mrun_scr[u] = functools.reduce(jnp.maximum, parts)

    def stage(t, _):
        for (u, _), scores in zip(query_blocks, score_fns):
            parts = []
            for j in (2 * t, 2 * t + 1):
                s = scores(j)
                s_scr[u, j] = s
                parts += fold(s)
            mrun_scr[u] = functools.reduce(jnp.maximum, parts, mrun_scr[u])
        return 0

    lax.fori_loop(0, n_pair_steps, stage, 0)

    row_max = []
    for u, _ in query_blocks:
        m = jnp.max(mrun_scr[u], axis=1, keepdims=True)
        l, pv = accumulate([(j, s_scr[u, j] - m) for j in head_blocks[u]])
        lrun_scr[u] = l
        acc_scr[u] = pv
        row_max.append(m)

    def weigh(t, _):
        for (u, _), m in zip(query_blocks, row_max):
            l, pv = accumulate([(j, s_scr[u, j] - m) for j in (2 * t, 2 * t + 1)])
            lrun_scr[u] += l
            acc_scr[u] += pv
        return 0

    lax.fori_loop(0, n_pair_steps, weigh, 0)

    for u, _ in query_blocks:
        out = acc_scr[u] / jnp.sum(lrun_scr[u], axis=1, keepdims=True)
        o_ref[u * ATTN_TQ:(u + 1) * ATTN_TQ, :] = _merge_heads(
            [out[h * ATTN_TQ:(h + 1) * ATTN_TQ, :] for h in range(N_HEADS)]).astype(BF16)


def _fox_attention(main3, f3, bf_row, qg_row, kg_row):
    b, s, _ = main3.shape
    rows_all = N_HEADS * ATTN_TQ
    step_rows = FOX_QBLOCKS * ATTN_TQ
    col = lambda k: pl.BlockSpec((None, s, MIX_WIDTH), lambda bi, i, k=k: (bi, 0, k))
    return pl.pallas_call(
        _fox_kernel,
        grid=(b, s // step_rows),
        in_specs=[col(COL_FOX_Q), col(COL_FOX_K), col(COL_FOX_V),
                  pl.BlockSpec((None, s, LANES), lambda bi, i: (bi, 0, 0)),
                  _resident((1, LANES)),
                  _resident((1, MIX_WIDTH)),
                  _resident((1, MIX_WIDTH))],
        out_specs=pl.BlockSpec((None, step_rows, MIX_WIDTH), lambda bi, i: (bi, i, 0)),
        out_shape=jax.ShapeDtypeStruct((b, s, MIX_WIDTH), BF16),
        scratch_shapes=[pltpu.VMEM((s, MIX_WIDTH), BF16),
                        pltpu.VMEM((s, MIX_WIDTH), BF16),
                        pltpu.VMEM((s, LANES), BF16),
                        pltpu.VMEM((s, LANES), BF16),
                        pltpu.VMEM((FOX_QBLOCKS, s // ATTN_TK, rows_all, ATTN_TK), F32),
                        pltpu.VMEM((FOX_QBLOCKS, rows_all, LANES), F32),
                        pltpu.VMEM((FOX_QBLOCKS, rows_all, LANES), F32),
                        pltpu.VMEM((FOX_QBLOCKS, rows_all, LANES), F32)],
        compiler_params=_params(2),
        name="fox",
    )(main3, main3, main3, f3, bf_row, qg_row, kg_row)


def _merge_kernel(h_ref, g_ref, yab_ref, yc_ref, yd_ref, wg_ref, wb_ref, wo_ref, o_ref, merged_scr):
    xn = _rms_norm_rows(h_ref[...], g_ref[...]).astype(BF16)
    ys = [yab_ref[:, 0:MIX_WIDTH], yab_ref[:, MIX_WIDTH:2 * MIX_WIDTH], yc_ref[...], yd_ref[...]]
    for c in range(D_MODEL // COL_CHUNK):
        cs = slice(c * COL_CHUNK, (c + 1) * COL_CHUNK)
        acc = None
        for n in range(N_BRANCH):
            gs = slice(n * D_MODEL + c * COL_CHUNK, n * D_MODEL + (c + 1) * COL_CHUNK)
            term = jax.nn.sigmoid(_dot(xn, wg_ref[:, gs])) * _dot(ys[n], wb_ref[n, :, cs])
            acc = term if acc is None else acc + term
        merged_scr[:, cs] = acc.astype(BF16)
    merged = merged_scr[...]
    for c in range(D_MODEL // COL_CHUNK):
        cs = slice(c * COL_CHUNK, (c + 1) * COL_CHUNK)
        o_ref[:, cs] = h_ref[:, cs] + _dot(merged, wo_ref[:, cs])


def _merge(h, g, yab, yc, yd, w_gate, w_branch, w_out, layer):
    t = h.shape[0]
    rows = lambda w: pl.BlockSpec((TOKEN_TILE, w), lambda i: (i, 0))
    return pl.pallas_call(
        _merge_kernel,
        grid=(t // TOKEN_TILE,),
        in_specs=[rows(D_MODEL), _resident((1, D_MODEL)),
                  rows(2 * MIX_WIDTH), rows(MIX_WIDTH), rows(MIX_WIDTH),
                  _resident_layer((D_MODEL, N_BRANCH * D_MODEL), layer),
                  _resident_layer((N_BRANCH, MIX_WIDTH, D_MODEL), layer),
                  _resident_layer((D_MODEL, D_MODEL), layer)],
        out_specs=rows(D_MODEL),
        out_shape=jax.ShapeDtypeStruct((t, D_MODEL), F32),
        scratch_shapes=[pltpu.VMEM((TOKEN_TILE, D_MODEL), BF16)],
        compiler_params=_params(1),
        name="merge",
    )(h, g, yab, yc, yd, w_gate, w_branch, w_out)


def _ffn_kernel(h_ref, g_ref, wi_ref, wo_ref, o_ref, acc_scr):
    xn = _rms_norm_rows(h_ref[...], g_ref[...]).astype(BF16)
    for c in range(FFN_HIDDEN // COL_CHUNK):
        gate = _dot(xn, wi_ref[:, c * COL_CHUNK:(c + 1) * COL_CHUNK])
        up = _dot(xn, wi_ref[:, FFN_HIDDEN + c * COL_CHUNK:FFN_HIDDEN + (c + 1) * COL_CHUNK])
        act = (jax.nn.silu(gate) * up).astype(BF16)
        part = _dot(act, wo_ref[c * COL_CHUNK:(c + 1) * COL_CHUNK, :])
        if c == 0:
            acc_scr[...] = part
        else:
            acc_scr[...] += part
    o_ref[...] = h_ref[...] + acc_scr[...]


def _ffn(h, g, w_in, w_out, layer):
    t = h.shape[0]
    rows = pl.BlockSpec((TOKEN_TILE, D_MODEL), lambda i: (i, 0))
    return pl.pallas_call(
        _ffn_kernel,
        grid=(t // TOKEN_TILE,),
        in_specs=[rows, _resident((1, D_MODEL)),
                  _resident_layer((D_MODEL, 2 * FFN_HIDDEN), layer),
                  _resident_layer((FFN_HIDDEN, D_MODEL), layer)],
        out_specs=rows,
        out_shape=jax.ShapeDtypeStruct((t, D_MODEL), F32),
        scratch_shapes=[pltpu.VMEM((TOKEN_TILE, D_MODEL), F32)],
        compiler_params=_params(1),
        name="ffn",
    )(h, g, w_in, w_out)


def kernel(x, norm_mix_g, w_in, w_conv, w_spatial, b_spatial, gmlp_ln_g, gmlp_ln_b,
           fox_q_norm_g, fox_k_norm_g, fox_forget_b, w_branch, w_out, norm_ffn_g,
           w_ffn_in, w_ffn_out):
    b, s, d = x.shape
    depth = w_in.shape[0]
    assert d == D_MODEL and ATTN_TQ == ATTN_TK
    for rows in (SB_QBLOCKS * ATTN_TQ, FOX_QBLOCKS * ATTN_TQ, GMLP_CHUNKS_PER_STEP * GMLP_CHUNK, FOX_PREP_ROWS,
                 TOKEN_TILE):
        assert s % rows == 0, (s, rows)
    t = b * s
    h = x.reshape(t, d)
    w_in_b = jnp.transpose(lax.optimization_barrier(jnp.transpose(w_in, (2, 0, 1)).astype(BF16)), (1, 2, 0))
    w_gate_b = w_in_b[:, :, N_MAIN + N_HEADS:]
    w_branch_b = w_branch.astype(BF16)
    w_out_b = w_out.astype(BF16)
    w_ffn_in_b = w_ffn_in.astype(BF16)
    w_ffn_out_b = w_ffn_out.astype(BF16)
    n_f = FORGET_COPIES * N_HEADS
    for l in range(depth):
        w_f = jnp.pad(jnp.tile(w_in_b[l, :, N_MAIN:N_MAIN + N_HEADS], (1, FORGET_COPIES)),
                      ((0, 0), (0, LANES - n_f)))
        bf_row = jnp.pad(jnp.tile(fox_forget_b[l], FORGET_COPIES), (0, LANES - n_f)).reshape(1, LANES)
        qg_row = jnp.tile(fox_q_norm_g[l], N_HEADS).reshape(1, MIX_WIDTH)
        kg_row = jnp.tile(fox_k_norm_g[l], N_HEADS).reshape(1, MIX_WIDTH)
        bs_rows = jnp.repeat(b_spatial[l].T, GROUP_WIDTH, axis=1)

        main, f_raw = _proj(h, norm_mix_g[l].reshape(1, d), w_in_b, l, w_f)
        main3 = main.reshape(b, s, N_MAIN)
        yab = _convgmlp(main3, w_conv[l], w_spatial[l], bs_rows,
                        gmlp_ln_g[l].reshape(1, MIX_WIDTH), gmlp_ln_b[l].reshape(1, MIX_WIDTH))
        yc = _sb_attention(main3)
        yd = _fox_attention(main3, f_raw.reshape(b, s, LANES), bf_row, qg_row, kg_row)
        h = _merge(h, norm_mix_g[l].reshape(1, d), yab.reshape(t, 2 * MIX_WIDTH),
                   yc.reshape(t, MIX_WIDTH), yd.reshape(t, MIX_WIDTH),
                   w_gate_b, w_branch_b, w_out_b, l)
        h = _ffn(h, norm_ffn_g[l].reshape(1, d), w_ffn_in_b, w_ffn_out_b, l)
    return h.reshape(b, s, d)
```

```python
import functools

import jax
import jax.numpy as jnp
from jax import lax
from jax.experimental import pallas as pl
from jax.experimental.pallas import tpu as pltpu

D_MODEL = 1024
MIX_WIDTH = 256
HEAD_DIM = 64
N_HEADS = MIX_WIDTH // HEAD_DIM
N_BRANCH = 4
CONV_K = 3
GMLP_GROUPS = 4
GMLP_CHUNK = 128
GROUP_WIDTH = MIX_WIDTH // GMLP_GROUPS
FFN_HIDDEN = 2816
EPS = 1e-6
N_MAIN = 11 * MIX_WIDTH
(COL_CONV_B, COL_CONV_C, COL_CONV_X, COL_GMLP_U, COL_GMLP_V, COL_SB_Q, COL_SB_K, COL_SB_V,
 COL_FOX_Q, COL_FOX_K, COL_FOX_V) = range(11)
LANES = 128
HEADS_PER_PAIR = LANES // HEAD_DIM
N_PAIRS = MIX_WIDTH // LANES
FORGET_COPIES = 6
FOX_PREP_ROWS = 256
GMLP_CHUNKS_PER_STEP = 4
SUBLANES = 8
VMEM_LIMIT_BYTES = 56 * 1024 * 1024

TOKEN_TILE = 1024
COL_CHUNK = 256
ATTN_TQ = 256
ATTN_TK = 256
SB_QBLOCKS = 8
FOX_QBLOCKS = 4
SCALE = HEAD_DIM ** -0.5
LOG2_E = 1.4426950408889634
SB_Q_SCALE = -SCALE * LOG2_E
LOG2_F32_UNDERFLOW = -151.0
NEG_BIG = -1e30

F32 = jnp.float32
BF16 = jnp.bfloat16


def _dot(a, b):
    return jnp.dot(a, b, preferred_element_type=F32)


def _dot_nt(a, b):
    return lax.dot_general(a, b, (((1,), (1,)), ((), ())), preferred_element_type=F32)


def _split_hi_lo(x):
    hi = x.astype(BF16)
    lo = (x - hi.astype(F32)).astype(BF16)
    return hi, lo


def _rms_norm_rows(x, g):
    ms = jnp.mean(x * x, axis=-1, keepdims=True)
    return x * lax.rsqrt(ms + EPS) * g


def _resident(shape):
    return pl.BlockSpec(shape, lambda *_: (0,) * len(shape), pipeline_mode=pl.Buffered(1))


def _resident_layer(shape, layer):
    return pl.BlockSpec((None,) + tuple(shape), lambda *_: (layer,) + (0,) * len(shape),
                        pipeline_mode=pl.Buffered(1))


def _params(n_axes):
    return pltpu.CompilerParams(dimension_semantics=("arbitrary",) * n_axes,
                                vmem_limit_bytes=VMEM_LIMIT_BYTES)


def _split3(x):
    hi = x.astype(BF16).astype(F32)
    rem = x - hi
    mid = rem.astype(BF16).astype(F32)
    return hi, mid, rem - mid


def _proj_kernel(h_ref, g_ref, wm_ref, wf_ref, main_ref, f_ref):
    xn = _rms_norm_rows(h_ref[...], g_ref[...]).astype(BF16)
    for c in range(N_MAIN // COL_CHUNK):
        cs = slice(c * COL_CHUNK, (c + 1) * COL_CHUNK)
        y = _dot(xn, wm_ref[:, cs])
        if c == COL_SB_Q:
            y = y * SB_Q_SCALE
        main_ref[:, cs] = y.astype(BF16)
    f_ref[...] = _dot(xn, wf_ref[...])


def _proj(h, g, w_in_all, layer, w_f):
    t = h.shape[0]
    return pl.pallas_call(
        _proj_kernel,
        grid=(t // TOKEN_TILE,),
        in_specs=[pl.BlockSpec((TOKEN_TILE, D_MODEL), lambda i: (i, 0)),
                  _resident((1, D_MODEL)),
                  _resident_layer((D_MODEL, N_MAIN), layer),
                  _resident((D_MODEL, LANES))],
        out_specs=[pl.BlockSpec((TOKEN_TILE, N_MAIN), lambda i: (i, 0)),
                   pl.BlockSpec((TOKEN_TILE, LANES), lambda i: (i, 0))],
        out_shape=[jax.ShapeDtypeStruct((t, N_MAIN), BF16),
                   jax.ShapeDtypeStruct((t, LANES), F32)],
        compiler_params=_params(1),
        name="proj",
    )(h, g, w_in_all, w_f)


def _convgmlp_kernel(cb_ref, cc_ref, cx_ref, u_ref, v_ref, wconv_ref, ws_ref, bs_ref,
                     lng_ref, lnb_ref, o_ref):
    seq = cb_ref.shape[0]
    n_chunks = seq // GMLP_CHUNK
    row = lax.broadcasted_iota(jnp.int32, (GMLP_CHUNK, GMLP_CHUNK), 0)
    col = lax.broadcasted_iota(jnp.int32, (GMLP_CHUNK, GMLP_CHUNK), 1)
    w_tril = [jnp.where(col <= row, ws_ref[gi], 0.0).astype(BF16) for gi in range(GMLP_GROUPS)]
    lane = lax.broadcasted_iota(jnp.int32, (GMLP_CHUNK, LANES), 1)
    first_group = lane < GROUP_WIDTH
    w0 = wconv_ref[0:1, :]
    w1 = wconv_ref[1:2, :]
    w2 = wconv_ref[2:3, :]

    def chunk(c, prev_tail):
        rows = pl.ds(pl.multiple_of(c * GMLP_CHUNK, GMLP_CHUNK), GMLP_CHUNK)
        xc = cc_ref[rows, :].astype(F32) * cx_ref[rows, :].astype(F32)
        win = jnp.concatenate([prev_tail, xc], axis=0)
        xc1 = pltpu.roll(win, 1, 0)[SUBLANES:, :]
        xc2 = pltpu.roll(win, 2, 0)[SUBLANES:, :]
        ya = cb_ref[rows, :].astype(F32) * (w0 * xc2 + w1 * xc1 + w2 * xc)
        o_ref[rows, 0:MIX_WIDTH] = ya.astype(BF16)

        gu = jax.nn.gelu(u_ref[rows, :].astype(F32))
        gv = jax.nn.gelu(v_ref[rows, :].astype(F32))
        mu = jnp.mean(gv, axis=-1, keepdims=True)
        cen = gv - mu
        var = jnp.mean(cen * cen, axis=-1, keepdims=True)
        vn = (cen * lax.rsqrt(var + EPS) * lng_ref[...] + lnb_ref[...]).astype(BF16)
        halves = []
        for lb in range(MIX_WIDTH // LANES):
            vb = vn[:, lb * LANES:(lb + 1) * LANES]
            m0 = _dot(w_tril[2 * lb], vb)
            m1 = _dot(w_tril[2 * lb + 1], vb)
            halves.append(jnp.where(first_group, m0, m1))
        mixed = jnp.concatenate(halves, axis=1) + bs_ref[...]
        o_ref[rows, MIX_WIDTH:2 * MIX_WIDTH] = (gu * mixed).astype(BF16)
        return xc[GMLP_CHUNK - SUBLANES:, :]

    def step(c0, tail):
        for n in range(GMLP_CHUNKS_PER_STEP):
            tail = chunk(GMLP_CHUNKS_PER_STEP * c0 + n, tail)
        return tail

    lax.fori_loop(0, n_chunks // GMLP_CHUNKS_PER_STEP, step, jnp.zeros((SUBLANES, MIX_WIDTH), F32))


def _convgmlp(main3, w_conv, w_s, bs_rows, ln_g, ln_b):
    b, s, _ = main3.shape
    col = lambda k: pl.BlockSpec((None, s, MIX_WIDTH), lambda i, k=k: (i, 0, k))
    return pl.pallas_call(
        _convgmlp_kernel,
        grid=(b,),
        in_specs=[col(COL_CONV_B), col(COL_CONV_C), col(COL_CONV_X), col(COL_GMLP_U), col(COL_GMLP_V),
                  _resident((CONV_K, MIX_WIDTH)),
                  _resident((GMLP_GROUPS, GMLP_CHUNK, GMLP_CHUNK)),
                  _resident((GMLP_CHUNK, MIX_WIDTH)),
                  _resident((1, MIX_WIDTH)),
                  _resident((1, MIX_WIDTH))],
        out_specs=pl.BlockSpec((None, s, 2 * MIX_WIDTH), lambda i: (i, 0, 0)),
        out_shape=jax.ShapeDtypeStruct((b, s, 2 * MIX_WIDTH), BF16),
        compiler_params=_params(1),
        name="convgmlp",
    )(main3, main3, main3, main3, main3, w_conv, w_s, bs_rows, ln_g, ln_b)


def _pair(x, h):
    p = (h * HEAD_DIM) // LANES
    return x[:, p * LANES:(p + 1) * LANES]


def _head_in_pair_mask(rows, h):
    lane = lax.broadcasted_iota(jnp.int32, (rows, LANES), 1)
    first = lane < HEAD_DIM
    return first if (h * HEAD_DIM) % LANES == 0 else jnp.logical_not(first)


def _masked_head(x, h):
    xp = _pair(x, h)
    return jnp.where(_head_in_pair_mask(x.shape[0], h), xp, jnp.zeros_like(xp))


def _merge_heads(per_head):
    rows = per_head[0].shape[0]
    blocks = []
    for p in range(MIX_WIDTH // LANES):
        h0 = p * (LANES // HEAD_DIM)
        blocks.append(jnp.where(_head_in_pair_mask(rows, h0), per_head[h0], per_head[h0 + 1]))
    return jnp.concatenate(blocks, axis=1)


def _sb_kernel(q_ref, k_ref, v_ref, o_ref, carry_scr, acc_scr):
    step_id = pl.program_id(1)
    r = lax.broadcasted_iota(jnp.int32, (ATTN_TK, ATTN_TK), 0)
    c = lax.broadcasted_iota(jnp.int32, (ATTN_TK, ATTN_TK), 1)
    suffix = jnp.where(r > c, 1.0, 0.0).astype(BF16)
    rows_all = N_HEADS * ATTN_TQ
    rq = lax.broadcasted_iota(jnp.int32, (rows_all, ATTN_TK), 0) & (ATTN_TQ - 1)
    cq = lax.broadcasted_iota(jnp.int32, (rows_all, ATTN_TK), 1)
    strict = cq < rq
    pair_rows = HEADS_PER_PAIR * ATTN_TQ

    def make_group(u, i):
        q = q_ref[pl.ds(pl.multiple_of(i * ATTN_TQ, ATTN_TQ), ATTN_TQ), :]
        qneg = [jnp.concatenate([_masked_head(q, HEADS_PER_PAIR * p + e) for e in range(HEADS_PER_PAIR)],
                                axis=0) for p in range(N_PAIRS)]

        def group(blocks, first):
            carry = jnp.zeros((rows_all, 1), F32) if first else carry_scr[u]
            staged = []
            for j, valid, diag in blocks:
                ks = pl.ds(pl.multiple_of(j * ATTN_TK, ATTN_TK), ATTN_TK)
                k = k_ref[ks, :]
                zn2 = jnp.concatenate(
                    [_dot_nt(qneg[p], k[:, p * LANES:(p + 1) * LANES]) for p in range(N_PAIRS)], axis=0)
                l2 = jnp.minimum(zn2, 0.0) - jnp.log2(1.0 + jnp.exp2(-jnp.abs(zn2)))
                if diag:
                    l2 = jnp.where(strict, l2, 0.0)
                staged.append((ks, valid, diag, l2 - zn2, _dot(l2.astype(BF16), suffix),
                               jnp.sum(l2, axis=1, keepdims=True)))
            pv = None
            for ks, valid, diag, log2_beta, later_in, row_sum in staged:
                if valid is None:
                    w = jnp.exp2(later_in + carry + log2_beta)
                else:
                    w = jnp.exp2(later_in + (carry + jnp.where(valid, 0.0, NEG_BIG)) + log2_beta)
                    row_sum = jnp.where(valid, row_sum, 0.0)
                if diag:
                    w = jnp.where(strict, w, 0.0)
                wb = w.astype(BF16)
                v = v_ref[ks, :]
                contrib = jnp.concatenate(
                    [_dot(wb[p * pair_rows:(p + 1) * pair_rows, :], v[:, p * LANES:(p + 1) * LANES])
                     for p in range(N_PAIRS)], axis=0)
                pv = contrib if pv is None else pv + contrib
                carry = carry + row_sum
            if first:
                acc_scr[u] = pv
            else:
                acc_scr[u] += pv
            carry_scr[u] = carry

        return group

    query_blocks = [(u, SB_QBLOCKS * step_id + u) for u in range(SB_QBLOCKS)]
    groups = [make_group(u, i) for u, i in query_blocks]
    for (u, i), group in zip(query_blocks, groups):
        group([(i, None, True), (jnp.maximum(i - 1, 0), i >= 1, False)], True)

    for (u, i), group in zip(query_blocks, groups):
        def alive(u=u):
            return jnp.max(carry_scr[u]) > LOG2_F32_UNDERFLOW

        first_left = i - 2
        n_groups = i // 2

        def step(st, group=group, alive=alive, first_left=first_left):
            t, _ = st
            j_near = first_left - 2 * t
            group([(j_near, None, False), (jnp.maximum(j_near - 1, 0), j_near >= 1, False)], False)
            return t + 1, alive()

        lax.while_loop(lambda st, n_groups=n_groups: (st[0] < n_groups) & st[1], step, (jnp.int32(0), alive()))
        acc = acc_scr[u]
        o_ref[u * ATTN_TQ:(u + 1) * ATTN_TQ, :] = _merge_heads(
            [acc[h * ATTN_TQ:(h + 1) * ATTN_TQ, :] for h in range(N_HEADS)]).astype(BF16)


def _sb_attention(main3):
    b, s, _ = main3.shape
    step_rows = SB_QBLOCKS * ATTN_TQ
    col = lambda k: pl.BlockSpec((None, s, MIX_WIDTH), lambda bi, i, k=k: (bi, 0, k))
    return pl.pallas_call(
        _sb_kernel,
        grid=(b, s // step_rows),
        in_specs=[col(COL_SB_Q), col(COL_SB_K), col(COL_SB_V)],
        out_specs=pl.BlockSpec((None, step_rows, MIX_WIDTH), lambda bi, i: (bi, i, 0)),
        out_shape=jax.ShapeDtypeStruct((b, s, MIX_WIDTH), BF16),
        scratch_shapes=[pltpu.VMEM((SB_QBLOCKS, N_HEADS * ATTN_TQ, 1), F32),
                        pltpu.VMEM((SB_QBLOCKS, N_HEADS * ATTN_TQ, LANES), F32)],
        compiler_params=_params(2),
        name="sb",
    )(main3, main3, main3)


def _fox_kernel(q_ref, k_ref, v_ref, f_ref, bf_ref, qg_ref, kg_ref, o_ref,
                qn_scr, kn_scr, qaug_scr, kaug_scr, s_scr, mrun_scr, lrun_scr, acc_scr):
    step_id = pl.program_id(1)
    seq = q_ref.shape[0]

    @pl.when(step_id == 0)
    def _prepare():
        r = lax.broadcasted_iota(jnp.int32, (MIX_WIDTH, MIX_WIDTH), 0) // HEAD_DIM
        c = lax.broadcasted_iota(jnp.int32, (MIX_WIDTH, MIX_WIDTH), 1) // HEAD_DIM
        same_head = jnp.where(r == c, 1.0, 0.0).astype(BF16)
        tr = lax.broadcasted_iota(jnp.int32, (FOX_PREP_ROWS, FOX_PREP_ROWS), 0)
        tc = lax.broadcasted_iota(jnp.int32, (FOX_PREP_ROWS, FOX_PREP_ROWS), 1)
        prefix = jnp.where(tc <= tr, 1.0, 0.0).astype(BF16)
        lane = lax.broadcasted_iota(jnp.int32, (FOX_PREP_ROWS, LANES), 1)
        n = N_HEADS

        def tile(tix, run):
            rows = pl.ds(pl.multiple_of(tix * FOX_PREP_ROWS, FOX_PREP_ROWS), FOX_PREP_ROWS)
            for x_ref, g_ref, scale, dst in ((q_ref, qg_ref, SCALE, qn_scr), (k_ref, kg_ref, 1.0, kn_scr)):
                x = x_ref[rows, :].astype(F32)
                hi, lo = _split_hi_lo(x * x)
                ms = (_dot(hi, same_head) + _dot(lo, same_head)) * (1.0 / HEAD_DIM)
                dst[rows, :] = (x * lax.rsqrt(ms + EPS) * (g_ref[...] * scale)).astype(BF16)
            hi, lo = _split_hi_lo(jax.nn.log_sigmoid(f_ref[rows, :] + bf_ref[...]))
            cum = run + _dot(prefix, hi) + _dot(prefix, lo)
            c_hi, c_mid, c_lo = _split3(cum)
            q_aug = jnp.where(lane < n, c_hi, jnp.where(lane < 2 * n, c_mid, jnp.where(
                lane < 3 * n, c_lo, jnp.where(lane < 6 * n, 1.0, 0.0))))
            k_aug = jnp.where(lane < 3 * n, 1.0, jnp.where(lane < 4 * n, -c_hi, jnp.where(
                lane < 5 * n, -c_mid, jnp.where(lane < 6 * n, -c_lo, 0.0))))
            qaug_scr[rows, :] = q_aug.astype(BF16)
            kaug_scr[rows, :] = k_aug.astype(BF16)
            return cum[FOX_PREP_ROWS - 1:FOX_PREP_ROWS, :]

        lax.fori_loop(0, seq // FOX_PREP_ROWS, tile, jnp.zeros((1, LANES), F32))

    rows_all = N_HEADS * ATTN_TQ
    pair_rows = HEADS_PER_PAIR * ATTN_TQ
    lane = lax.broadcasted_iota(jnp.int32, (ATTN_TQ, LANES), 1)
    r = lax.broadcasted_iota(jnp.int32, (rows_all, ATTN_TK), 0) & (ATTN_TQ - 1)
    c = lax.broadcasted_iota(jnp.int32, (rows_all, ATTN_TK), 1)
    causal = c <= r
    query_blocks = [(u, FOX_QBLOCKS * step_id + u) for u in range(FOX_QBLOCKS)]

    def key_rows(j):
        return pl.ds(pl.multiple_of(j * ATTN_TK, ATTN_TK), ATTN_TK)

    def fold(x):
        return [x[:, n * LANES:(n + 1) * LANES] for n in range(ATTN_TK // LANES)]

    def make_scores(i):
        qrows = key_rows(i)
        q = qn_scr[qrows, :]
        qaug = qaug_scr[qrows, :]
        qs = []
        for p in range(N_PAIRS):
            rows = []
            for e in range(HEADS_PER_PAIR):
                h = HEADS_PER_PAIR * p + e
                aug_h = jnp.where((lane & (N_HEADS - 1)) == h, qaug, jnp.zeros_like(qaug))
                rows.append(jnp.concatenate([_masked_head(q, h), aug_h], axis=1))
            qs.append(jnp.concatenate(rows, axis=0))

        def scores(j):
            ks = key_rows(j)
            kaug = kaug_scr[ks, :]
            return jnp.concatenate(
                [_dot_nt(qs[p], jnp.concatenate([kn_scr[ks, p * LANES:(p + 1) * LANES], kaug], axis=1))
                 for p in range(N_PAIRS)], axis=0) * LOG2_E

        return scores

    def accumulate(blocks):
        l_parts, pv = [], None
        for j, shifted in blocks:
            ks = key_rows(j)
            p = jnp.exp2(shifted)
            l_parts += fold(p)
            pb = p.astype(BF16)
            contrib = jnp.concatenate(
                [_dot(pb[pr * pair_rows:(pr + 1) * pair_rows, :], v_ref[ks, pr * LANES:(pr + 1) * LANES])
                 for pr in range(N_PAIRS)], axis=0)
            pv = contrib if pv is None else pv + contrib
        return functools.reduce(jnp.add, l_parts), pv

    assert FOX_QBLOCKS % 2 == 0
    i0 = FOX_QBLOCKS * step_id
    head_blocks = [[i0 + v for v in range(u, -1, -1)] for u in range(FOX_QBLOCKS)]
    n_pair_steps = (FOX_QBLOCKS // 2) * step_id
    score_fns = [make_scores(i) for _, i in query_blocks]

    for (u, i), scores in zip(query_blocks, score_fns):
        parts = []
        for n, j in enumerate(head_blocks[u]):
            s = jnp.where(causal, scores(j), NEG_BIG) if n == 0 else scores(j)
            s_scr[u, j] = s
            parts += fold(s)
        mrun_scr[u] = functools.reduce(jnp.maximum, parts)

    def stage(t, _):
        for (u, _), scores in zip(query_blocks, score_fns):
            parts = []
            for j in (2 * t, 2 * t + 1):
                s = scores(j)
                s_scr[u, j] = s
                parts += fold(s)
            mrun_scr[u] = functools.reduce(jnp.maximum, parts, mrun_scr[u])
        return 0

    lax.fori_loop(0, n_pair_steps, stage, 0)

    row_max = []
    for u, _ in query_blocks:
        m = jnp.max(mrun_scr[u], axis=1, keepdims=True)
        l, pv = accumulate([(j, s_scr[u, j] - m) for j in head_blocks[u]])
        lrun_scr[u] = l
        acc_scr[u] = pv
        row_max.append(m)

    def weigh(t, _):
        for (u, _), m in zip(query_blocks, row_max):
            l, pv = accumulate([(j, s_scr[u, j] - m) for j in (2 * t, 2 * t + 1)])
            lrun_scr[u] += l
            acc_scr[u] += pv
        return 0

    lax.fori_loop(0, n_pair_steps, weigh, 0)

    for u, _ in query_blocks:
        out = acc_scr[u] / jnp.sum(lrun_scr[u], axis=1, keepdims=True)
        o_ref[u * ATTN_TQ:(u + 1) * ATTN_TQ, :] = _merge_heads(
            [out[h * ATTN_TQ:(h + 1) * ATTN_TQ, :] for h in range(N_HEADS)]).astype(BF16)


def _fox_attention(main3, f3, bf_row, qg_row, kg_row):
    b, s, _ = main3.shape
    rows_all = N_HEADS * ATTN_TQ
    step_rows = FOX_QBLOCKS * ATTN_TQ
    col = lambda k: pl.BlockSpec((None, s, MIX_WIDTH), lambda bi, i, k=k: (bi, 0, k))
    return pl.pallas_call(
        _fox_kernel,
        grid=(b, s // step_rows),
        in_specs=[col(COL_FOX_Q), col(COL_FOX_K), col(COL_FOX_V),
                  pl.BlockSpec((None, s, LANES), lambda bi, i: (bi, 0, 0)),
                  _resident((1, LANES)),
                  _resident((1, MIX_WIDTH)),
                  _resident((1, MIX_WIDTH))],
        out_specs=pl.BlockSpec((None, step_rows, MIX_WIDTH), lambda bi, i: (bi, i, 0)),
        out_shape=jax.ShapeDtypeStruct((b, s, MIX_WIDTH), BF16),
        scratch_shapes=[pltpu.VMEM((s, MIX_WIDTH), BF16),
                        pltpu.VMEM((s, MIX_WIDTH), BF16),
                        pltpu.VMEM((s, LANES), BF16),
                        pltpu.VMEM((s, LANES), BF16),
                        pltpu.VMEM((FOX_QBLOCKS, s // ATTN_TK, rows_all, ATTN_TK), F32),
                        pltpu.VMEM((FOX_QBLOCKS, rows_all, LANES), F32),
                        pltpu.VMEM((FOX_QBLOCKS, rows_all, LANES), F32),
                        pltpu.VMEM((FOX_QBLOCKS, rows_all, LANES), F32)],
        compiler_params=_params(2),
        name="fox",
    )(main3, main3, main3, f3, bf_row, qg_row, kg_row)


def _merge_kernel(h_ref, g_ref, yab_ref, yc_ref, yd_ref, wg_ref, wb_ref, wo_ref, o_ref, merged_scr):
    xn = _rms_norm_rows(h_ref[...], g_ref[...]).astype(BF16)
    ys = [yab_ref[:, 0:MIX_WIDTH], yab_ref[:, MIX_WIDTH:2 * MIX_WIDTH], yc_ref[...], yd_ref[...]]
    for c in range(D_MODEL // COL_CHUNK):
        cs = slice(c * COL_CHUNK, (c + 1) * COL_CHUNK)
        acc = None
        for n in range(N_BRANCH):
            gs = slice(n * D_MODEL + c * COL_CHUNK, n * D_MODEL + (c + 1) * COL_CHUNK)
            term = jax.nn.sigmoid(_dot(xn, wg_ref[:, gs])) * _dot(ys[n], wb_ref[n, :, cs])
            acc = term if acc is None else acc + term
        merged_scr[:, cs] = acc.astype(BF16)
    merged = merged_scr[...]
    for c in range(D_MODEL // COL_CHUNK):
        cs = slice(c * COL_CHUNK, (c + 1) * COL_CHUNK)
        o_ref[:, cs] = h_ref[:, cs] + _dot(merged, wo_ref[:, cs])


def _merge(h, g, yab, yc, yd, w_gate, w_branch, w_out, layer):
    t = h.shape[0]
    rows = lambda w: pl.BlockSpec((TOKEN_TILE, w), lambda i: (i, 0))
    return pl.pallas_call(
        _merge_kernel,
        grid=(t // TOKEN_TILE,),
        in_specs=[rows(D_MODEL), _resident((1, D_MODEL)),
                  rows(2 * MIX_WIDTH), rows(MIX_WIDTH), rows(MIX_WIDTH),
                  _resident_layer((D_MODEL, N_BRANCH * D_MODEL), layer),
                  _resident_layer((N_BRANCH, MIX_WIDTH, D_MODEL), layer),
                  _resident_layer((D_MODEL, D_MODEL), layer)],
        out_specs=rows(D_MODEL),
        out_shape=jax.ShapeDtypeStruct((t, D_MODEL), F32),
        scratch_shapes=[pltpu.VMEM((TOKEN_TILE, D_MODEL), BF16)],
        compiler_params=_params(1),
        name="merge",
    )(h, g, yab, yc, yd, w_gate, w_branch, w_out)


def _ffn_kernel(h_ref, g_ref, wi_ref, wo_ref, o_ref, act_scr):
    xn = _rms_norm_rows(h_ref[...], g_ref[...]).astype(BF16)
    for c in range(FFN_HIDDEN // COL_CHUNK):
        cs = slice(c * COL_CHUNK, (c + 1) * COL_CHUNK)
        gate = _dot(xn, wi_ref[:, cs])
        up = _dot(xn, wi_ref[:, FFN_HIDDEN + c * COL_CHUNK:FFN_HIDDEN + (c + 1) * COL_CHUNK])
        act_scr[:, cs] = (jax.nn.silu(gate) * up).astype(BF16)
    act = act_scr[...]
    for c in range(D_MODEL // COL_CHUNK):
        cs = slice(c * COL_CHUNK, (c + 1) * COL_CHUNK)
        o_ref[:, cs] = h_ref[:, cs] + _dot(act, wo_ref[:, cs])


def _ffn(h, g, w_in, w_out, layer):
    t = h.shape[0]
    rows = pl.BlockSpec((TOKEN_TILE, D_MODEL), lambda i: (i, 0))
    return pl.pallas_call(
        _ffn_kernel,
        grid=(t // TOKEN_TILE,),
        in_specs=[rows, _resident((1, D_MODEL)),
                  _resident_layer((D_MODEL, 2 * FFN_HIDDEN), layer),
                  _resident_layer((FFN_HIDDEN, D_MODEL), layer)],
        out_specs=rows,
        out_shape=jax.ShapeDtypeStruct((t, D_MODEL), F32),
        scratch_shapes=[pltpu.VMEM((TOKEN_TILE, FFN_HIDDEN), BF16)],
        compiler_params=_params(1),
        name="ffn",
    )(h, g, w_in, w_out)


def kernel(x, norm_mix_g, w_in, w_conv, w_spatial, b_spatial, gmlp_ln_g, gmlp_ln_b,
           fox_q_norm_g, fox_k_norm_g, fox_forget_b, w_branch, w_out, norm_ffn_g,
           w_ffn_in, w_ffn_out):
    b, s, d = x.shape
    depth = w_in.shape[0]
    assert d == D_MODEL and ATTN_TQ == ATTN_TK
    for rows in (SB_QBLOCKS * ATTN_TQ, FOX_QBLOCKS * ATTN_TQ, GMLP_CHUNKS_PER_STEP * GMLP_CHUNK, FOX_PREP_ROWS,
                 TOKEN_TILE):
        assert s % rows == 0, (s, rows)
    t = b * s
    h = x.reshape(t, d)
    w_in_b = jnp.transpose(lax.optimization_barrier(jnp.transpose(w_in, (2, 0, 1)).astype(BF16)), (1, 2, 0))
    w_gate_b = w_in_b[:, :, N_MAIN + N_HEADS:]
    w_branch_b = w_branch.astype(BF16)
    w_out_b = w_out.astype(BF16)
    w_ffn_in_b = w_ffn_in.astype(BF16)
    w_ffn_out_b = w_ffn_out.astype(BF16)
    n_f = FORGET_COPIES * N_HEADS
    for l in range(depth):
        w_f = jnp.pad(jnp.tile(w_in_b[l, :, N_MAIN:N_MAIN + N_HEADS], (1, FORGET_COPIES)),
                      ((0, 0), (0, LANES - n_f)))
        bf_row = jnp.pad(jnp.tile(fox_forget_b[l], FORGET_COPIES), (0, LANES - n_f)).reshape(1, LANES)
        qg_row = jnp.tile(fox_q_norm_g[l], N_HEADS).reshape(1, MIX_WIDTH)
        kg_row = jnp.tile(fox_k_norm_g[l], N_HEADS).reshape(1, MIX_WIDTH)
        bs_rows = jnp.repeat(b_spatial[l].T, GROUP_WIDTH, axis=1)

        main, f_raw = _proj(h, norm_mix_g[l].reshape(1, d), w_in_b, l, w_f)
        main3 = main.reshape(b, s, N_MAIN)
        yab = _convgmlp(main3, w_conv[l], w_spatial[l], bs_rows,
                        gmlp_ln_g[l].reshape(1, MIX_WIDTH), gmlp_ln_b[l].reshape(1, MIX_WIDTH))
        yc = _sb_attention(main3)
        yd = _fox_attention(main3, f_raw.reshape(b, s, LANES), bf_row, qg_row, kg_row)
        h = _merge(h, norm_mix_g[l].reshape(1, d), yab.reshape(t, 2 * MIX_WIDTH),
                   yc.reshape(t, MIX_WIDTH), yd.reshape(t, MIX_WIDTH),
                   w_gate_b, w_branch_b, w_out_b, l)
        h = _ffn(h, norm_ffn_g[l].reshape(1, d), w_ffn_in_b, w_ffn_out_b, l)
    return h.reshape(b, s, d)
```

```python
import functools

import jax
import jax.numpy as jnp
from jax import lax
from jax.experimental import pallas as pl
from jax.experimental.pallas import tpu as pltpu

D_MODEL = 1024
MIX_WIDTH = 256
HEAD_DIM = 64
N_HEADS = MIX_WIDTH // HEAD_DIM
N_BRANCH = 4
CONV_K = 3
GMLP_GROUPS = 4
GMLP_CHUNK = 128
GROUP_WIDTH = MIX_WIDTH // GMLP_GROUPS
FFN_HIDDEN = 2816
EPS = 1e-6
N_MAIN = 11 * MIX_WIDTH
(COL_CONV_B, COL_CONV_C, COL_CONV_X, COL_GMLP_U, COL_GMLP_V, COL_SB_Q, COL_SB_K, COL_SB_V,
 COL_FOX_Q, COL_FOX_K, COL_FOX_V) = range(11)
LANES = 128
HEADS_PER_PAIR = LANES // HEAD_DIM
N_PAIRS = MIX_WIDTH // LANES
FORGET_COPIES = 6
FOX_PREP_ROWS = 256
GMLP_CHUNKS_PER_STEP = 4
SUBLANES = 8
VMEM_LIMIT_BYTES = 56 * 1024 * 1024

TOKEN_TILE = 1024
COL_CHUNK = 256
ATTN_TQ = 256
ATTN_TK = 256
SB_QBLOCKS = 8
FOX_QBLOCKS = 4
SCALE = HEAD_DIM ** -0.5
LOG2_E = 1.4426950408889634
SB_Q_SCALE = -SCALE * LOG2_E
LOG2_F32_UNDERFLOW = -151.0
NEG_BIG = -1e30

F32 = jnp.float32
BF16 = jnp.bfloat16


def _dot(a, b):
    return jnp.dot(a, b, preferred_element_type=F32)


def _dot_nt(a, b):
    return lax.dot_general(a, b, (((1,), (1,)), ((), ())), preferred_element_type=F32)


def _split_hi_lo(x):
    hi = x.astype(BF16)
    lo = (x - hi.astype(F32)).astype(BF16)
    return hi, lo


def _rms_norm_rows(x, g):
    ms = jnp.mean(x * x, axis=-1, keepdims=True)
    return x * lax.rsqrt(ms + EPS) * g


def _resident(shape):
    return pl.BlockSpec(shape, lambda *_: (0,) * len(shape), pipeline_mode=pl.Buffered(1))


def _resident_layer(shape, layer):
    return pl.BlockSpec((None,) + tuple(shape), lambda *_: (layer,) + (0,) * len(shape),
                        pipeline_mode=pl.Buffered(1))


def _params(n_axes):
    return pltpu.CompilerParams(dimension_semantics=("arbitrary",) * n_axes,
                                vmem_limit_bytes=VMEM_LIMIT_BYTES)


def _split3(x):
    hi = x.astype(BF16).astype(F32)
    rem = x - hi
    mid = rem.astype(BF16).astype(F32)
    return hi, mid, rem - mid


def _proj_kernel(h_ref, g_ref, wm_ref, wf_ref, main_ref, f_ref):
    xn = _rms_norm_rows(h_ref[...], g_ref[...]).astype(BF16)
    for c in range(N_MAIN // COL_CHUNK):
        cs = slice(c * COL_CHUNK, (c + 1) * COL_CHUNK)
        y = _dot(xn, wm_ref[:, cs])
        if c == COL_SB_Q:
            y = y * SB_Q_SCALE
        main_ref[:, cs] = y.astype(BF16)
    f_ref[...] = _dot(xn, wf_ref[...])


def _proj(h, g, w_in_all, layer, w_f):
    t = h.shape[0]
    return pl.pallas_call(
        _proj_kernel,
        grid=(t // TOKEN_TILE,),
        in_specs=[pl.BlockSpec((TOKEN_TILE, D_MODEL), lambda i: (i, 0)),
                  _resident((1, D_MODEL)),
                  _resident_layer((D_MODEL, N_MAIN), layer),
                  _resident((D_MODEL, LANES))],
        out_specs=[pl.BlockSpec((TOKEN_TILE, N_MAIN), lambda i: (i, 0)),
                   pl.BlockSpec((TOKEN_TILE, LANES), lambda i: (i, 0))],
        out_shape=[jax.ShapeDtypeStruct((t, N_MAIN), BF16),
                   jax.ShapeDtypeStruct((t, LANES), F32)],
        compiler_params=_params(1),
        name="proj",
    )(h, g, w_in_all, w_f)


def _convgmlp_kernel(cb_ref, cc_ref, cx_ref, u_ref, v_ref, wconv_ref, ws_ref, bs_ref,
                     lng_ref, lnb_ref, o_ref):
    seq = cb_ref.shape[0]
    n_chunks = seq // GMLP_CHUNK
    row = lax.broadcasted_iota(jnp.int32, (GMLP_CHUNK, GMLP_CHUNK), 0)
    col = lax.broadcasted_iota(jnp.int32, (GMLP_CHUNK, GMLP_CHUNK), 1)
    w_tril = [jnp.where(col <= row, ws_ref[gi], 0.0).astype(BF16) for gi in range(GMLP_GROUPS)]
    lane = lax.broadcasted_iota(jnp.int32, (GMLP_CHUNK, LANES), 1)
    first_group = lane < GROUP_WIDTH
    w0 = wconv_ref[0:1, :]
    w1 = wconv_ref[1:2, :]
    w2 = wconv_ref[2:3, :]

    def chunk(c, prev_tail):
        rows = pl.ds(pl.multiple_of(c * GMLP_CHUNK, GMLP_CHUNK), GMLP_CHUNK)
        xc = cc_ref[rows, :].astype(F32) * cx_ref[rows, :].astype(F32)
        win = jnp.concatenate([prev_tail, xc], axis=0)
        xc1 = pltpu.roll(win, 1, 0)[SUBLANES:, :]
        xc2 = pltpu.roll(win, 2, 0)[SUBLANES:, :]
        ya = cb_ref[rows, :].astype(F32) * (w0 * xc2 + w1 * xc1 + w2 * xc)
        o_ref[rows, 0:MIX_WIDTH] = ya.astype(BF16)

        gu = jax.nn.gelu(u_ref[rows, :].astype(F32))
        gv = jax.nn.gelu(v_ref[rows, :].astype(F32))
        mu = jnp.mean(gv, axis=-1, keepdims=True)
        cen = gv - mu
        var = jnp.mean(cen * cen, axis=-1, keepdims=True)
        vn = (cen * lax.rsqrt(var + EPS) * lng_ref[...] + lnb_ref[...]).astype(BF16)
        halves = []
        for lb in range(MIX_WIDTH // LANES):
            vb = vn[:, lb * LANES:(lb + 1) * LANES]
            m0 = _dot(w_tril[2 * lb], vb)
            m1 = _dot(w_tril[2 * lb + 1], vb)
            halves.append(jnp.where(first_group, m0, m1))
        mixed = jnp.concatenate(halves, axis=1) + bs_ref[...]
        o_ref[rows, MIX_WIDTH:2 * MIX_WIDTH] = (gu * mixed).astype(BF16)
        return xc[GMLP_CHUNK - SUBLANES:, :]

    def step(c0, tail):
        for n in range(GMLP_CHUNKS_PER_STEP):
            tail = chunk(GMLP_CHUNKS_PER_STEP * c0 + n, tail)
        return tail

    lax.fori_loop(0, n_chunks // GMLP_CHUNKS_PER_STEP, step, jnp.zeros((SUBLANES, MIX_WIDTH), F32))


def _convgmlp(main3, w_conv, w_s, bs_rows, ln_g, ln_b):
    b, s, _ = main3.shape
    col = lambda k: pl.BlockSpec((None, s, MIX_WIDTH), lambda i, k=k: (i, 0, k))
    return pl.pallas_call(
        _convgmlp_kernel,
        grid=(b,),
        in_specs=[col(COL_CONV_B), col(COL_CONV_C), col(COL_CONV_X), col(COL_GMLP_U), col(COL_GMLP_V),
                  _resident((CONV_K, MIX_WIDTH)),
                  _resident((GMLP_GROUPS, GMLP_CHUNK, GMLP_CHUNK)),
                  _resident((GMLP_CHUNK, MIX_WIDTH)),
                  _resident((1, MIX_WIDTH)),
                  _resident((1, MIX_WIDTH))],
        out_specs=pl.BlockSpec((None, s, 2 * MIX_WIDTH), lambda i: (i, 0, 0)),
        out_shape=jax.ShapeDtypeStruct((b, s, 2 * MIX_WIDTH), BF16),
        compiler_params=_params(1),
        name="convgmlp",
    )(main3, main3, main3, main3, main3, w_conv, w_s, bs_rows, ln_g, ln_b)


def _pair(x, h):
    p = (h * HEAD_DIM) // LANES
    return x[:, p * LANES:(p + 1) * LANES]


def _head_in_pair_mask(rows, h):
    lane = lax.broadcasted_iota(jnp.int32, (rows, LANES), 1)
    first = lane < HEAD_DIM
    return first if (h * HEAD_DIM) % LANES == 0 else jnp.logical_not(first)


def _masked_head(x, h):
    xp = _pair(x, h)
    return jnp.where(_head_in_pair_mask(x.shape[0], h), xp, jnp.zeros_like(xp))


def _merge_heads(per_head):
    rows = per_head[0].shape[0]
    blocks = []
    for p in range(MIX_WIDTH // LANES):
        h0 = p * (LANES // HEAD_DIM)
        blocks.append(jnp.where(_head_in_pair_mask(rows, h0), per_head[h0], per_head[h0 + 1]))
    return jnp.concatenate(blocks, axis=1)


def _sb_kernel(q_ref, k_ref, v_ref, o_ref, carry_scr, acc_scr):
    step_id = pl.program_id(1)
    r = lax.broadcasted_iota(jnp.int32, (ATTN_TK, ATTN_TK), 0)
    c = lax.broadcasted_iota(jnp.int32, (ATTN_TK, ATTN_TK), 1)
    suffix = jnp.where(r > c, 1.0, 0.0).astype(BF16)
    rows_all = N_HEADS * ATTN_TQ
    rq = lax.broadcasted_iota(jnp.int32, (rows_all, ATTN_TK), 0) & (ATTN_TQ - 1)
    cq = lax.broadcasted_iota(jnp.int32, (rows_all, ATTN_TK), 1)
    strict = cq < rq
    pair_rows = HEADS_PER_PAIR * ATTN_TQ

    def make_group(u, i):
        q = q_ref[pl.ds(pl.multiple_of(i * ATTN_TQ, ATTN_TQ), ATTN_TQ), :]
        qneg = [jnp.concatenate([_masked_head(q, HEADS_PER_PAIR * p + e) for e in range(HEADS_PER_PAIR)],
                                axis=0) for p in range(N_PAIRS)]

        def group(blocks, first):
            for pairs in ([[p] for p in range(N_PAIRS)] if first else [list(range(N_PAIRS))]):
                n_rows = len(pairs) * pair_rows
                rows_p = slice(pairs[0] * pair_rows, pairs[0] * pair_rows + n_rows)
                strict_p = strict[:n_rows, :]
                carry = jnp.zeros((n_rows, 1), F32) if first else carry_scr[u, rows_p, :]
                staged = []
                for j, valid, diag in blocks:
                    ks = pl.ds(pl.multiple_of(j * ATTN_TK, ATTN_TK), ATTN_TK)
                    zn2 = jnp.concatenate(
                        [_dot_nt(qneg[p], k_ref[ks, p * LANES:(p + 1) * LANES]) for p in pairs], axis=0)
                    l2 = jnp.minimum(zn2, 0.0) - jnp.log2(1.0 + jnp.exp2(-jnp.abs(zn2)))
                    if diag:
                        l2 = jnp.where(strict_p, l2, 0.0)
                    staged.append((ks, valid, diag, l2 - zn2, _dot(l2.astype(BF16), suffix),
                                   jnp.sum(l2, axis=1, keepdims=True)))
                pv = None
                for ks, valid, diag, log2_beta, later_in, row_sum in staged:
                    if valid is None:
                        w = jnp.exp2(later_in + carry + log2_beta)
                    else:
                        w = jnp.exp2(later_in + (carry + jnp.where(valid, 0.0, NEG_BIG)) + log2_beta)
                        row_sum = jnp.where(valid, row_sum, 0.0)
                    if diag:
                        w = jnp.where(strict_p, w, 0.0)
                    wb = w.astype(BF16)
                    contrib = jnp.concatenate(
                        [_dot(wb[x * pair_rows:(x + 1) * pair_rows, :], v_ref[ks, p * LANES:(p + 1) * LANES])
                         for x, p in enumerate(pairs)], axis=0)
                    pv = contrib if pv is None else pv + contrib
                    carry = carry + row_sum
                if first:
                    acc_scr[u, rows_p, :] = pv
                else:
                    acc_scr[u, rows_p, :] += pv
                carry_scr[u, rows_p, :] = carry

        return group

    query_blocks = [(u, SB_QBLOCKS * step_id + u) for u in range(SB_QBLOCKS)]
    groups = [make_group(u, i) for u, i in query_blocks]
    for (u, i), group in zip(query_blocks, groups):
        group([(i, None, True), (jnp.maximum(i - 1, 0), i >= 1, False)], True)

    for (u, i), group in zip(query_blocks, groups):
        def alive(u=u):
            return jnp.max(carry_scr[u]) > LOG2_F32_UNDERFLOW

        first_left = i - 2
        n_groups = i // 2

        def step(st, group=group, alive=alive, first_left=first_left):
            t, _ = st
            j_near = first_left - 2 * t
            group([(j_near, None, False), (jnp.maximum(j_near - 1, 0), j_near >= 1, False)], False)
            return t + 1, alive()

        lax.while_loop(lambda st, n_groups=n_groups: (st[0] < n_groups) & st[1], step, (jnp.int32(0), alive()))
        acc = acc_scr[u]
        o_ref[u * ATTN_TQ:(u + 1) * ATTN_TQ, :] = _merge_heads(
            [acc[h * ATTN_TQ:(h + 1) * ATTN_TQ, :] for h in range(N_HEADS)]).astype(BF16)


def _sb_attention(main3):
    b, s, _ = main3.shape
    step_rows = SB_QBLOCKS * ATTN_TQ
    col = lambda k: pl.BlockSpec((None, s, MIX_WIDTH), lambda bi, i, k=k: (bi, 0, k))
    return pl.pallas_call(
        _sb_kernel,
        grid=(b, s // step_rows),
        in_specs=[col(COL_SB_Q), col(COL_SB_K), col(COL_SB_V)],
        out_specs=pl.BlockSpec((None, step_rows, MIX_WIDTH), lambda bi, i: (bi, i, 0)),
        out_shape=jax.ShapeDtypeStruct((b, s, MIX_WIDTH), BF16),
        scratch_shapes=[pltpu.VMEM((SB_QBLOCKS, N_HEADS * ATTN_TQ, 1), F32),
                        pltpu.VMEM((SB_QBLOCKS, N_HEADS * ATTN_TQ, LANES), F32)],
        compiler_params=_params(2),
        name="sb",
    )(main3, main3, main3)


def _fox_kernel(q_ref, k_ref, v_ref, f_ref, bf_ref, qg_ref, kg_ref, o_ref,
                qn_scr, kn_scr, qaug_scr, kaug_scr, s_scr, mrun_scr, lrun_scr, acc_scr):
    step_id = pl.program_id(1)
    seq = q_ref.shape[0]

    @pl.when(step_id == 0)
    def _prepare():
        r = lax.broadcasted_iota(jnp.int32, (MIX_WIDTH, MIX_WIDTH), 0) // HEAD_DIM
        c = lax.broadcasted_iota(jnp.int32, (MIX_WIDTH, MIX_WIDTH), 1) // HEAD_DIM
        same_head = jnp.where(r == c, 1.0, 0.0).astype(BF16)
        tr = lax.broadcasted_iota(jnp.int32, (FOX_PREP_ROWS, FOX_PREP_ROWS), 0)
        tc = lax.broadcasted_iota(jnp.int32, (FOX_PREP_ROWS, FOX_PREP_ROWS), 1)
        prefix = jnp.where(tc <= tr, 1.0, 0.0).astype(BF16)
        lane = lax.broadcasted_iota(jnp.int32, (FOX_PREP_ROWS, LANES), 1)
        n = N_HEADS

        def tile(tix, run):
            rows = pl.ds(pl.multiple_of(tix * FOX_PREP_ROWS, FOX_PREP_ROWS), FOX_PREP_ROWS)
            for x_ref, g_ref, scale, dst in ((q_ref, qg_ref, SCALE, qn_scr), (k_ref, kg_ref, 1.0, kn_scr)):
                x = x_ref[rows, :].astype(F32)
                hi, lo = _split_hi_lo(x * x)
                ms = (_dot(hi, same_head) + _dot(lo, same_head)) * (1.0 / HEAD_DIM)
                dst[rows, :] = (x * lax.rsqrt(ms + EPS) * (g_ref[...] * scale)).astype(BF16)
            hi, lo = _split_hi_lo(jax.nn.log_sigmoid(f_ref[rows, :] + bf_ref[...]))
            cum = run + _dot(prefix, hi) + _dot(prefix, lo)
            c_hi, c_mid, c_lo = _split3(cum)
            q_aug = jnp.where(lane < n, c_hi, jnp.where(lane < 2 * n, c_mid, jnp.where(
                lane < 3 * n, c_lo, jnp.where(lane < 6 * n, 1.0, 0.0))))
            k_aug = jnp.where(lane < 3 * n, 1.0, jnp.where(lane < 4 * n, -c_hi, jnp.where(
                lane < 5 * n, -c_mid, jnp.where(lane < 6 * n, -c_lo, 0.0))))
            qaug_scr[rows, :] = q_aug.astype(BF16)
            kaug_scr[rows, :] = k_aug.astype(BF16)
            return cum[FOX_PREP_ROWS - 1:FOX_PREP_ROWS, :]

        lax.fori_loop(0, seq // FOX_PREP_ROWS, tile, jnp.zeros((1, LANES), F32))

    rows_all = N_HEADS * ATTN_TQ
    pair_rows = HEADS_PER_PAIR * ATTN_TQ
    lane = lax.broadcasted_iota(jnp.int32, (ATTN_TQ, LANES), 1)
    r = lax.broadcasted_iota(jnp.int32, (rows_all, ATTN_TK), 0) & (ATTN_TQ - 1)
    c = lax.broadcasted_iota(jnp.int32, (rows_all, ATTN_TK), 1)
    causal = c <= r
    query_blocks = [(u, FOX_QBLOCKS * step_id + u) for u in range(FOX_QBLOCKS)]

    def key_rows(j):
        return pl.ds(pl.multiple_of(j * ATTN_TK, ATTN_TK), ATTN_TK)

    def fold(x):
        return [x[:, n * LANES:(n + 1) * LANES] for n in range(ATTN_TK // LANES)]

    def make_scores(i):
        qrows = key_rows(i)
        q = qn_scr[qrows, :]
        qaug = qaug_scr[qrows, :]
        qs = []
        for p in range(N_PAIRS):
            rows = []
            for e in range(HEADS_PER_PAIR):
                h = HEADS_PER_PAIR * p + e
                aug_h = jnp.where((lane & (N_HEADS - 1)) == h, qaug, jnp.zeros_like(qaug))
                rows.append(jnp.concatenate([_masked_head(q, h), aug_h], axis=1))
            qs.append(jnp.concatenate(rows, axis=0))

        def scores(j):
            ks = key_rows(j)
            kaug = kaug_scr[ks, :]
            return jnp.concatenate(
                [_dot_nt(qs[p], jnp.concatenate([kn_scr[ks, p * LANES:(p + 1) * LANES], kaug], axis=1))
                 for p in range(N_PAIRS)], axis=0) * LOG2_E

        return scores

    def accumulate(blocks):
        l_parts, pv = [], None
        for j, shifted in blocks:
            ks = key_rows(j)
            p = jnp.exp2(shifted)
            l_parts += fold(p)
            pb = p.astype(BF16)
            contrib = jnp.concatenate(
                [_dot(pb[pr * pair_rows:(pr + 1) * pair_rows, :], v_ref[ks, pr * LANES:(pr + 1) * LANES])
                 for pr in range(N_PAIRS)], axis=0)
            pv = contrib if pv is None else pv + contrib
        return functools.reduce(jnp.add, l_parts), pv

    assert FOX_QBLOCKS % 2 == 0
    i0 = FOX_QBLOCKS * step_id
    head_blocks = [[i0 + v for v in range(u, -1, -1)] for u in range(FOX_QBLOCKS)]
    n_pair_steps = (FOX_QBLOCKS // 2) * step_id
    score_fns = [make_scores(i) for _, i in query_blocks]

    for (u, i), scores in zip(query_blocks, score_fns):
        parts = []
        for n, j in enumerate(head_blocks[u]):
            s = jnp.where(causal, scores(j), NEG_BIG) if n == 0 else scores(j)
            s_scr[u, j] = s
            parts += fold(s)
        mrun_scr[u] = functools.reduce(jnp.maximum, parts)

    def stage(t, _):
        for (u, _), scores in zip(query_blocks, score_fns):
            parts = []
            for j in (2 * t, 2 * t + 1):
                s = scores(j)
                s_scr[u, j] = s
                parts += fold(s)
            mrun_scr[u] = functools.reduce(jnp.maximum, parts, mrun_scr[u])
        return 0

    lax.fori_loop(0, n_pair_steps, stage, 0)

    row_max = []
    for u, _ in query_blocks:
        m = jnp.max(mrun_scr[u], axis=1, keepdims=True)
        l, pv = accumulate([(j, s_scr[u, j] - m) for j in head_blocks[u]])
        lrun_scr[u] = l
        acc_scr[u] = pv
        row_max.append(m)

    def weigh(t, _):
        for (u, _), m in zip(query_blocks, row_max):
            l, pv = accumulate([(j, s_scr[u, j] - m) for j in (2 * t, 2 * t + 1)])
            lrun_scr[u] += l
            acc_scr[u] += pv
        return 0

    lax.fori_loop(0, n_pair_steps, weigh, 0)

    for u, _ in query_blocks:
        out = acc_scr[u] / jnp.sum(lrun_scr[u], axis=1, keepdims=True)
        o_ref[u * ATTN_TQ:(u + 1) * ATTN_TQ, :] = _merge_heads(
            [out[h * ATTN_TQ:(h + 1) * ATTN_TQ, :] for h in range(N_HEADS)]).astype(BF16)


def _fox_attention(main3, f3, bf_row, qg_row, kg_row):
    b, s, _ = main3.shape
    rows_all = N_HEADS * ATTN_TQ
    step_rows = FOX_QBLOCKS * ATTN_TQ
    col = lambda k: pl.BlockSpec((None, s, MIX_WIDTH), lambda bi, i, k=k: (bi, 0, k))
    return pl.pallas_call(
        _fox_kernel,
        grid=(b, s // step_rows),
        in_specs=[col(COL_FOX_Q), col(COL_FOX_K), col(COL_FOX_V),
                  pl.BlockSpec((None, s, LANES), lambda bi, i: (bi, 0, 0)),
                  _resident((1, LANES)),
                  _resident((1, MIX_WIDTH)),
                  _resident((1, MIX_WIDTH))],
        out_specs=pl.BlockSpec((None, step_rows, MIX_WIDTH), lambda bi, i: (bi, i, 0)),
        out_shape=jax.ShapeDtypeStruct((b, s, MIX_WIDTH), BF16),
        scratch_shapes=[pltpu.VMEM((s, MIX_WIDTH), BF16),
                        pltpu.VMEM((s, MIX_WIDTH), BF16),
                        pltpu.VMEM((s, LANES), BF16),
                        pltpu.VMEM((s, LANES), BF16),
                        pltpu.VMEM((FOX_QBLOCKS, s // ATTN_TK, rows_all, ATTN_TK), F32),
                        pltpu.VMEM((FOX_QBLOCKS, rows_all, LANES), F32),
                        pltpu.VMEM((FOX_QBLOCKS, rows_all, LANES), F32),
                        pltpu.VMEM((FOX_QBLOCKS, rows_all, LANES), F32)],
        compiler_params=_params(2),
        name="fox",
    )(main3, main3, main3, f3, bf_row, qg_row, kg_row)


def _merge_kernel(h_ref, g_ref, yab_ref, yc_ref, yd_ref, wg_ref, wb_ref, wo_ref, o_ref, merged_scr):
    xn = _rms_norm_rows(h_ref[...], g_ref[...]).astype(BF16)
    ys = [yab_ref[:, 0:MIX_WIDTH], yab_ref[:, MIX_WIDTH:2 * MIX_WIDTH], yc_ref[...], yd_ref[...]]
    for c in range(D_MODEL // COL_CHUNK):
        cs = slice(c * COL_CHUNK, (c + 1) * COL_CHUNK)
        acc = None
        for n in range(N_BRANCH):
            gs = slice(n * D_MODEL + c * COL_CHUNK, n * D_MODEL + (c + 1) * COL_CHUNK)
            term = jax.nn.sigmoid(_dot(xn, wg_ref[:, gs])) * _dot(ys[n], wb_ref[n, :, cs])
            acc = term if acc is None else acc + term
        merged_scr[:, cs] = acc.astype(BF16)
    merged = merged_scr[...]
    for c in range(D_MODEL // COL_CHUNK):
        cs = slice(c * COL_CHUNK, (c + 1) * COL_CHUNK)
        o_ref[:, cs] = h_ref[:, cs] + _dot(merged, wo_ref[:, cs])


def _merge(h, g, yab, yc, yd, w_gate, w_branch, w_out, layer):
    t = h.shape[0]
    rows = lambda w: pl.BlockSpec((TOKEN_TILE, w), lambda i: (i, 0))
    return pl.pallas_call(
        _merge_kernel,
        grid=(t // TOKEN_TILE,),
        in_specs=[rows(D_MODEL), _resident((1, D_MODEL)),
                  rows(2 * MIX_WIDTH), rows(MIX_WIDTH), rows(MIX_WIDTH),
                  _resident_layer((D_MODEL, N_BRANCH * D_MODEL), layer),
                  _resident_layer((N_BRANCH, MIX_WIDTH, D_MODEL), layer),
                  _resident_layer((D_MODEL, D_MODEL), layer)],
        out_specs=rows(D_MODEL),
        out_shape=jax.ShapeDtypeStruct((t, D_MODEL), F32),
        scratch_shapes=[pltpu.VMEM((TOKEN_TILE, D_MODEL), BF16)],
        compiler_params=_params(1),
        name="merge",
    )(h, g, yab, yc, yd, w_gate, w_branch, w_out)


def _ffn_kernel(h_ref, g_ref, wi_ref, wo_ref, o_ref, act_scr):
    xn = _rms_norm_rows(h_ref[...], g_ref[...]).astype(BF16)
    for c in range(FFN_HIDDEN // COL_CHUNK):
        cs = slice(c * COL_CHUNK, (c + 1) * COL_CHUNK)
        gate = _dot(xn, wi_ref[:, cs])
        up = _dot(xn, wi_ref[:, FFN_HIDDEN + c * COL_CHUNK:FFN_HIDDEN + (c + 1) * COL_CHUNK])
        act_scr[:, cs] = (jax.nn.silu(gate) * up).astype(BF16)
    act = act_scr[...]
    for c in range(D_MODEL // COL_CHUNK):
        cs = slice(c * COL_CHUNK, (c + 1) * COL_CHUNK)
        o_ref[:, cs] = h_ref[:, cs] + _dot(act, wo_ref[:, cs])


def _ffn(h, g, w_in, w_out, layer):
    t = h.shape[0]
    rows = pl.BlockSpec((TOKEN_TILE, D_MODEL), lambda i: (i, 0))
    return pl.pallas_call(
        _ffn_kernel,
        grid=(t // TOKEN_TILE,),
        in_specs=[rows, _resident((1, D_MODEL)),
                  _resident_layer((D_MODEL, 2 * FFN_HIDDEN), layer),
                  _resident_layer((FFN_HIDDEN, D_MODEL), layer)],
        out_specs=rows,
        out_shape=jax.ShapeDtypeStruct((t, D_MODEL), F32),
        scratch_shapes=[pltpu.VMEM((TOKEN_TILE, FFN_HIDDEN), BF16)],
        compiler_params=_params(1),
        name="ffn",
    )(h, g, w_in, w_out)


def kernel(x, norm_mix_g, w_in, w_conv, w_spatial, b_spatial, gmlp_ln_g, gmlp_ln_b,
           fox_q_norm_g, fox_k_norm_g, fox_forget_b, w_branch, w_out, norm_ffn_g,
           w_ffn_in, w_ffn_out):
    b, s, d = x.shape
    depth = w_in.shape[0]
    assert d == D_MODEL and ATTN_TQ == ATTN_TK
    for rows in (SB_QBLOCKS * ATTN_TQ, FOX_QBLOCKS * ATTN_TQ, GMLP_CHUNKS_PER_STEP * GMLP_CHUNK, FOX_PREP_ROWS,
                 TOKEN_TILE):
        assert s % rows == 0, (s, rows)
    t = b * s
    h = x.reshape(t, d)
    w_in_b = jnp.transpose(lax.optimization_barrier(jnp.transpose(w_in, (2, 0, 1)).astype(BF16)), (1, 2, 0))
    w_gate_b = w_in_b[:, :, N_MAIN + N_HEADS:]
    w_branch_b = w_branch.astype(BF16)
    w_out_b = w_out.astype(BF16)
    w_ffn_in_b = w_ffn_in.astype(BF16)
    w_ffn_out_b = w_ffn_out.astype(BF16)
    n_f = FORGET_COPIES * N_HEADS
    for l in range(depth):
        w_f = jnp.pad(jnp.tile(w_in_b[l, :, N_MAIN:N_MAIN + N_HEADS], (1, FORGET_COPIES)),
                      ((0, 0), (0, LANES - n_f)))
        bf_row = jnp.pad(jnp.tile(fox_forget_b[l], FORGET_COPIES), (0, LANES - n_f)).reshape(1, LANES)
        qg_row = jnp.tile(fox_q_norm_g[l], N_HEADS).reshape(1, MIX_WIDTH)
        kg_row = jnp.tile(fox_k_norm_g[l], N_HEADS).reshape(1, MIX_WIDTH)
        bs_rows = jnp.repeat(b_spatial[l].T, GROUP_WIDTH, axis=1)

        main, f_raw = _proj(h, norm_mix_g[l].reshape(1, d), w_in_b, l, w_f)
        main3 = main.reshape(b, s, N_MAIN)
        yab = _convgmlp(main3, w_conv[l], w_spatial[l], bs_rows,
                        gmlp_ln_g[l].reshape(1, MIX_WIDTH), gmlp_ln_b[l].reshape(1, MIX_WIDTH))
        yc = _sb_attention(main3)
        yd = _fox_attention(main3, f_raw.reshape(b, s, LANES), bf_row, qg_row, kg_row)
        h = _merge(h, norm_mix_g[l].reshape(1, d), yab.reshape(t, 2 * MIX_WIDTH),
                   yc.reshape(t, MIX_WIDTH), yd.reshape(t, MIX_WIDTH),
                   w_gate_b, w_branch_b, w_out_b, l)
        h = _ffn(h, norm_ffn_g[l].reshape(1, d), w_ffn_in_b, w_ffn_out_b, l)
    return h.reshape(b, s, d)
```
